```python
import math
import jax, jax.numpy as jnp
from jax import lax
import numpy as np

D_MODEL = 2048
BATCH = 8
SEQ = 8192
DEPTH = 4

CHUNK = 64
Q_BLOCK = 128

N_HEADS = 16
Q_LORA = 512
KV_LORA = 512
QK_NOPE = 128
QK_ROPE = 64
V_HEAD = 128
MLA_WIDTH = N_HEADS * V_HEAD
ROPE_THETA = 10000.0

LRU_WIDTH = D_MODEL
LRU_BLOCKS = 16
LRU_BLOCK_DIM = LRU_WIDTH // LRU_BLOCKS
CONV_K = 4
LRU_C = 8.0

PLE_DIM = 256

EPS = 1e-6

IN_SIZES = (Q_LORA, KV_LORA, QK_ROPE, MLA_WIDTH, LRU_WIDTH, LRU_WIDTH, D_MODEL, D_MODEL)
IN_TOTAL = sum(IN_SIZES)
IN_SPLITS = tuple(int(s) for s in np.cumsum(IN_SIZES)[:-1])

kernel_name = "hybrid_mla_rglru_gated_merge"


def rms_norm(x, g):
    xf = x.astype(jnp.float32)
    y = xf * lax.rsqrt(jnp.mean(xf * xf, axis=-1, keepdims=True) + EPS)
    return (y * g.astype(jnp.float32)).astype(x.dtype)


def rope_tables(positions):
    inv_freq = ROPE_THETA ** (-jnp.arange(0, QK_ROPE, 2, dtype=jnp.float32) / QK_ROPE)
    ang = positions.astype(jnp.float32)[..., None] * inv_freq
    return jnp.cos(ang), jnp.sin(ang)


def apply_rope(x, cos, sin):
    x1, x2 = jnp.split(x, 2, axis=-1)
    c = cos.astype(x.dtype)
    s = sin.astype(x.dtype)
    return jnp.concatenate([x1 * c - x2 * s, x2 * c + x1 * s], axis=-1)


def mla_branch(q_lat, kv_lat, k_rope_raw, gate, cos, sin, q_a_norm, w_q_b, kv_a_norm, w_kv_b, w_o):
    B, S, _ = q_lat.shape
    q = (rms_norm(q_lat, q_a_norm) @ w_q_b).reshape(B, S, N_HEADS, QK_NOPE + QK_ROPE)
    q_nope, q_rope = q[..., :QK_NOPE], q[..., QK_NOPE:]
    q_rope = apply_rope(q_rope, cos[:, :, None, :], sin[:, :, None, :])
    kv = (rms_norm(kv_lat, kv_a_norm) @ w_kv_b).reshape(B, S, N_HEADS, QK_NOPE + V_HEAD)
    k_nope, v = kv[..., :QK_NOPE], kv[..., QK_NOPE:]
    k_rope = apply_rope(k_rope_raw, cos, sin)
    scale = 1.0 / math.sqrt(QK_NOPE + QK_ROPE)
    n_blk = S // Q_BLOCK
    k_chunk = jnp.arange(S) // CHUNK

    qn_b = q_nope.reshape(B, n_blk, Q_BLOCK, N_HEADS, QK_NOPE).transpose(1, 0, 2, 3, 4)
    qr_b = q_rope.reshape(B, n_blk, Q_BLOCK, N_HEADS, QK_ROPE).transpose(1, 0, 2, 3, 4)

    def attend(args):
        qn, qr, blk = args
        s = (jnp.einsum('bqhd,bkhd->bhqk', qn, k_nope)
             + jnp.einsum('bqhr,bkr->bhqk', qr, k_rope)).astype(jnp.float32) * scale
        q_chunk = (blk * Q_BLOCK + jnp.arange(Q_BLOCK)) // CHUNK
        mask = k_chunk[None, :] <= q_chunk[:, None]
        s = jnp.where(mask[None, None], s, -jnp.inf)
        pr = jax.nn.softmax(s, axis=-1).astype(v.dtype)
        return jnp.einsum('bhqk,bkhd->bqhd', pr, v)

    o = lax.map(attend, (qn_b, qr_b, jnp.arange(n_blk)))
    o = o.transpose(1, 0, 2, 3, 4).reshape(B, S, MLA_WIDTH)
    return (o * jax.nn.silu(gate)) @ w_o


def causal_dwconv(x, w, b):
    S = x.shape[1]
    xp = jnp.pad(x, ((0, 0), (CONV_K - 1, 0), (0, 0)))
    y = xp[:, 0:S, :] * w[0]
    for k in range(1, CONV_K):
        y = y + xp[:, k:k + S, :] * w[k]
    return y + b


def rglru_branch(u, gate, conv_w, conv_b, w_rg, b_rg, w_ig, b_ig, lam, w_o):
    B, S, _ = u.shape
    xc = causal_dwconv(u, conv_w, conv_b)
    xh = xc.reshape(B, S, LRU_BLOCKS, LRU_BLOCK_DIM)
    r = jax.nn.sigmoid(jnp.einsum('bshi,hij->bshj', xh, w_rg).reshape(B, S, LRU_WIDTH) + b_rg)
    i = jax.nn.sigmoid(jnp.einsum('bshi,hij->bshj', xh, w_ig).reshape(B, S, LRU_WIDTH) + b_ig)
    log_a = (-LRU_C * r.astype(jnp.float32)) * jax.nn.softplus(-lam.astype(jnp.float32))
    a = jnp.exp(log_a)
    mult = jnp.sqrt(-jnp.expm1(2.0 * log_a))
    bterm = mult * (i * xc).astype(jnp.float32)

    def combine(lhs, rhs):
        a1, b1 = lhs
        a2, b2 = rhs
        return a1 * a2, a2 * b1 + b2

    _, h = lax.associative_scan(combine, (a, bterm), axis=1)
    h = h.astype(u.dtype)
    return (h * jax.nn.silu(gate)) @ w_o


def _fwd_setup_inputs(seed: int = 0) -> dict:
    key = jax.random.key(seed)
    ks = jax.random.split(key, 32)
    f32 = jnp.float32

    def nrm(k, shape, fan_in):
        return jax.random.normal(k, shape, f32) * (fan_in ** -0.5)

    def gain(k, shape):
        return 1.0 + 0.02 * jax.random.normal(k, shape, f32)

    x = jax.random.normal(ks[0], (BATCH, SEQ, D_MODEL), f32)
    p = jax.random.normal(ks[1], (DEPTH, BATCH, SEQ, PLE_DIM), f32)
    offset = jax.random.randint(ks[2], (BATCH, 1), 0, 4096, dtype=jnp.int32)
    positions = offset + jnp.arange(SEQ, dtype=jnp.int32)[None, :]

    u = jax.random.uniform(ks[3], (DEPTH, LRU_WIDTH), f32, 0.9, 0.999)
    a0 = u ** (1.0 / LRU_C)
    lru_lambda = jnp.log(a0) - jnp.log1p(-a0)

    return {
        "x": x,
        "p": p,
        "positions": positions,
        "attn_norm": gain(ks[4], (DEPTH, D_MODEL)),
        "w_in": nrm(ks[5], (DEPTH, D_MODEL, IN_TOTAL), D_MODEL),
        "q_a_norm": gain(ks[6], (DEPTH, Q_LORA)),
        "w_q_b": nrm(ks[7], (DEPTH, Q_LORA, N_HEADS * (QK_NOPE + QK_ROPE)), Q_LORA),
        "kv_a_norm": gain(ks[8], (DEPTH, KV_LORA)),
        "w_kv_b": nrm(ks[9], (DEPTH, KV_LORA, N_HEADS * (QK_NOPE + V_HEAD)), KV_LORA),
        "conv_w": nrm(ks[10], (DEPTH, CONV_K, LRU_WIDTH), CONV_K),
        "conv_b": 0.01 * jax.random.normal(ks[11], (DEPTH, LRU_WIDTH), f32),
        "w_rg": nrm(ks[12], (DEPTH, LRU_BLOCKS, LRU_BLOCK_DIM, LRU_BLOCK_DIM), LRU_BLOCK_DIM),
        "b_rg": 0.01 * jax.random.normal(ks[13], (DEPTH, LRU_WIDTH), f32),
        "w_ig": nrm(ks[14], (DEPTH, LRU_BLOCKS, LRU_BLOCK_DIM, LRU_BLOCK_DIM), LRU_BLOCK_DIM),
        "b_ig": 0.01 * jax.random.normal(ks[15], (DEPTH, LRU_WIDTH), f32),
        "lru_lambda": lru_lambda,
        "w_o_mla": nrm(ks[16], (DEPTH, MLA_WIDTH, D_MODEL), MLA_WIDTH),
        "w_o_lru": nrm(ks[17], (DEPTH, LRU_WIDTH, D_MODEL), LRU_WIDTH),
        "w_out": nrm(ks[18], (DEPTH, D_MODEL, D_MODEL), D_MODEL),
        "ple_norm": gain(ks[19], (DEPTH, D_MODEL)),
        "w_ple_gate": nrm(ks[20], (DEPTH, D_MODEL, D_MODEL), D_MODEL),
        "w_ple": nrm(ks[21], (DEPTH, PLE_DIM, D_MODEL), PLE_DIM),
        "final_norm": gain(ks[22], (D_MODEL,)),
    }


def _fwd_reference(x, p, positions, attn_norm, w_in, q_a_norm, w_q_b, kv_a_norm, w_kv_b,
              conv_w, conv_b, w_rg, b_rg, w_ig, b_ig, lru_lambda,
              w_o_mla, w_o_lru, w_out, ple_norm, w_ple_gate, w_ple, final_norm):
    cos, sin = rope_tables(positions)
    for l in range(DEPTH):
        h = rms_norm(x, attn_norm[l])
        z = h @ w_in[l]
        q_lat, kv_lat, k_rope_raw, g_mla, u_lru, g_lru, m_mla, m_lru = jnp.split(z, IN_SPLITS, axis=-1)
        y_mla = mla_branch(q_lat, kv_lat, k_rope_raw, g_mla, cos, sin,
                           q_a_norm[l], w_q_b[l], kv_a_norm[l], w_kv_b[l], w_o_mla[l])
        y_lru = rglru_branch(u_lru, g_lru, conv_w[l], conv_b[l], w_rg[l], b_rg[l],
                             w_ig[l], b_ig[l], lru_lambda[l], w_o_lru[l])
        merged = jax.nn.sigmoid(m_mla) * y_mla + jax.nn.sigmoid(m_lru) * y_lru
        x = x + merged @ w_out[l]
        ple_gate = jax.nn.sigmoid(rms_norm(x, ple_norm[l]) @ w_ple_gate[l])
        x = x + (p[l] @ w_ple[l]) * ple_gate
    return rms_norm(x, final_norm)


import jax as _jax
import jax.numpy as _jnp

TWIN_FORMAT = 'train_step'
FWD_PARAMS = ['x', 'p', 'positions', 'attn_norm', 'w_in', 'q_a_norm', 'w_q_b', 'kv_a_norm', 'w_kv_b', 'conv_w', 'conv_b', 'w_rg', 'b_rg', 'w_ig', 'b_ig', 'lru_lambda', 'w_o_mla', 'w_o_lru', 'w_out', 'ple_norm', 'w_ple_gate', 'w_ple', 'final_norm']
TWIN_WEIGHTS = ['attn_norm', 'w_in', 'q_a_norm', 'w_q_b', 'kv_a_norm', 'w_kv_b', 'conv_w', 'conv_b', 'w_rg', 'b_rg', 'w_ig', 'b_ig', 'lru_lambda', 'w_o_mla', 'w_o_lru', 'w_out', 'ple_norm', 'w_ple_gate', 'w_ple', 'final_norm']
TWIN_DIFF_INPUT = 'x'
TWIN_INPUTS = ['x', 'p', 'positions', 'attn_norm', 'w_in', 'q_a_norm', 'w_q_b', 'kv_a_norm', 'w_kv_b', 'conv_w', 'conv_b', 'w_rg', 'b_rg', 'w_ig', 'b_ig', 'lru_lambda', 'w_o_mla', 'w_o_lru', 'w_out', 'ple_norm', 'w_ple_gate', 'w_ple', 'final_norm', 'loss_target', 'm_attn_norm', 'm_w_in', 'm_q_a_norm', 'm_w_q_b', 'm_kv_a_norm', 'm_w_kv_b', 'm_conv_w', 'm_conv_b', 'm_w_rg', 'm_b_rg', 'm_w_ig', 'm_b_ig', 'm_lru_lambda', 'm_w_o_mla', 'm_w_o_lru', 'm_w_out', 'm_ple_norm', 'm_w_ple_gate', 'm_w_ple', 'm_final_norm', 'v_attn_norm', 'v_w_in', 'v_q_a_norm', 'v_w_q_b', 'v_kv_a_norm', 'v_w_kv_b', 'v_conv_w', 'v_conv_b', 'v_w_rg', 'v_b_rg', 'v_w_ig', 'v_b_ig', 'v_lru_lambda', 'v_w_o_mla', 'v_w_o_lru', 'v_w_out', 'v_ple_norm', 'v_w_ple_gate', 'v_w_ple', 'v_final_norm']
TWIN_OUTPUTS = ['loss', 'grad_x', 'grad_attn_norm', 'grad_w_in', 'grad_q_a_norm', 'grad_w_q_b', 'grad_kv_a_norm', 'grad_w_kv_b', 'grad_conv_w', 'grad_conv_b', 'grad_w_rg', 'grad_b_rg', 'grad_w_ig', 'grad_b_ig', 'grad_lru_lambda', 'grad_w_o_mla', 'grad_w_o_lru', 'grad_w_out', 'grad_ple_norm', 'grad_w_ple_gate', 'grad_w_ple', 'grad_final_norm', 'delta_attn_norm', 'delta_w_in', 'delta_q_a_norm', 'delta_w_q_b', 'delta_kv_a_norm', 'delta_w_kv_b', 'delta_conv_w', 'delta_conv_b', 'delta_w_rg', 'delta_b_rg', 'delta_w_ig', 'delta_b_ig', 'delta_lru_lambda', 'delta_w_o_mla', 'delta_w_o_lru', 'delta_w_out', 'delta_ple_norm', 'delta_w_ple_gate', 'delta_w_ple', 'delta_final_norm', 'new_m_attn_norm', 'new_m_w_in', 'new_m_q_a_norm', 'new_m_w_q_b', 'new_m_kv_a_norm', 'new_m_w_kv_b', 'new_m_conv_w', 'new_m_conv_b', 'new_m_w_rg', 'new_m_b_rg', 'new_m_w_ig', 'new_m_b_ig', 'new_m_lru_lambda', 'new_m_w_o_mla', 'new_m_w_o_lru', 'new_m_w_out', 'new_m_ple_norm', 'new_m_w_ple_gate', 'new_m_w_ple', 'new_m_final_norm', 'new_v_attn_norm', 'new_v_w_in', 'new_v_q_a_norm', 'new_v_w_q_b', 'new_v_kv_a_norm', 'new_v_w_kv_b', 'new_v_conv_w', 'new_v_conv_b', 'new_v_w_rg', 'new_v_b_rg', 'new_v_w_ig', 'new_v_b_ig', 'new_v_lru_lambda', 'new_v_w_o_mla', 'new_v_w_o_lru', 'new_v_w_out', 'new_v_ple_norm', 'new_v_w_ple_gate', 'new_v_w_ple', 'new_v_final_norm']
TWIN_LEAF_KINDS = {'loss': 'loss', 'grad_x': 'grad_x', 'grad_attn_norm': 'grad_w', 'grad_w_in': 'grad_w', 'grad_q_a_norm': 'grad_w', 'grad_w_q_b': 'grad_w', 'grad_kv_a_norm': 'grad_w', 'grad_w_kv_b': 'grad_w', 'grad_conv_w': 'grad_w', 'grad_conv_b': 'grad_w', 'grad_w_rg': 'grad_w', 'grad_b_rg': 'grad_w', 'grad_w_ig': 'grad_w', 'grad_b_ig': 'grad_w', 'grad_lru_lambda': 'grad_w', 'grad_w_o_mla': 'grad_w', 'grad_w_o_lru': 'grad_w', 'grad_w_out': 'grad_w', 'grad_ple_norm': 'grad_w', 'grad_w_ple_gate': 'grad_w', 'grad_w_ple': 'grad_w', 'grad_final_norm': 'grad_w', 'delta_attn_norm': 'delta_w', 'delta_w_in': 'delta_w', 'delta_q_a_norm': 'delta_w', 'delta_w_q_b': 'delta_w', 'delta_kv_a_norm': 'delta_w', 'delta_w_kv_b': 'delta_w', 'delta_conv_w': 'delta_w', 'delta_conv_b': 'delta_w', 'delta_w_rg': 'delta_w', 'delta_b_rg': 'delta_w', 'delta_w_ig': 'delta_w', 'delta_b_ig': 'delta_w', 'delta_lru_lambda': 'delta_w', 'delta_w_o_mla': 'delta_w', 'delta_w_o_lru': 'delta_w', 'delta_w_out': 'delta_w', 'delta_ple_norm': 'delta_w', 'delta_w_ple_gate': 'delta_w', 'delta_w_ple': 'delta_w', 'delta_final_norm': 'delta_w', 'new_m_attn_norm': 'new_m', 'new_m_w_in': 'new_m', 'new_m_q_a_norm': 'new_m', 'new_m_w_q_b': 'new_m', 'new_m_kv_a_norm': 'new_m', 'new_m_w_kv_b': 'new_m', 'new_m_conv_w': 'new_m', 'new_m_conv_b': 'new_m', 'new_m_w_rg': 'new_m', 'new_m_b_rg': 'new_m', 'new_m_w_ig': 'new_m', 'new_m_b_ig': 'new_m', 'new_m_lru_lambda': 'new_m', 'new_m_w_o_mla': 'new_m', 'new_m_w_o_lru': 'new_m', 'new_m_w_out': 'new_m', 'new_m_ple_norm': 'new_m', 'new_m_w_ple_gate': 'new_m', 'new_m_w_ple': 'new_m', 'new_m_final_norm': 'new_m', 'new_v_attn_norm': 'new_v', 'new_v_w_in': 'new_v', 'new_v_q_a_norm': 'new_v', 'new_v_w_q_b': 'new_v', 'new_v_kv_a_norm': 'new_v', 'new_v_w_kv_b': 'new_v', 'new_v_conv_w': 'new_v', 'new_v_conv_b': 'new_v', 'new_v_w_rg': 'new_v', 'new_v_b_rg': 'new_v', 'new_v_w_ig': 'new_v', 'new_v_b_ig': 'new_v', 'new_v_lru_lambda': 'new_v', 'new_v_w_o_mla': 'new_v', 'new_v_w_o_lru': 'new_v', 'new_v_w_out': 'new_v', 'new_v_ple_norm': 'new_v', 'new_v_w_ple_gate': 'new_v', 'new_v_w_ple': 'new_v', 'new_v_final_norm': 'new_v'}


def _forward(args):
    return _fwd_reference(*[args[k] for k in FWD_PARAMS])


def _output_shape():
    def fwd():
        inp = _fwd_setup_inputs(0)
        return _fwd_reference(*[inp[k] for k in FWD_PARAMS])
    out = _jax.eval_shape(fwd)
    return out.shape, out.dtype

N_MICROBATCH = 1
ADAM_LR = 0.001
ADAM_B1 = 0.9
ADAM_B2 = 0.999
ADAM_EPS = 1e-08
ADAM_WD = 0.01
ADAM_STEP = 10
PER_EXAMPLE_BATCH_AXIS = {'x': 0, 'p': 1, 'positions': 0, 'loss_target': 0}
SHARED_INPUTS = []
_WEIGHT_DTYPES = {'attn_norm': _jnp.float32, 'w_in': _jnp.float32, 'q_a_norm': _jnp.float32, 'w_q_b': _jnp.float32, 'kv_a_norm': _jnp.float32, 'w_kv_b': _jnp.float32, 'conv_w': _jnp.float32, 'conv_b': _jnp.float32, 'w_rg': _jnp.float32, 'b_rg': _jnp.float32, 'w_ig': _jnp.float32, 'b_ig': _jnp.float32, 'lru_lambda': _jnp.float32, 'w_o_mla': _jnp.float32, 'w_o_lru': _jnp.float32, 'w_out': _jnp.float32, 'ple_norm': _jnp.float32, 'w_ple_gate': _jnp.float32, 'w_ple': _jnp.float32, 'final_norm': _jnp.float32}
MOMENT_SCALE = {'attn_norm': 3.025765e-02, 'w_in': 1.300778e-02, 'q_a_norm': 8.366748e-03, 'w_q_b': 3.356115e-03, 'kv_a_norm': 1.171895e-02, 'w_kv_b': 3.987956e-03, 'conv_w': 2.381285e-02, 'conv_b': 2.833499e-01, 'w_rg': 7.212497e-03, 'b_rg': 6.124204e-03, 'w_ig': 1.304205e-02, 'b_ig': 7.695682e-03, 'lru_lambda': 1.155697e-02, 'w_o_mla': 4.487477e-03, 'w_o_lru': 2.317626e-02, 'w_out': 2.280590e-02, 'ple_norm': 1.891027e-02, 'w_ple_gate': 1.841942e-02, 'w_ple': 4.731632e-02, 'final_norm': 3.194966e+01}


def _to_microbatches(a, axis):
    t = _jnp.moveaxis(a, axis, 0)
    t = t.reshape((N_MICROBATCH, t.shape[0] // N_MICROBATCH) + t.shape[1:])
    return _jnp.moveaxis(t, 1, axis + 1)


def setup_inputs(seed: int = 0) -> dict:
    inp = _fwd_setup_inputs(seed)
    key = _jax.random.fold_in(_jax.random.key(seed), 7919)
    shape, _ = _output_shape()
    out = dict(inp)
    out["loss_target"] = _jax.random.normal(_jax.random.fold_in(key, 0), shape, _jnp.float32)
    for i, name in enumerate(TWIN_WEIGHTS):
        w = inp[name].astype(_jnp.float32)
        if MOMENT_SCALE is None:
            s = _jnp.sqrt(_jnp.mean(_jnp.square(w)) + 1e-30)
        else:
            s = MOMENT_SCALE[name]
        km, kv = _jax.random.split(_jax.random.fold_in(key, i + 1))
        out[name] = w
        out["m_" + name] = s * _jax.random.normal(km, w.shape, _jnp.float32)
        out["v_" + name] = (s * s) * _jax.random.uniform(kv, w.shape, _jnp.float32, 0.5, 1.5)
    if N_MICROBATCH > 1:
        for name, axis in PER_EXAMPLE_BATCH_AXIS.items():
            out[name] = _to_microbatches(out[name], axis)
    return {'x': out['x'], 'p': out['p'], 'positions': out['positions'], 'attn_norm': out['attn_norm'], 'w_in': out['w_in'], 'q_a_norm': out['q_a_norm'], 'w_q_b': out['w_q_b'], 'kv_a_norm': out['kv_a_norm'], 'w_kv_b': out['w_kv_b'], 'conv_w': out['conv_w'], 'conv_b': out['conv_b'], 'w_rg': out['w_rg'], 'b_rg': out['b_rg'], 'w_ig': out['w_ig'], 'b_ig': out['b_ig'], 'lru_lambda': out['lru_lambda'], 'w_o_mla': out['w_o_mla'], 'w_o_lru': out['w_o_lru'], 'w_out': out['w_out'], 'ple_norm': out['ple_norm'], 'w_ple_gate': out['w_ple_gate'], 'w_ple': out['w_ple'], 'final_norm': out['final_norm'], 'loss_target': out['loss_target'], 'm_attn_norm': out['m_attn_norm'], 'm_w_in': out['m_w_in'], 'm_q_a_norm': out['m_q_a_norm'], 'm_w_q_b': out['m_w_q_b'], 'm_kv_a_norm': out['m_kv_a_norm'], 'm_w_kv_b': out['m_w_kv_b'], 'm_conv_w': out['m_conv_w'], 'm_conv_b': out['m_conv_b'], 'm_w_rg': out['m_w_rg'], 'm_b_rg': out['m_b_rg'], 'm_w_ig': out['m_w_ig'], 'm_b_ig': out['m_b_ig'], 'm_lru_lambda': out['m_lru_lambda'], 'm_w_o_mla': out['m_w_o_mla'], 'm_w_o_lru': out['m_w_o_lru'], 'm_w_out': out['m_w_out'], 'm_ple_norm': out['m_ple_norm'], 'm_w_ple_gate': out['m_w_ple_gate'], 'm_w_ple': out['m_w_ple'], 'm_final_norm': out['m_final_norm'], 'v_attn_norm': out['v_attn_norm'], 'v_w_in': out['v_w_in'], 'v_q_a_norm': out['v_q_a_norm'], 'v_w_q_b': out['v_w_q_b'], 'v_kv_a_norm': out['v_kv_a_norm'], 'v_w_kv_b': out['v_w_kv_b'], 'v_conv_w': out['v_conv_w'], 'v_conv_b': out['v_conv_b'], 'v_w_rg': out['v_w_rg'], 'v_b_rg': out['v_b_rg'], 'v_w_ig': out['v_w_ig'], 'v_b_ig': out['v_b_ig'], 'v_lru_lambda': out['v_lru_lambda'], 'v_w_o_mla': out['v_w_o_mla'], 'v_w_o_lru': out['v_w_o_lru'], 'v_w_out': out['v_w_out'], 'v_ple_norm': out['v_ple_norm'], 'v_w_ple_gate': out['v_w_ple_gate'], 'v_w_ple': out['v_w_ple'], 'v_final_norm': out['v_final_norm']}


def _loss(weights, diff, rest, loss_target):
    with _jax.named_scope("forward"):
        args = {**rest, TWIN_DIFF_INPUT: diff, **{k: w.astype(_WEIGHT_DTYPES[k]) for k, w in weights.items()}}
        y = _forward(args)
    with _jax.named_scope("loss_head"):
        err = _jnp.square(y.astype(_jnp.float32) - loss_target)
        return 0.5 * _jnp.sum(_jnp.mean(err, axis=-1)) if err.ndim else 0.5 * err


def _adamw(w, g, m, v):
    m = ADAM_B1 * m + (1.0 - ADAM_B1) * g
    v = ADAM_B2 * v + (1.0 - ADAM_B2) * _jnp.square(g)
    m_hat = m / (1.0 - ADAM_B1 ** ADAM_STEP)
    v_hat = v / (1.0 - ADAM_B2 ** ADAM_STEP)
    delta = -ADAM_LR * (m_hat / (_jnp.sqrt(v_hat) + ADAM_EPS) + ADAM_WD * w)
    return delta, m, v


def reference(x, p, positions, attn_norm, w_in, q_a_norm, w_q_b, kv_a_norm, w_kv_b, conv_w, conv_b, w_rg, b_rg, w_ig, b_ig, lru_lambda, w_o_mla, w_o_lru, w_out, ple_norm, w_ple_gate, w_ple, final_norm, loss_target, m_attn_norm, m_w_in, m_q_a_norm, m_w_q_b, m_kv_a_norm, m_w_kv_b, m_conv_w, m_conv_b, m_w_rg, m_b_rg, m_w_ig, m_b_ig, m_lru_lambda, m_w_o_mla, m_w_o_lru, m_w_out, m_ple_norm, m_w_ple_gate, m_w_ple, m_final_norm, v_attn_norm, v_w_in, v_q_a_norm, v_w_q_b, v_kv_a_norm, v_w_kv_b, v_conv_w, v_conv_b, v_w_rg, v_b_rg, v_w_ig, v_b_ig, v_lru_lambda, v_w_o_mla, v_w_o_lru, v_w_out, v_ple_norm, v_w_ple_gate, v_w_ple, v_final_norm):
    given = dict(x=x, p=p, positions=positions, attn_norm=attn_norm, w_in=w_in, q_a_norm=q_a_norm, w_q_b=w_q_b, kv_a_norm=kv_a_norm, w_kv_b=w_kv_b, conv_w=conv_w, conv_b=conv_b, w_rg=w_rg, b_rg=b_rg, w_ig=w_ig, b_ig=b_ig, lru_lambda=lru_lambda, w_o_mla=w_o_mla, w_o_lru=w_o_lru, w_out=w_out, ple_norm=ple_norm, w_ple_gate=w_ple_gate, w_ple=w_ple, final_norm=final_norm, loss_target=loss_target, m_attn_norm=m_attn_norm, m_w_in=m_w_in, m_q_a_norm=m_q_a_norm, m_w_q_b=m_w_q_b, m_kv_a_norm=m_kv_a_norm, m_w_kv_b=m_w_kv_b, m_conv_w=m_conv_w, m_conv_b=m_conv_b, m_w_rg=m_w_rg, m_b_rg=m_b_rg, m_w_ig=m_w_ig, m_b_ig=m_b_ig, m_lru_lambda=m_lru_lambda, m_w_o_mla=m_w_o_mla, m_w_o_lru=m_w_o_lru, m_w_out=m_w_out, m_ple_norm=m_ple_norm, m_w_ple_gate=m_w_ple_gate, m_w_ple=m_w_ple, m_final_norm=m_final_norm, v_attn_norm=v_attn_norm, v_w_in=v_w_in, v_q_a_norm=v_q_a_norm, v_w_q_b=v_w_q_b, v_kv_a_norm=v_kv_a_norm, v_w_kv_b=v_w_kv_b, v_conv_w=v_conv_w, v_conv_b=v_conv_b, v_w_rg=v_w_rg, v_b_rg=v_b_rg, v_w_ig=v_w_ig, v_b_ig=v_b_ig, v_lru_lambda=v_lru_lambda, v_w_o_mla=v_w_o_mla, v_w_o_lru=v_w_o_lru, v_w_out=v_w_out, v_ple_norm=v_ple_norm, v_w_ple_gate=v_w_ple_gate, v_w_ple=v_w_ple, v_final_norm=v_final_norm)
    weights = {n: given[n] for n in TWIN_WEIGHTS}
    shared = {n: given[n] for n in SHARED_INPUTS}
    per_example = {n: given[n] for n in ['x', 'p', 'positions']}
    grad_fn = _jax.value_and_grad(_loss, argnums=(0, 1))

    def one_microbatch(ex, loss_target):
        ex = dict(ex)
        diff = ex.pop(TWIN_DIFF_INPUT)
        return grad_fn(weights, diff, {**shared, **ex}, loss_target)

    if N_MICROBATCH == 1:
        loss, (grad_w, grad_x) = one_microbatch(per_example, given["loss_target"])
    else:
        def body(carry, xs):
            loss_sum, grad_sum = carry
            l_k, (gw_k, gx_k) = one_microbatch(xs[0], xs[1])
            with _jax.named_scope("update"):
                return (loss_sum + l_k, _jax.tree.map(_jnp.add, grad_sum, gw_k)), gx_k

        init = (_jnp.zeros((), _jnp.float32), _jax.tree.map(_jnp.zeros_like, weights))
        (loss, grad_w), grad_x = _jax.lax.scan(body, init, (per_example, given["loss_target"]))
    with _jax.named_scope("update"):
        delta_w, new_m, new_v = {}, {}, {}
        for n in TWIN_WEIGHTS:
            delta_w[n], new_m[n], new_v[n] = _adamw(weights[n], grad_w[n], given["m_" + n], given["v_" + n])
    return (loss, grad_x, *[grad_w[n] for n in TWIN_WEIGHTS], *[delta_w[n] for n in TWIN_WEIGHTS],
            *[new_m[n] for n in TWIN_WEIGHTS], *[new_v[n] for n in TWIN_WEIGHTS])
```

```python
import functools
import math

import numpy as np
import jax
import jax.numpy as jnp
from jax import lax
from jax.experimental import pallas as pl
from jax.experimental.pallas import tpu as pltpu

F32 = jnp.float32
BF = jnp.bfloat16
MESH_AXES = ("x", "y", "c")
NDEV = 8

D = 2048
NH = 16
QL = 512
KVL = 512
ROPE = 64
PLE = 256
NBLK = 16
BD = 128
CHUNK_SHIFT = 6
EPS = 1e-6
LRU_C = 8.0
ROPE_THETA = 10000.0
IN_TOTAL = 11328
ZW = 11520
Z_Q = 10240
Z_KV = 10752
Z_KR = 11264
SCALE = 1.0 / math.sqrt(128 + 64)

ADAM_LR = 0.001
ADAM_B1 = 0.9
ADAM_B2 = 0.999
ADAM_EPS = 1e-08
ADAM_WD = 0.01
ADAM_STEP = 10

VMEM_LIMIT = 60 * 1024 * 1024

NN = (((1,), (0,)), ((), ()))
NT = (((1,), (1,)), ((), ()))


def _tile(n, pref):
    t = min(n, pref)
    while n % t:
        t //= 2
    return t


def _params(sem):
    return pltpu.CompilerParams(dimension_semantics=sem, vmem_limit_bytes=VMEM_LIMIT)


def _dot(a, b, dims=NN):
    return lax.dot_general(a, b, dims, preferred_element_type=F32)


def _dot_tn(a, b):
    return lax.dot_general(a.T, b, NN, preferred_element_type=F32)


def _bf(v):
    return v if v.dtype == BF else v.astype(BF)


def mm(name, a, b, mode, *, out_dtype=F32, tm=1024, tn=1024, tk=1024, res=None,
       a_heads=False, b_heads=False, o_heads=False):
    if mode == "nn":
        M = a.shape[0]
        K = a.shape[1]
        N = b.shape[1]
    elif mode == "nt":
        M = a.shape[1] if a_heads else a.shape[0]
        K = a.shape[0] * a.shape[2] if a_heads else a.shape[1]
        N = b.shape[0]
    else:
        K = a.shape[0]
        M = a.shape[1]
        N = b.shape[0] * b.shape[2] if b_heads else b.shape[1]
    tm = _tile(M, tm)
    tn = 256 if (o_heads or b_heads) else _tile(N, tn)
    tk = 256 if a_heads else _tile(K, tk)
    nk = K // tk
    grid = (M // tm, N // tn, nk)

    if mode == "nn":
        a_spec = pl.BlockSpec((tm, tk), lambda i, j, k: (i, k))
        b_spec = pl.BlockSpec((tk, tn), lambda i, j, k: (k, j))
    elif mode == "nt":
        if a_heads:
            a_spec = pl.BlockSpec((None, tm, 256), lambda i, j, k: (k, i, 0))
        else:
            a_spec = pl.BlockSpec((tm, tk), lambda i, j, k: (i, k))
        b_spec = pl.BlockSpec((tn, tk), lambda i, j, k: (j, k))
    else:
        a_spec = pl.BlockSpec((tk, tm), lambda i, j, k: (k, i))
        if b_heads:
            b_spec = pl.BlockSpec((None, tk, 256), lambda i, j, k: (j, k, 0))
        else:
            b_spec = pl.BlockSpec((tk, tn), lambda i, j, k: (k, j))
    if o_heads:
        o_spec = pl.BlockSpec((None, tm, 256), lambda i, j, k: (j, i, 0))
        o_shape = jax.ShapeDtypeStruct((N // 256, M, 256), out_dtype)
    else:
        o_spec = pl.BlockSpec((tm, tn), lambda i, j, k: (i, j))
        o_shape = jax.ShapeDtypeStruct((M, N), out_dtype)
    in_specs = [a_spec, b_spec]
    args = [a, b]
    if res is not None:
        in_specs.append(pl.BlockSpec((tm, tn), lambda i, j, k: (i, j)))
        args.append(res)

    def body(*refs):
        a_ref, b_ref = refs[0], refs[1]
        r_ref = refs[2] if res is not None else None
        o_ref = refs[3] if res is not None else refs[2]
        av = _bf(a_ref[...])
        bv = _bf(b_ref[...])
        if mode == "nn":
            p = _dot(av, bv, NN)
        elif mode == "nt":
            p = _dot(av, bv, NT)
        else:
            p = _dot_tn(av, bv)

        def fin(acc):
            if r_ref is not None:
                acc = acc + r_ref[...]
            o_ref[...] = acc.astype(o_ref.dtype)

        if nk == 1:
            fin(p)
        else:
            acc_ref = refs[-1]
            k = pl.program_id(2)

            @pl.when(k == 0)
            def _():
                acc_ref[...] = p

            @pl.when(k > 0)
            def _():
                acc_ref[...] += p

            @pl.when(k == nk - 1)
            def _():
                fin(acc_ref[...])

    scratch = [] if nk == 1 else [pltpu.VMEM((tm, tn), F32)]
    return pl.pallas_call(
        body, name=name, grid=grid, in_specs=in_specs, out_specs=o_spec, out_shape=o_shape,
        scratch_shapes=scratch, compiler_params=_params(("parallel", "parallel", "arbitrary")),
    )(*args)


def _rs(tr, w, cb=0):
    return pl.BlockSpec((tr, w), lambda i: (i, cb))


def _fs(shape):
    nd = len(shape)
    return pl.BlockSpec(shape, lambda i: (0,) * nd)


def _rowcall(name, body, S, tr, ins, in_specs, outs, out_specs, scratch=()):
    return pl.pallas_call(
        body, name=name, grid=(S // tr,), in_specs=in_specs, out_specs=out_specs, out_shape=outs,
        scratch_shapes=list(scratch), compiler_params=_params(("arbitrary",)),
    )(*ins)


def _sds(shape, dt):
    return jax.ShapeDtypeStruct(shape, dt)


def _fold8(v):
    tr, w = v.shape
    return jnp.sum(v.reshape(tr // 8, 8, w), axis=0)


def _acc_rows(i, n, acc_ref, out_ref, part):
    @pl.when(i == 0)
    def _():
        acc_ref[...] = part

    @pl.when(i > 0)
    def _():
        acc_ref[...] += part

    @pl.when(i == n - 1)
    def _():
        out_ref[...] = jnp.sum(acc_ref[...], axis=0, keepdims=True)


def _rms_fwd(x, g):
    r = lax.rsqrt(jnp.mean(x * x, axis=-1, keepdims=True) + EPS)
    return x * r * g


def _rms_bwd(x, dy, g):
    r = lax.rsqrt(jnp.mean(x * x, axis=-1, keepdims=True) + EPS)
    xh = x * r
    dxh = dy * g
    dx = r * (dxh - xh * jnp.mean(dxh * xh, axis=-1, keepdims=True))
    return dx, dy * xh


def _sig(x):
    return jax.nn.sigmoid(x)


def norm_fwd(name, x, g, tr):
    S, W = x.shape

    def body(x_ref, g_ref, o_ref):
        o_ref[...] = _rms_fwd(x_ref[...], g_ref[...]).astype(BF)

    return _rowcall(name, body, S, tr, [x, g], [_rs(tr, W), _fs((1, W))], _sds((S, W), BF), _rs(tr, W))


def norm_bwd(name, x, dy, g, dres, tr):
    S, W = x.shape
    n = S // tr

    def body(x_ref, dy_ref, g_ref, dr_ref, dx_ref, dg_ref, acc_ref):
        i = pl.program_id(0)
        dx, dgp = _rms_bwd(x_ref[...], dy_ref[...], g_ref[...])
        dx_ref[...] = dr_ref[...] + dx
        _acc_rows(i, n, acc_ref, dg_ref, _fold8(dgp))

    return _rowcall(name, body, S, tr, [x, dy, g, dres],
                    [_rs(tr, W), _rs(tr, W), _fs((1, W)), _rs(tr, W)],
                    (_sds((S, W), F32), _sds((1, W), F32)), (_rs(tr, W), _fs((1, W))),
                    scratch=[pltpu.VMEM((8, W), F32)])


def _lane_lt(shape, n):
    return lax.broadcasted_iota(jnp.int32, shape, 1) < n


def qkv_prep(z, qg, kvg, cs, tr):
    S = z.shape[0]

    def body(ql_ref, kl_ref, zk_ref, qg_ref, kg_ref, cs_ref, qn_ref, kn_ref, kr_ref):
        qn_ref[...] = _rms_fwd(ql_ref[...], qg_ref[...]).astype(BF)
        kn_ref[...] = _rms_fwd(kl_ref[...], kg_ref[...]).astype(BF)
        t = zk_ref[...] * cs_ref[...]
        t = t + pltpu.roll(t, 64, 1)
        kr_ref[...] = jnp.where(_lane_lt(t.shape, 64), t, 0.0).astype(BF)

    return _rowcall(
        "qkv_prep", body, S, tr, [z, z, z, qg, kvg, cs],
        [_rs(tr, QL, Z_Q // QL), _rs(tr, KVL, Z_KV // KVL), _rs(tr, 128, Z_KR // 128),
         _fs((1, QL)), _fs((1, KVL)), _rs(tr, 128)],
        (_sds((S, QL), BF), _sds((S, KVL), BF), _sds((S, 128), BF)),
        (_rs(tr, QL), _rs(tr, KVL), _rs(tr, 128)))


def qkv_prep_bwd(z, dqn, dkn, dkr, qg, kvg, cs, tr):
    S = z.shape[0]
    n = S // tr

    def body(ql_ref, kl_ref, dq_ref, dk_ref, dkr_ref, qg_ref, kg_ref, cs_ref,
             dz_ref, dqg_ref, dkg_ref, accq, acck):
        i = pl.program_id(0)
        dql, gq = _rms_bwd(ql_ref[...], dq_ref[...], qg_ref[...])
        dkl, gk = _rms_bwd(kl_ref[...], dk_ref[...], kg_ref[...])
        t = jnp.where(_lane_lt((tr, 128), 64), dkr_ref[...], 0.0)
        t = (t + pltpu.roll(t, 64, 1)) * cs_ref[...]
        dz_ref[...] = jnp.concatenate(
            [dql.astype(BF), dkl.astype(BF), t.astype(BF), jnp.zeros((tr, 128), BF)], axis=1)
        _acc_rows(i, n, accq, dqg_ref, _fold8(gq))
        _acc_rows(i, n, acck, dkg_ref, _fold8(gk))

    return _rowcall(
        "qkv_prep_bwd", body, S, tr, [z, z, dqn, dkn, dkr, qg, kvg, cs],
        [_rs(tr, QL, Z_Q // QL), _rs(tr, KVL, Z_KV // KVL), _rs(tr, QL), _rs(tr, KVL), _rs(tr, 128),
         _fs((1, QL)), _fs((1, KVL)), _rs(tr, 128)],
        (_sds((S, ZW - Z_Q), BF), _sds((1, QL), F32), _sds((1, KVL), F32)),
        (_rs(tr, ZW - Z_Q), _fs((1, QL)), _fs((1, KVL))),
        scratch=[pltpu.VMEM((8, QL), F32), pltpu.VMEM((8, KVL), F32)])


def gate_out(o, z, hs, tr):
    S = o.shape[0]

    def body(o_ref, gm_ref, h_ref, gl_ref, om_ref, hl_ref):
        gm = gm_ref[...]
        gl = gl_ref[...]
        om_ref[...] = (o_ref[...] * (gm * _sig(gm))).astype(BF)
        hl_ref[...] = (h_ref[...] * (gl * _sig(gl))).astype(BF)

    return _rowcall("gate_out", body, S, tr, [o, z, hs, z],
                    [_rs(tr, D), _rs(tr, D, 0), _rs(tr, D), _rs(tr, D, 2)],
                    (_sds((S, D), BF), _sds((S, D), BF)), (_rs(tr, D), _rs(tr, D)))


def gate_bwd(dom, o, dhl, hs, z, tr):
    S = o.shape[0]

    def body(dom_ref, o_ref, dhl_ref, h_ref, gm_ref, gl_ref, do_ref, dgm_ref, dh_ref, dgl_ref):
        gm = gm_ref[...]
        sm = _sig(gm)
        dom_v = dom_ref[...]
        do_ref[...] = dom_v * (gm * sm)
        dgm_ref[...] = (dom_v * o_ref[...] * (sm * (1.0 + gm * (1.0 - sm)))).astype(BF)
        gl = gl_ref[...]
        sl = _sig(gl)
        dhl_v = dhl_ref[...]
        dh_ref[...] = dhl_v * (gl * sl)
        dgl_ref[...] = (dhl_v * h_ref[...] * (sl * (1.0 + gl * (1.0 - sl)))).astype(BF)

    return _rowcall("gate_bwd", body, S, tr, [dom, o, dhl, hs, z, z],
                    [_rs(tr, D), _rs(tr, D), _rs(tr, D), _rs(tr, D), _rs(tr, D, 0), _rs(tr, D, 2)],
                    (_sds((S, D), F32), _sds((S, D), BF), _sds((S, D), F32), _sds((S, D), BF)),
                    (_rs(tr, D),) * 4)


def merge_fwd(ym, yl, z, tr):
    S = ym.shape[0]

    def body(ym_ref, yl_ref, mm_ref, ml_ref, o_ref):
        o_ref[...] = (_sig(mm_ref[...]) * ym_ref[...] + _sig(ml_ref[...]) * yl_ref[...]).astype(BF)

    return _rowcall("merge_fwd", body, S, tr, [ym, yl, z, z],
                    [_rs(tr, D), _rs(tr, D), _rs(tr, D, 3), _rs(tr, D, 4)],
                    _sds((S, D), BF), _rs(tr, D))


def merge_bwd(dmg, ym, yl, z, tr):
    S = ym.shape[0]

    def body(d_ref, ym_ref, yl_ref, mm_ref, ml_ref, dym_ref, dyl_ref, dz_ref):
        d = d_ref[...]
        sm = _sig(mm_ref[...])
        sl = _sig(ml_ref[...])
        dym_ref[...] = (d * sm).astype(BF)
        dyl_ref[...] = (d * sl).astype(BF)
        dz_ref[:, 0:D] = (d * ym_ref[...] * (sm * (1.0 - sm))).astype(BF)
        dz_ref[:, D:2 * D] = (d * yl_ref[...] * (sl * (1.0 - sl))).astype(BF)

    return _rowcall("merge_bwd", body, S, tr, [dmg, ym, yl, z, z],
                    [_rs(tr, D), _rs(tr, D), _rs(tr, D), _rs(tr, D, 3), _rs(tr, D, 4)],
                    (_sds((S, D), BF), _sds((S, D), BF), _sds((S, 2 * D), BF)),
                    (_rs(tr, D), _rs(tr, D), _rs(tr, 2 * D)))


def ple_fin(x1, pe, gp, tr):
    S = x1.shape[0]

    def body(x_ref, pe_ref, gp_ref, o_ref):
        o_ref[...] = x_ref[...] + pe_ref[...] * _sig(gp_ref[...])

    return _rowcall("ple_fin", body, S, tr, [x1, pe, gp], [_rs(tr, D)] * 3, _sds((S, D), F32), _rs(tr, D))


def ple_bwd(dx2, pe, gp, tr):
    S = dx2.shape[0]

    def body(d_ref, pe_ref, gp_ref, dpe_ref, dgp_ref):
        d = d_ref[...]
        sg = _sig(gp_ref[...])
        dpe_ref[...] = (d * sg).astype(BF)
        dgp_ref[...] = (d * pe_ref[...] * (sg * (1.0 - sg))).astype(BF)

    return _rowcall("ple_bwd", body, S, tr, [dx2, pe, gp], [_rs(tr, D)] * 3,
                    (_sds((S, D), BF), _sds((S, D), BF)), (_rs(tr, D), _rs(tr, D)))


def final_loss(x, g, tgt, tr):
    S, W = x.shape
    n = S // tr

    def body(x_ref, g_ref, t_ref, ls_ref, dx_ref, dg_ref, accl, accg):
        i = pl.program_id(0)
        xv = x_ref[...]
        gv = g_ref[...]
        e = _rms_fwd(xv, gv) - t_ref[...]
        e2 = _fold8(e * e)
        l8 = e2[:, 0:128]
        for k in range(1, W // 128):
            l8 = l8 + e2[:, k * 128:(k + 1) * 128]
        dx, dgp = _rms_bwd(xv, e * (1.0 / W), gv)
        dx_ref[...] = dx
        _acc_rows(i, n, accg, dg_ref, _fold8(dgp))

        @pl.when(i == 0)
        def _():
            accl[...] = l8

        @pl.when(i > 0)
        def _():
            accl[...] += l8

        @pl.when(i == n - 1)
        def _():
            tot = jnp.sum(jnp.sum(accl[...], axis=0, keepdims=True), axis=1, keepdims=True)
            ls_ref[...] = jnp.broadcast_to(tot, (1, 128))

    return _rowcall("final_loss", body, S, tr, [x, g, tgt], [_rs(tr, W), _fs((1, W)), _rs(tr, W)],
                    (_sds((1, 128), F32), _sds((S, W), F32), _sds((1, W), F32)),
                    (_fs((1, 128)), _rs(tr, W), _fs((1, W))),
                    scratch=[pltpu.VMEM((8, 128), F32), pltpu.VMEM((8, W), F32)])


def _chunk_mask(qi, kj, tq, tk):
    rows = qi * tq + lax.broadcasted_iota(jnp.int32, (tq, tk), 0)
    cols = kj * tk + lax.broadcasted_iota(jnp.int32, (tq, tk), 1)
    return (cols >> CHUNK_SHIFT) <= (rows >> CHUNK_SHIFT)


def _load_kv(kv_ref, kr_ref, kj, tk):
    off = pl.multiple_of(kj * tk, tk)
    kv = kv_ref[pl.ds(off, tk), :]
    k = jnp.concatenate([kv[:, 0:128], kr_ref[pl.ds(off, tk), :]], axis=1)
    return k, kv[:, 128:256], off


def attn_fwd(q_ext, kv_ext, kr, cs, tq):
    H, S, _ = q_ext.shape
    nq = S // tq
    tk = tq

    def body(q_ref, kv_ref, kr_ref, cs_ref, o_ref, lse_ref, qf_ref):
        qi = pl.program_id(1)
        q = q_ref[...]
        hi = q[:, 128:256] * cs_ref[...]
        hi = hi + pltpu.roll(hi, 64, 1)
        hi = jnp.where(_lane_lt(hi.shape, 64), hi, 0.0)
        qf = jnp.concatenate([q[:, 0:128], hi], axis=1).astype(BF)
        qf_ref[...] = qf

        def step(kj, carry, masked):
            m, l, acc = carry
            k, v, _ = _load_kv(kv_ref, kr_ref, kj, tk)
            s = _dot(qf, k, NT) * SCALE
            if masked:
                s = jnp.where(_chunk_mask(qi, kj, tq, tk), s, -jnp.inf)
            m_new = jnp.maximum(m, jnp.max(s, axis=1, keepdims=True))
            alpha = jnp.exp(m - m_new)
            p = jnp.exp(s - m_new)
            l = alpha * l + jnp.sum(p, axis=1, keepdims=True)
            acc = alpha * acc + _dot(p.astype(BF), v, NN)
            return m_new, l, acc

        init = (jnp.full((tq, 1), -jnp.inf, F32), jnp.zeros((tq, 1), F32), jnp.zeros((tq, 128), F32))
        carry = lax.fori_loop(0, qi, lambda kj, c: step(kj, c, False), init)
        m, l, acc = step(qi, carry, True)
        o_ref[...] = acc / l
        lse_ref[...] = jnp.broadcast_to(m + jnp.log(l), (tq, 128))

    return pl.pallas_call(
        body, name="attn_fwd", grid=(H, nq),
        in_specs=[pl.BlockSpec((None, tq, 256), lambda h, i: (h, i, 0)),
                  pl.BlockSpec((None, S, 256), lambda h, i: (h, 0, 0)),
                  pl.BlockSpec((S, 128), lambda h, i: (0, 0)),
                  pl.BlockSpec((tq, 128), lambda h, i: (i, 0))],
        out_specs=(pl.BlockSpec((tq, 128), lambda h, i: (i, h)),
                   pl.BlockSpec((None, tq, 128), lambda h, i: (h, i, 0)),
                   pl.BlockSpec((None, tq, 256), lambda h, i: (h, i, 0))),
        out_shape=(_sds((S, H * 128), F32), _sds((H, S, 128), F32), _sds((H, S, 256), BF)),
        compiler_params=_params(("arbitrary", "arbitrary")),
    )(q_ext, kv_ext, kr, cs)


def attn_bwd(qf, kv_ext, kr, o, do, lse, cs, tq):
    H, S, _ = qf.shape
    nq = S // tq
    tk = tq

    def body(q_ref, kv_ref, kr_ref, o_ref, do_ref, lse_ref, cs_ref, dq_ref, dkv_ref, dkr_ref, dk_acc, dv_acc):
        h = pl.program_id(0)
        qi = pl.program_id(1)

        @pl.when(qi == 0)
        def _():
            dk_acc[...] = jnp.zeros_like(dk_acc)
            dv_acc[...] = jnp.zeros_like(dv_acc)

        @pl.when(jnp.logical_and(h == 0, qi == 0))
        def _():
            dkr_ref[...] = jnp.zeros_like(dkr_ref)

        q = q_ref[...]
        dov = do_ref[...]
        delta = jnp.sum(dov * o_ref[...], axis=1, keepdims=True)
        lse_v = lse_ref[:, 0:1]
        do_b = dov.astype(BF)

        def step(kj, dq, masked):
            k, v, off = _load_kv(kv_ref, kr_ref, kj, tk)
            s = _dot(q, k, NT) * SCALE
            if masked:
                s = jnp.where(_chunk_mask(qi, kj, tq, tk), s, -jnp.inf)
            p = jnp.exp(s - lse_v)
            dp = _dot(do_b, v, NT)
            ds = (p * (dp - delta) * SCALE).astype(BF)
            dv_acc[pl.ds(off, tk), :] += _dot_tn(p.astype(BF), do_b)
            dk_acc[pl.ds(off, tk), :] += _dot_tn(ds, q)
            return dq + _dot(ds, k, NN)

        dq = lax.fori_loop(0, qi, lambda kj, c: step(kj, c, False), jnp.zeros((tq, 256), F32))
        dq = step(qi, dq, True)
        hi = dq[:, 128:256]
        hi = (hi + pltpu.roll(hi, 64, 1)) * cs_ref[...]
        dq_ref[...] = jnp.concatenate([dq[:, 0:128], hi], axis=1).astype(BF)

        @pl.when(qi == nq - 1)
        def _():
            dkv_ref[...] = jnp.concatenate([dk_acc[:, 0:128], dv_acc[...]], axis=1).astype(BF)
            dkr_ref[...] += dk_acc[:, 128:256]

    return pl.pallas_call(
        body, name="attn_bwd", grid=(H, nq),
        in_specs=[pl.BlockSpec((None, tq, 256), lambda h, i: (h, i, 0)),
                  pl.BlockSpec((None, S, 256), lambda h, i: (h, 0, 0)),
                  pl.BlockSpec((S, 128), lambda h, i: (0, 0)),
                  pl.BlockSpec((tq, 128), lambda h, i: (i, h)),
                  pl.BlockSpec((tq, 128), lambda h, i: (i, h)),
                  pl.BlockSpec((None, tq, 128), lambda h, i: (h, i, 0)),
                  pl.BlockSpec((tq, 128), lambda h, i: (i, 0))],
        out_specs=(pl.BlockSpec((None, tq, 256), lambda h, i: (h, i, 0)),
                   pl.BlockSpec((None, S, 256), lambda h, i: (h, 0, 0)),
                   pl.BlockSpec((S, 128), lambda h, i: (0, 0))),
        out_shape=(_sds((H, S, 256), BF), _sds((H, S, 256), BF), _sds((S, 128), F32)),
        scratch_shapes=[pltpu.VMEM((S, 256), F32), pltpu.VMEM((S, 128), F32)],
        compiler_params=_params(("arbitrary", "arbitrary")),
    )(qf, kv_ext, kr, o, do, lse, cs)


def _log1p(y):
    w = 1.0 + y
    return jnp.where(w == 1.0, y, jnp.log(w) * (y / (w - 1.0)))


def _expm1(x):
    u = jnp.exp(x)
    return jnp.where(u == 1.0, x, (u - 1.0) * (x / jnp.log(u)))


def _softplus(x):
    return jnp.maximum(x, 0.0) + _log1p(jnp.exp(-jnp.abs(x)))


def _blockdiag(xb, w_ref):
    return jnp.concatenate(
        [_dot(xb[:, k * BD:(k + 1) * BD], w_ref[k], NN) for k in range(NBLK)], axis=1)


def _gates(xc, wr_ref, br_ref, wi_ref, bi_ref, sp):
    xb = xc.astype(BF)
    r = _sig(_blockdiag(xb, wr_ref) + br_ref[...])
    ig = _sig(_blockdiag(xb, wi_ref) + bi_ref[...])
    log_a = (-LRU_C * r) * sp
    mult = jnp.sqrt(-_expm1(2.0 * log_a))
    return xb, r, ig, log_a, mult


def _prev8_spec(tr, w, cb):
    return pl.BlockSpec((8, w), lambda i: (jnp.maximum(i * (tr // 8) - 1, 0), cb))


def lru_fwd(z, conv_w, conv_b, w_rg, b_rg, w_ig, b_ig, lam, tr):
    S = z.shape[0]
    W = D

    def body(u_ref, up_ref, cw_ref, cb_ref, wr_ref, br_ref, wi_ref, bi_ref, lam_ref,
             xc_ref, a_ref, h_ref, buf, bt, hcar):
        i = pl.program_id(0)

        @pl.when(i == 0)
        def _():
            buf[0:8, :] = jnp.zeros((8, W), F32)
            hcar[...] = jnp.zeros_like(hcar)

        @pl.when(i > 0)
        def _():
            buf[0:8, :] = up_ref[...]

        buf[8:8 + tr, :] = u_ref[...]
        cw = cw_ref[...]
        xc = buf[pl.ds(5, tr), :] * cw[0:1, :]
        for kk in range(1, 4):
            xc = xc + buf[pl.ds(5 + kk, tr), :] * cw[kk:kk + 1, :]
        xc = xc + cb_ref[...]
        xc_ref[...] = xc
        sp = _softplus(-lam_ref[...])
        _, _, ig, log_a, mult = _gates(xc, wr_ref, br_ref, wi_ref, bi_ref, sp)
        a_ref[...] = jnp.exp(log_a)
        bt[...] = mult * (ig * xc)
        row = lax.broadcasted_iota(jnp.int32, (8, W), 0)

        def grp(g, hp):
            off = pl.multiple_of(g * 8, 8)
            A = a_ref[pl.ds(off, 8), :]
            B = bt[pl.ds(off, 8), :]
            for d in (1, 2, 4):
                ok = row >= d
                B = jnp.where(ok, A * pltpu.roll(B, d, 0) + B, B)
                A = jnp.where(ok, A * pltpu.roll(A, d, 0), A)
            hh = A * hp + B
            h_ref[pl.ds(off, 8), :] = hh
            return hh[7:8, :]

        hcar[...] = lax.fori_loop(0, tr // 8, grp, hcar[...])

    return _rowcall(
        "lru_fwd", body, S, tr, [z, z, conv_w, conv_b, w_rg, b_rg, w_ig, b_ig, lam],
        [_rs(tr, W, 1), _prev8_spec(tr, W, 1), _fs((4, W)), _fs((1, W)), _fs((NBLK, BD, BD)), _fs((1, W)),
         _fs((NBLK, BD, BD)), _fs((1, W)), _fs((1, W))],
        (_sds((S, W), F32),) * 3, (_rs(tr, W),) * 3,
        scratch=[pltpu.VMEM((tr + 8, W), F32), pltpu.VMEM((tr, W), F32), pltpu.VMEM((1, W), F32)])


def lru_bwd(dh, a, hs, xc, w_rg, b_rg, w_ig, b_ig, lam, tr):
    S, W = dh.shape
    n = S // tr
    nb8 = S // 8

    def rev(i):
        return n - 1 - i

    row_spec = pl.BlockSpec((tr, W), lambda i: (rev(i), 0))
    next8 = pl.BlockSpec((8, W), lambda i: (jnp.minimum((rev(i) + 1) * (tr // 8), nb8 - 1), 0))
    prev8 = pl.BlockSpec((8, W), lambda i: (jnp.maximum(rev(i) * (tr // 8) - 1, 0), 0))

    def body(dh_ref, a_ref, an_ref, h_ref, hp_ref, xc_ref, wr_ref, br_ref, wi_ref, bi_ref, lam_ref,
             dxc_ref, dwr_ref, dwi_ref, dbr_ref, dbi_ref, dlam_ref, dcb_ref,
             bufa, bufh, apr, gsc, gcar, acc_br, acc_bi, acc_sp, acc_cb):
        i = pl.program_id(0)
        first = i == 0
        last_tile = i == n - 1

        bufa[0:tr, :] = a_ref[...]
        bufa[tr:tr + 8, :] = an_ref[...]
        apr[...] = bufa[pl.ds(1, tr), :]
        bufh[8:8 + tr, :] = h_ref[...]

        @pl.when(last_tile)
        def _():
            bufh[0:8, :] = jnp.zeros((8, W), F32)

        @pl.when(jnp.logical_not(last_tile))
        def _():
            bufh[0:8, :] = hp_ref[...]

        @pl.when(first)
        def _():
            gcar[...] = jnp.zeros_like(gcar)

        row = lax.broadcasted_iota(jnp.int32, (8, W), 0)
        ng = tr // 8

        def grp(t, gn):
            g = ng - 1 - t
            off = pl.multiple_of(g * 8, 8)
            A = apr[pl.ds(off, 8), :]
            B = dh_ref[pl.ds(off, 8), :]
            for d in (1, 2, 4):
                ok = row < 8 - d
                B = jnp.where(ok, B + A * pltpu.roll(B, 8 - d, 0), B)
                A = jnp.where(ok, A * pltpu.roll(A, 8 - d, 0), A)
            gg = B + A * gn
            gsc[pl.ds(off, 8), :] = gg
            return gg[0:1, :]

        gcar[...] = lax.fori_loop(0, ng, grp, gcar[...])

        G = gsc[...]
        hprev = bufh[pl.ds(7, tr), :]
        xcv = xc_ref[...]
        lam_v = lam_ref[...]
        sp = _softplus(-lam_v)
        xb, r, ig, log_a, mult = _gates(xcv, wr_ref, br_ref, wi_ref, bi_ref, sp)
        av = a_ref[...]
        d_a = G * hprev
        d_mult = G * (ig * xcv)
        d_ig = G * (mult * xcv)
        d_log_a = d_a * av - d_mult * (av * av / mult)
        d_gr = (d_log_a * (-LRU_C * sp)) * (r * (1.0 - r))
        d_gi = d_ig * (ig * (1.0 - ig))
        gr_b = d_gr.astype(BF)
        gi_b = d_gi.astype(BF)
        dxc = G * (mult * ig)
        dxc = dxc + jnp.concatenate(
            [_dot(gr_b[:, k * BD:(k + 1) * BD], wr_ref[k], NT)
             + _dot(gi_b[:, k * BD:(k + 1) * BD], wi_ref[k], NT) for k in range(NBLK)], axis=1)
        dxc_ref[...] = dxc

        @pl.when(first)
        def _():
            dwr_ref[...] = jnp.zeros_like(dwr_ref)
            dwi_ref[...] = jnp.zeros_like(dwi_ref)

        for k in range(NBLK):
            sl = slice(k * BD, (k + 1) * BD)
            dwr_ref[k] += _dot_tn(xb[:, sl], gr_b[:, sl])
            dwi_ref[k] += _dot_tn(xb[:, sl], gi_b[:, sl])

        _acc_rows(i, n, acc_br, dbr_ref, _fold8(d_gr))
        _acc_rows(i, n, acc_bi, dbi_ref, _fold8(d_gi))
        _acc_rows(i, n, acc_cb, dcb_ref, _fold8(dxc))

        part = _fold8(d_log_a * (-LRU_C * r))

        @pl.when(first)
        def _():
            acc_sp[...] = part

        @pl.when(jnp.logical_not(first))
        def _():
            acc_sp[...] += part

        @pl.when(last_tile)
        def _():
            dsp = jnp.sum(acc_sp[...], axis=0, keepdims=True)
            dlam_ref[...] = dsp * (-_sig(-lam_v))

    vec = _sds((1, W), F32)
    wsh = _sds((NBLK, BD, BD), F32)
    return _rowcall(
        "lru_bwd", body, S, tr, [dh, a, a, hs, hs, xc, w_rg, b_rg, w_ig, b_ig, lam],
        [row_spec, row_spec, next8, row_spec, prev8, row_spec, _fs((NBLK, BD, BD)), _fs((1, W)),
         _fs((NBLK, BD, BD)), _fs((1, W)), _fs((1, W))],
        (_sds((S, W), F32), wsh, wsh, vec, vec, vec, vec),
        (row_spec, _fs((NBLK, BD, BD)), _fs((NBLK, BD, BD)), _fs((1, W)), _fs((1, W)), _fs((1, W)), _fs((1, W))),
        scratch=[pltpu.VMEM((tr + 8, W), F32), pltpu.VMEM((tr + 8, W), F32), pltpu.VMEM((tr, W), F32),
                 pltpu.VMEM((tr, W), F32), pltpu.VMEM((1, W), F32), pltpu.VMEM((8, W), F32),
                 pltpu.VMEM((8, W), F32), pltpu.VMEM((8, W), F32), pltpu.VMEM((8, W), F32)])


def conv_bwd(dxc, z, conv_w, tr):
    S, W = dxc.shape
    n = S // tr
    nb8 = S // 8

    def body(d_ref, dn_ref, u_ref, up_ref, cw_ref, du_ref, dcw_ref, bufd, bufu, acc):
        i = pl.program_id(0)
        bufd[0:tr, :] = d_ref[...]

        @pl.when(i == n - 1)
        def _():
            bufd[tr:tr + 8, :] = jnp.zeros((8, W), F32)

        @pl.when(i < n - 1)
        def _():
            bufd[tr:tr + 8, :] = dn_ref[...]

        @pl.when(i == 0)
        def _():
            bufu[0:8, :] = jnp.zeros((8, W), F32)

        @pl.when(i > 0)
        def _():
            bufu[0:8, :] = up_ref[...]

        bufu[8:8 + tr, :] = u_ref[...]
        cw = cw_ref[...]
        dv = d_ref[...]
        du = dv * cw[3:4, :]
        for j in range(1, 4):
            du = du + bufd[pl.ds(j, tr), :] * cw[3 - j:4 - j, :]
        du_ref[...] = du.astype(BF)
        parts = [jnp.sum(_fold8(dv * bufu[pl.ds(5 + kk, tr), :]), axis=0, keepdims=True) for kk in range(4)]
        part = jnp.concatenate(parts, axis=0)

        @pl.when(i == 0)
        def _():
            acc[...] = part

        @pl.when(i > 0)
        def _():
            acc[...] += part

        @pl.when(i == n - 1)
        def _():
            dcw_ref[...] = acc[...]

    next8 = pl.BlockSpec((8, W), lambda i: (jnp.minimum((i + 1) * (tr // 8), nb8 - 1), 0))
    return _rowcall(
        "conv_bwd", body, S, tr, [dxc, dxc, z, z, conv_w],
        [_rs(tr, W), next8, _rs(tr, W, 1), _prev8_spec(tr, W, 1), _fs((4, W))],
        (_sds((S, W), BF), _sds((4, W), F32)), (_rs(tr, W), _fs((4, W))),
        scratch=[pltpu.VMEM((tr + 8, W), F32), pltpu.VMEM((tr + 8, W), F32), pltpu.VMEM((4, W), F32)])


def _me():
    x = lax.axis_index("x")
    y = lax.axis_index("y")
    c = lax.axis_index("c")
    return x, y, c


def _peer(r):
    x, y, c = _me()
    px = jnp.bitwise_xor(x, (r >> 2) & 1)
    py = jnp.bitwise_xor(y, (r >> 1) & 1)
    pc = jnp.bitwise_xor(c, r & 1)
    return (px, py, pc), 4 * px + 2 * py + pc


def _comm_call(name, arrays, out_shapes, gather):
    na = len(arrays)

    def body(*refs):
        ins = refs[:na]
        outs = refs[na:2 * na]
        send_sems, recv_sems, loc_sems = refs[2 * na:]
        x, y, c = _me()
        me = 4 * x + 2 * y + c
        local = []
        for ai in range(na):
            src = ins[ai] if gather else ins[ai].at[me]
            cp = pltpu.make_async_copy(src, outs[ai].at[me], loc_sems.at[ai])
            cp.start()
            local.append(cp)
        remote = []
        for r in range(1, NDEV):
            dev, idx = _peer(r)
            for ai in range(na):
                src = ins[ai] if gather else ins[ai].at[idx]
                cp = pltpu.make_async_remote_copy(
                    src_ref=src, dst_ref=outs[ai].at[me],
                    send_sem=send_sems.at[ai * (NDEV - 1) + r - 1],
                    recv_sem=recv_sems.at[ai * (NDEV - 1) + r - 1],
                    device_id=dev, device_id_type=pl.DeviceIdType.MESH)
                cp.start()
                remote.append(cp)
        for cp in remote:
            cp.wait()
        for cp in local:
            cp.wait()

    any_spec = pl.BlockSpec(memory_space=pl.ANY)
    return pl.pallas_call(
        body, name=name, in_specs=[any_spec] * na, out_specs=[any_spec] * na,
        out_shape=[_sds(s, a.dtype) for s, a in zip(out_shapes, arrays)],
        scratch_shapes=[pltpu.SemaphoreType.DMA((na * (NDEV - 1),)),
                        pltpu.SemaphoreType.DMA((na * (NDEV - 1),)),
                        pltpu.SemaphoreType.DMA((na,))],
        compiler_params=pltpu.CompilerParams(has_side_effects=True),
    )(*arrays)


def all_gather(name, arrays):
    return _comm_call(name, arrays, [(NDEV,) + a.shape for a in arrays], True)


def all_to_all(name, arrays):
    return _comm_call(name, arrays, [a.shape for a in arrays], False)


def adamw(name, parts, w, m, v, tr):
    L, _, R, C = parts.shape
    tr = _tile(R, tr)

    def body(p_ref, w_ref, m_ref, v_ref, g_ref, d_ref, nm_ref, nv_ref):
        g = p_ref[0]
        for k in range(1, NDEV):
            g = g + p_ref[k]
        g_ref[...] = g
        mn = ADAM_B1 * m_ref[...] + (1.0 - ADAM_B1) * g
        vn = ADAM_B2 * v_ref[...] + (1.0 - ADAM_B2) * (g * g)
        m_hat = mn / (1.0 - ADAM_B1 ** ADAM_STEP)
        v_hat = vn / (1.0 - ADAM_B2 ** ADAM_STEP)
        d_ref[...] = -ADAM_LR * (m_hat / (jnp.sqrt(v_hat) + ADAM_EPS) + ADAM_WD * w_ref[...])
        nm_ref[...] = mn
        nv_ref[...] = vn

    blk = pl.BlockSpec((None, tr, C), lambda l, i: (l, i, 0))
    return pl.pallas_call(
        body, name=name, grid=(L, R // tr),
        in_specs=[pl.BlockSpec((None, NDEV, tr, C), lambda l, i: (l, 0, i, 0)), blk, blk, blk],
        out_specs=(blk,) * 4, out_shape=(_sds((L, R, C), F32),) * 4,
        compiler_params=_params(("parallel", "parallel")),
    )(parts, w, m, v)


def _rot(w):
    h = w.shape[-1] // 2
    return jnp.concatenate([-w[..., h:], w[..., :h]], axis=-1)


def _unrot(dw):
    h = dw.shape[-1] // 2
    return jnp.concatenate([dw[..., h:], -dw[..., :h]], axis=-1)


def _cols(g):
    n, L, R, C = g.shape
    return g.transpose(1, 2, 0, 3).reshape(L, R, n * C)


def _rows(g):
    n, L, R, C = g.shape
    return g.transpose(1, 0, 2, 3).reshape(L, n * R, C)


def _split_cols(dw):
    R, NC = dw.shape
    return dw.reshape(R, NDEV, NC // NDEV).transpose(1, 0, 2)


REPL = ["attn_norm", "q_a_norm", "kv_a_norm", "conv_b", "w_rg", "b_rg", "w_ig", "b_ig", "lru_lambda",
        "ple_norm", "final_norm"]
SHARDED = ["w_in", "w_q_b", "w_kv_b", "conv_w", "w_o_mla", "w_o_lru", "w_out", "w_ple_gate", "w_ple"]
WEIGHTS = ["attn_norm", "w_in", "q_a_norm", "w_q_b", "kv_a_norm", "w_kv_b", "conv_w", "conv_b", "w_rg", "b_rg",
           "w_ig", "b_ig", "lru_lambda", "w_o_mla", "w_o_lru", "w_out", "ple_norm", "w_ple_gate", "w_ple",
           "final_norm"]


def _pack(vals):
    flat = jnp.concatenate([v.reshape(-1) for v in vals])
    n = flat.shape[0]
    rows = -(-n // (128 * 256)) * 256
    return jnp.pad(flat, (0, rows * 128 - n)).reshape(rows, 128)


def _unpack(packed, like):
    flat = packed.reshape(-1)
    out = []
    off = 0
    for v in like:
        out.append(flat[off:off + v.size].reshape(v.shape))
        off += v.size
    return out


def kernel(x, p, positions, attn_norm, w_in, q_a_norm, w_q_b, kv_a_norm, w_kv_b, conv_w, conv_b, w_rg, b_rg, w_ig, b_ig, lru_lambda, w_o_mla, w_o_lru, w_out, ple_norm, w_ple_gate, w_ple, final_norm, loss_target, m_attn_norm, m_w_in, m_q_a_norm, m_w_q_b, m_kv_a_norm, m_w_kv_b, m_conv_w, m_conv_b, m_w_rg, m_b_rg, m_w_ig, m_b_ig, m_lru_lambda, m_w_o_mla, m_w_o_lru, m_w_out, m_ple_norm, m_w_ple_gate, m_w_ple, m_final_norm, v_attn_norm, v_w_in, v_q_a_norm, v_w_q_b, v_kv_a_norm, v_w_kv_b, v_conv_w, v_conv_b, v_w_rg, v_b_rg, v_w_ig, v_b_ig, v_lru_lambda, v_w_o_mla, v_w_o_lru, v_w_out, v_ple_norm, v_w_ple_gate, v_w_ple, v_final_norm):
    W = dict(attn_norm=attn_norm, w_in=w_in, q_a_norm=q_a_norm, w_q_b=w_q_b, kv_a_norm=kv_a_norm, w_kv_b=w_kv_b,
             conv_w=conv_w, conv_b=conv_b, w_rg=w_rg, b_rg=b_rg, w_ig=w_ig, b_ig=b_ig, lru_lambda=lru_lambda,
             w_o_mla=w_o_mla, w_o_lru=w_o_lru, w_out=w_out, ple_norm=ple_norm, w_ple_gate=w_ple_gate, w_ple=w_ple,
             final_norm=final_norm)
    M = dict(attn_norm=m_attn_norm, w_in=m_w_in, q_a_norm=m_q_a_norm, w_q_b=m_w_q_b, kv_a_norm=m_kv_a_norm,
             w_kv_b=m_w_kv_b, conv_w=m_conv_w, conv_b=m_conv_b, w_rg=m_w_rg, b_rg=m_b_rg, w_ig=m_w_ig, b_ig=m_b_ig,
             lru_lambda=m_lru_lambda, w_o_mla=m_w_o_mla, w_o_lru=m_w_o_lru, w_out=m_w_out, ple_norm=m_ple_norm,
             w_ple_gate=m_w_ple_gate, w_ple=m_w_ple, final_norm=m_final_norm)
    V = dict(attn_norm=v_attn_norm, w_in=v_w_in, q_a_norm=v_q_a_norm, w_q_b=v_w_q_b, kv_a_norm=v_kv_a_norm,
             w_kv_b=v_w_kv_b, conv_w=v_conv_w, conv_b=v_conv_b, w_rg=v_w_rg, b_rg=v_b_rg, w_ig=v_w_ig, b_ig=v_b_ig,
             lru_lambda=v_lru_lambda, w_o_mla=v_w_o_mla, w_o_lru=v_w_o_lru, w_out=v_w_out, ple_norm=v_ple_norm,
             w_ple_gate=v_w_ple_gate, w_ple=v_w_ple, final_norm=v_final_norm)

    L = w_in.shape[0]
    S = x.shape[1]
    xs = x[0]
    tgt = loss_target[0]
    tr = _tile(S, max(8, min(256, S // 2)))
    tq = _tile(S, max(128, min(512, S // 2)))

    inv_freq = ROPE_THETA ** (-jnp.arange(0, ROPE, 2, dtype=F32) / ROPE)
    ang = positions[0].astype(F32)[:, None] * inv_freq
    cs = jnp.concatenate([jnp.cos(ang), jnp.cos(ang), jnp.sin(ang), jnp.sin(ang)], axis=1)

    g_in, g_qb, g_kvb, g_om, g_ol, g_out, g_pg, g_ple = all_gather(
        "gather_w", [w_in.astype(BF), w_q_b.astype(BF), w_kv_b.astype(BF), w_o_mla.astype(BF),
                     w_o_lru.astype(BF), w_out.astype(BF), w_ple_gate.astype(BF), w_ple.astype(BF)])
    (g_cw,) = all_gather("gather_conv", [conv_w])
    win = _cols(g_in)
    kr_w = win[..., 1024:1088]
    win_ext = jnp.concatenate([win[..., 1088:], win[..., 0:1024], kr_w, _rot(kr_w),
                               jnp.zeros((L, D, ZW - Z_KR - 128), BF)], axis=-1)
    wq = _cols(g_qb).reshape(L, QL, NH, 192)
    wq_ext = jnp.concatenate([wq, _rot(wq[..., 128:])], axis=-1).reshape(L, QL, NH * 256)
    wkv = _cols(g_kvb)
    wom, wol, wout, wpg = _rows(g_om), _rows(g_ol), _rows(g_out), _rows(g_pg)
    wple = _cols(g_ple)
    cw = _cols(g_cw)
    wrg_b = w_rg.astype(BF)
    wig_b = w_ig.astype(BF)

    def vec(a, l):
        return a[l][None, :]

    saved = []
    xcur = xs
    for l in range(L):
        h = norm_fwd("norm_in", xcur, vec(attn_norm, l), tr)
        z = mm("mm_in", h, win_ext[l], "nn", tn=1280, tk=2048)
        qn, kvn, kr = qkv_prep(z, vec(q_a_norm, l), vec(kv_a_norm, l), cs, tr)
        q_ext = mm("mm_q", qn, wq_ext[l], "nn", o_heads=True, tk=512)
        kv_ext = mm("mm_kv", kvn, wkv[l], "nn", o_heads=True, tk=512, out_dtype=BF)
        o, lse, qf = attn_fwd(q_ext, kv_ext, kr, cs, tq)
        xc, a, hs = lru_fwd(z, cw[l], vec(conv_b, l), wrg_b[l], vec(b_rg, l), wig_b[l], vec(b_ig, l),
                            vec(lru_lambda, l), tr)
        om, hl = gate_out(o, z, hs, tr)
        ym = mm("mm_om", om, wom[l], "nn", tk=2048)
        yl = mm("mm_ol", hl, wol[l], "nn", tk=2048)
        mg = merge_fwd(ym, yl, z, tr)
        x1 = mm("mm_out", mg, wout[l], "nn", tk=2048, res=xcur)
        hp = norm_fwd("norm_ple", x1, vec(ple_norm, l), tr)
        gp = mm("mm_pg", hp, wpg[l], "nn", tk=2048)
        pe = mm("mm_ple", p[l, 0], wple[l], "nn")
        x2 = ple_fin(x1, pe, gp, tr)
        saved.append(dict(x=xcur, h=h, z=z, qn=qn, kvn=kvn, kr=kr, kv_ext=kv_ext, o=o, lse=lse, qf=qf, xc=xc, a=a,
                          hs=hs, om=om, hl=hl, ym=ym, yl=yl, mg=mg, x1=x1, hp=hp, gp=gp, pe=pe))
        xcur = x2

    lsum, dx, d_final = final_loss(xcur, final_norm[None, :], tgt, tr)
    loss = lax.psum(0.5 * lsum[0, 0] / D, MESH_AXES)

    gr = {k: [None] * L for k in WEIGHTS if k != "final_norm"}
    for l in reversed(range(L)):
        sv = saved[l]
        z = sv["z"]
        dpe, dgp = ple_bwd(dx, sv["pe"], sv["gp"], tr)
        gr["w_ple"][l] = mm("mm_dple", p[l, 0], dpe, "tn", tm=256)
        gr["w_ple_gate"][l] = mm("mm_dpg", sv["hp"], dgp, "tn")
        dhp = mm("mm_dhp", dgp, wpg[l], "nt", tk=2048)
        dx1, gr["ple_norm"][l] = norm_bwd("norm_ple_bwd", sv["x1"], dhp, vec(ple_norm, l), dx, tr)
        dmg = mm("mm_dmg", dx1, wout[l], "nt", tk=2048)
        gr["w_out"][l] = mm("mm_dwout", sv["mg"], dx1, "tn")
        dym, dyl, dz_m = merge_bwd(dmg, sv["ym"], sv["yl"], z, tr)
        dom = mm("mm_dom", dym, wom[l], "nt", tk=2048)
        gr["w_o_mla"][l] = mm("mm_dwom", sv["om"], dym, "tn")
        dhl = mm("mm_dhl", dyl, wol[l], "nt", tk=2048)
        gr["w_o_lru"][l] = mm("mm_dwol", sv["hl"], dyl, "tn")
        do, dz_gm, dhs, dz_gl = gate_bwd(dom, sv["o"], dhl, sv["hs"], z, tr)
        dq_ext, dkv_ext, dkr = attn_bwd(sv["qf"], sv["kv_ext"], sv["kr"], sv["o"], do, sv["lse"], cs, tq)
        dqn = mm("mm_dqn", dq_ext, wq_ext[l], "nt", a_heads=True, tn=512)
        dwq_ext = mm("mm_dwq", sv["qn"], dq_ext, "tn", b_heads=True, tm=512)
        dkn = mm("mm_dkn", dkv_ext, wkv[l], "nt", a_heads=True, tn=512)
        gr["w_kv_b"][l] = mm("mm_dwkv", sv["kvn"], dkv_ext, "tn", b_heads=True, tm=512)
        dz_t, gr["q_a_norm"][l], gr["kv_a_norm"][l] = qkv_prep_bwd(
            z, dqn, dkn, dkr, vec(q_a_norm, l), vec(kv_a_norm, l), cs, tr)
        dwq4 = dwq_ext.reshape(QL, NH, 256)
        gr["w_q_b"][l] = jnp.concatenate(
            [dwq4[..., 0:128], dwq4[..., 128:192] + _unrot(dwq4[..., 192:256])], axis=-1).reshape(QL, NH * 192)
        (dxc, gr["w_rg"][l], gr["w_ig"][l], gr["b_rg"][l], gr["b_ig"][l], gr["lru_lambda"][l],
         gr["conv_b"][l]) = lru_bwd(dhs, sv["a"], sv["hs"], sv["xc"], wrg_b[l], vec(b_rg, l), wig_b[l],
                                    vec(b_ig, l), vec(lru_lambda, l), tr)
        dz_u, gr["conv_w"][l] = conv_bwd(dxc, z, cw[l], tr)
        dz = jnp.concatenate([dz_gm, dz_u, dz_gl, dz_m, dz_t], axis=1)
        dwin_ext = mm("mm_dwin", sv["h"], dz, "tn", tn=1280)
        dh = mm("mm_dh", dz, win_ext[l], "nt", tk=1280)
        dx, gr["attn_norm"][l] = norm_bwd("norm_in_bwd", sv["x"], dh, vec(attn_norm, l), dx1, tr)
        gr["w_in"][l] = jnp.concatenate(
            [dwin_ext[:, Z_Q:Z_KR], dwin_ext[:, Z_KR:Z_KR + 64] + _unrot(dwin_ext[:, Z_KR + 64:Z_KR + 128]),
             dwin_ext[:, 0:Z_Q]], axis=1)
    grad_x = dx[None]

    landed = {k: [None] * L for k in SHARDED}
    for l in range(L):
        parts = [_split_cols(gr["w_in"][l]), _split_cols(gr["w_q_b"][l]), _split_cols(gr["w_kv_b"][l]),
                 _split_cols(gr["conv_w"][l]), gr["w_o_mla"][l].reshape(NDEV, D // NDEV, D),
                 gr["w_o_lru"][l].reshape(NDEV, D // NDEV, D), gr["w_out"][l].reshape(NDEV, D // NDEV, D),
                 gr["w_ple_gate"][l].reshape(NDEV, D // NDEV, D), _split_cols(gr["w_ple"][l])]
        got = all_to_all("exchange_grads", parts)
        for k, g_l in zip(SHARDED, got):
            landed[k][l] = g_l
    res = {}
    for k in SHARDED:
        res[k] = adamw("adamw_" + k, jnp.stack(landed[k]), W[k], M[k], V[k], 128)

    rep_g = []
    for k in REPL:
        if k == "final_norm":
            rep_g.append(d_final[0])
        else:
            rep_g.append(jnp.stack(gr[k]).reshape(W[k].shape))
    (rep_all,) = all_gather("gather_rep", [_pack(rep_g)])
    pw, pm, pv = (_pack([T[k] for k in REPL]) for T in (W, M, V))
    rg, rd, rm, rv = adamw("adamw_rep", rep_all[None], pw[None], pm[None], pv[None], 256)
    like = [W[k] for k in REPL]
    for k, g_, d_, m_, v_ in zip(REPL, _unpack(rg[0], like), _unpack(rd[0], like), _unpack(rm[0], like),
                                 _unpack(rv[0], like)):
        res[k] = (g_, d_, m_, v_)

    outs = [loss, grad_x]
    for t in range(4):
        outs += [res[k][t] for k in WEIGHTS]
    return tuple(outs)
```

```python
import functools
import math

import numpy as np
import jax
import jax.numpy as jnp
from jax import lax
from jax.experimental import pallas as pl
from jax.experimental.pallas import tpu as pltpu

F32 = jnp.float32
BF = jnp.bfloat16
MESH_AXES = ("x", "y", "c")
NDEV = 8

D = 2048
NH = 16
QL = 512
KVL = 512
ROPE = 64
PLE = 256
NBLK = 16
BD = 128
CHUNK_SHIFT = 6
EPS = 1e-6
LRU_C = 8.0
ROPE_THETA = 10000.0
IN_TOTAL = 11328
ZW = 11520
Z_Q = 10240
Z_KV = 10752
Z_KR = 11264
SCALE = 1.0 / math.sqrt(128 + 64)

ADAM_LR = 0.001
ADAM_B1 = 0.9
ADAM_B2 = 0.999
ADAM_EPS = 1e-08
ADAM_WD = 0.01
ADAM_STEP = 10

VMEM_LIMIT = 60 * 1024 * 1024

NN = (((1,), (0,)), ((), ()))
NT = (((1,), (1,)), ((), ()))


def _tile(n, pref):
    t = min(n, pref)
    while n % t:
        t //= 2
    return t


def _params(sem):
    return pltpu.CompilerParams(dimension_semantics=sem, vmem_limit_bytes=VMEM_LIMIT)


def _dot(a, b, dims=NN):
    return lax.dot_general(a, b, dims, preferred_element_type=F32)


def _dot_tn(a, b):
    return lax.dot_general(a.T, b, NN, preferred_element_type=F32)


def _bf(v):
    return v if v.dtype == BF else v.astype(BF)


def mm(name, a, b, mode, *, out_dtype=F32, tm=1024, tn=1024, tk=1024, res=None,
       a_heads=False, b_heads=False, o_heads=False):
    if mode == "nn":
        M = a.shape[0]
        K = a.shape[1]
        N = b.shape[1]
    elif mode == "nt":
        M = a.shape[1] if a_heads else a.shape[0]
        K = a.shape[0] * a.shape[2] if a_heads else a.shape[1]
        N = b.shape[0]
    else:
        K = a.shape[0]
        M = a.shape[1]
        N = b.shape[0] * b.shape[2] if b_heads else b.shape[1]
    tm = _tile(M, tm)
    tn = 256 if (o_heads or b_heads) else _tile(N, tn)
    tk = 256 if a_heads else _tile(K, tk)
    nk = K // tk
    grid = (M // tm, N // tn, nk)

    if mode == "nn":
        a_spec = pl.BlockSpec((tm, tk), lambda i, j, k: (i, k))
        b_spec = pl.BlockSpec((tk, tn), lambda i, j, k: (k, j))
    elif mode == "nt":
        if a_heads:
            a_spec = pl.BlockSpec((None, tm, 256), lambda i, j, k: (k, i, 0))
        else:
            a_spec = pl.BlockSpec((tm, tk), lambda i, j, k: (i, k))
        b_spec = pl.BlockSpec((tn, tk), lambda i, j, k: (j, k))
    else:
        a_spec = pl.BlockSpec((tk, tm), lambda i, j, k: (k, i))
        if b_heads:
            b_spec = pl.BlockSpec((None, tk, 256), lambda i, j, k: (j, k, 0))
        else:
            b_spec = pl.BlockSpec((tk, tn), lambda i, j, k: (k, j))
    if o_heads:
        o_spec = pl.BlockSpec((None, tm, 256), lambda i, j, k: (j, i, 0))
        o_shape = jax.ShapeDtypeStruct((N // 256, M, 256), out_dtype)
    else:
        o_spec = pl.BlockSpec((tm, tn), lambda i, j, k: (i, j))
        o_shape = jax.ShapeDtypeStruct((M, N), out_dtype)
    in_specs = [a_spec, b_spec]
    args = [a, b]
    if res is not None:
        in_specs.append(pl.BlockSpec((tm, tn), lambda i, j, k: (i, j)))
        args.append(res)

    def body(*refs):
        a_ref, b_ref = refs[0], refs[1]
        r_ref = refs[2] if res is not None else None
        o_ref = refs[3] if res is not None else refs[2]
        av = _bf(a_ref[...])
        bv = _bf(b_ref[...])
        if mode == "nn":
            p = _dot(av, bv, NN)
        elif mode == "nt":
            p = _dot(av, bv, NT)
        else:
            p = _dot_tn(av, bv)

        def fin(acc):
            if r_ref is not None:
                acc = acc + r_ref[...]
            o_ref[...] = acc.astype(o_ref.dtype)

        if nk == 1:
            fin(p)
        else:
            acc_ref = refs[-1]
            k = pl.program_id(2)

            @pl.when(k == 0)
            def _():
                acc_ref[...] = p

            @pl.when(k > 0)
            def _():
                acc_ref[...] += p

            @pl.when(k == nk - 1)
            def _():
                fin(acc_ref[...])

    scratch = [] if nk == 1 else [pltpu.VMEM((tm, tn), F32)]
    return pl.pallas_call(
        body, name=name, grid=grid, in_specs=in_specs, out_specs=o_spec, out_shape=o_shape,
        scratch_shapes=scratch, compiler_params=_params(("parallel", "parallel", "arbitrary")),
    )(*args)


def _rs(tr, w, cb=0):
    return pl.BlockSpec((tr, w), lambda i: (i, cb))


def _fs(shape):
    nd = len(shape)
    return pl.BlockSpec(shape, lambda i: (0,) * nd)


def _rowcall(name, body, S, tr, ins, in_specs, outs, out_specs, scratch=()):
    return pl.pallas_call(
        body, name=name, grid=(S // tr,), in_specs=in_specs, out_specs=out_specs, out_shape=outs,
        scratch_shapes=list(scratch), compiler_params=_params(("arbitrary",)),
    )(*ins)


def _sds(shape, dt):
    return jax.ShapeDtypeStruct(shape, dt)


def _fold8(v):
    tr, w = v.shape
    return jnp.sum(v.reshape(tr // 8, 8, w), axis=0)


def _acc_rows(i, n, acc_ref, out_ref, part):
    @pl.when(i == 0)
    def _():
        acc_ref[...] = part

    @pl.when(i > 0)
    def _():
        acc_ref[...] += part

    @pl.when(i == n - 1)
    def _():
        out_ref[...] = jnp.sum(acc_ref[...], axis=0, keepdims=True)


def _rms_fwd(x, g):
    r = lax.rsqrt(jnp.mean(x * x, axis=-1, keepdims=True) + EPS)
    return x * r * g


def _rms_bwd(x, dy, g):
    r = lax.rsqrt(jnp.mean(x * x, axis=-1, keepdims=True) + EPS)
    xh = x * r
    dxh = dy * g
    dx = r * (dxh - xh * jnp.mean(dxh * xh, axis=-1, keepdims=True))
    return dx, dy * xh


def _sig(x):
    return jax.nn.sigmoid(x)


def norm_fwd(name, x, g, tr):
    S, W = x.shape

    def body(x_ref, g_ref, o_ref):
        o_ref[...] = _rms_fwd(x_ref[...], g_ref[...]).astype(BF)

    return _rowcall(name, body, S, tr, [x, g], [_rs(tr, W), _fs((1, W))], _sds((S, W), BF), _rs(tr, W))


def norm_bwd(name, x, dy, g, dres, tr):
    S, W = x.shape
    n = S // tr

    def body(x_ref, dy_ref, g_ref, dr_ref, dx_ref, dg_ref, acc_ref):
        i = pl.program_id(0)
        dx, dgp = _rms_bwd(x_ref[...], dy_ref[...], g_ref[...])
        dx_ref[...] = dr_ref[...] + dx
        _acc_rows(i, n, acc_ref, dg_ref, _fold8(dgp))

    return _rowcall(name, body, S, tr, [x, dy, g, dres],
                    [_rs(tr, W), _rs(tr, W), _fs((1, W)), _rs(tr, W)],
                    (_sds((S, W), F32), _sds((1, W), F32)), (_rs(tr, W), _fs((1, W))),
                    scratch=[pltpu.VMEM((8, W), F32)])


def _lane_lt(shape, n):
    return lax.broadcasted_iota(jnp.int32, shape, 1) < n


def qkv_prep(z, qg, kvg, cs, tr):
    S = z.shape[0]

    def body(ql_ref, kl_ref, zk_ref, qg_ref, kg_ref, cs_ref, qn_ref, kn_ref, kr_ref):
        qn_ref[...] = _rms_fwd(ql_ref[...], qg_ref[...]).astype(BF)
        kn_ref[...] = _rms_fwd(kl_ref[...], kg_ref[...]).astype(BF)
        t = zk_ref[...] * cs_ref[...]
        t = t + pltpu.roll(t, 64, 1)
        kr_ref[...] = jnp.where(_lane_lt(t.shape, 64), t, 0.0).astype(BF)

    return _rowcall(
        "qkv_prep", body, S, tr, [z, z, z, qg, kvg, cs],
        [_rs(tr, QL, Z_Q // QL), _rs(tr, KVL, Z_KV // KVL), _rs(tr, 128, Z_KR // 128),
         _fs((1, QL)), _fs((1, KVL)), _rs(tr, 128)],
        (_sds((S, QL), BF), _sds((S, KVL), BF), _sds((S, 128), BF)),
        (_rs(tr, QL), _rs(tr, KVL), _rs(tr, 128)))


def qkv_prep_bwd(z, dqn, dkn, dkr, qg, kvg, cs, tr):
    S = z.shape[0]
    n = S // tr

    def body(ql_ref, kl_ref, dq_ref, dk_ref, dkr_ref, qg_ref, kg_ref, cs_ref,
             dz_ref, dqg_ref, dkg_ref, accq, acck):
        i = pl.program_id(0)
        dql, gq = _rms_bwd(ql_ref[...], dq_ref[...], qg_ref[...])
        dkl, gk = _rms_bwd(kl_ref[...], dk_ref[...], kg_ref[...])
        t = jnp.where(_lane_lt((tr, 128), 64), dkr_ref[...], 0.0)
        t = (t + pltpu.roll(t, 64, 1)) * cs_ref[...]
        dz_ref[...] = jnp.concatenate(
            [dql.astype(BF), dkl.astype(BF), t.astype(BF), jnp.zeros((tr, 128), BF)], axis=1)
        _acc_rows(i, n, accq, dqg_ref, _fold8(gq))
        _acc_rows(i, n, acck, dkg_ref, _fold8(gk))

    return _rowcall(
        "qkv_prep_bwd", body, S, tr, [z, z, dqn, dkn, dkr, qg, kvg, cs],
        [_rs(tr, QL, Z_Q // QL), _rs(tr, KVL, Z_KV // KVL), _rs(tr, QL), _rs(tr, KVL), _rs(tr, 128),
         _fs((1, QL)), _fs((1, KVL)), _rs(tr, 128)],
        (_sds((S, ZW - Z_Q), BF), _sds((1, QL), F32), _sds((1, KVL), F32)),
        (_rs(tr, ZW - Z_Q), _fs((1, QL)), _fs((1, KVL))),
        scratch=[pltpu.VMEM((8, QL), F32), pltpu.VMEM((8, KVL), F32)])


def gate_out(o, z, hs, tr):
    S = o.shape[0]

    def body(o_ref, gm_ref, h_ref, gl_ref, om_ref, hl_ref):
        gm = gm_ref[...]
        gl = gl_ref[...]
        om_ref[...] = (o_ref[...] * (gm * _sig(gm))).astype(BF)
        hl_ref[...] = (h_ref[...] * (gl * _sig(gl))).astype(BF)

    return _rowcall("gate_out", body, S, tr, [o, z, hs, z],
                    [_rs(tr, D), _rs(tr, D, 0), _rs(tr, D), _rs(tr, D, 2)],
                    (_sds((S, D), BF), _sds((S, D), BF)), (_rs(tr, D), _rs(tr, D)))


def gate_bwd(dom, o, dhl, hs, z, tr):
    S = o.shape[0]

    def body(dom_ref, o_ref, dhl_ref, h_ref, gm_ref, gl_ref, do_ref, dgm_ref, dh_ref, dgl_ref):
        gm = gm_ref[...]
        sm = _sig(gm)
        dom_v = dom_ref[...]
        do_ref[...] = dom_v * (gm * sm)
        dgm_ref[...] = (dom_v * o_ref[...] * (sm * (1.0 + gm * (1.0 - sm)))).astype(BF)
        gl = gl_ref[...]
        sl = _sig(gl)
        dhl_v = dhl_ref[...]
        dh_ref[...] = dhl_v * (gl * sl)
        dgl_ref[...] = (dhl_v * h_ref[...] * (sl * (1.0 + gl * (1.0 - sl)))).astype(BF)

    return _rowcall("gate_bwd", body, S, tr, [dom, o, dhl, hs, z, z],
                    [_rs(tr, D), _rs(tr, D), _rs(tr, D), _rs(tr, D), _rs(tr, D, 0), _rs(tr, D, 2)],
                    (_sds((S, D), F32), _sds((S, D), BF), _sds((S, D), F32), _sds((S, D), BF)),
                    (_rs(tr, D),) * 4)


def merge_fwd(ym, yl, z, tr):
    S = ym.shape[0]

    def body(ym_ref, yl_ref, mm_ref, ml_ref, o_ref):
        o_ref[...] = (_sig(mm_ref[...]) * ym_ref[...] + _sig(ml_ref[...]) * yl_ref[...]).astype(BF)

    return _rowcall("merge_fwd", body, S, tr, [ym, yl, z, z],
                    [_rs(tr, D), _rs(tr, D), _rs(tr, D, 3), _rs(tr, D, 4)],
                    _sds((S, D), BF), _rs(tr, D))


def merge_bwd(dmg, ym, yl, z, tr):
    S = ym.shape[0]

    def body(d_ref, ym_ref, yl_ref, mm_ref, ml_ref, dym_ref, dyl_ref, dz_ref):
        d = d_ref[...]
        sm = _sig(mm_ref[...])
        sl = _sig(ml_ref[...])
        dym_ref[...] = (d * sm).astype(BF)
        dyl_ref[...] = (d * sl).astype(BF)
        dz_ref[:, 0:D] = (d * ym_ref[...] * (sm * (1.0 - sm))).astype(BF)
        dz_ref[:, D:2 * D] = (d * yl_ref[...] * (sl * (1.0 - sl))).astype(BF)

    return _rowcall("merge_bwd", body, S, tr, [dmg, ym, yl, z, z],
                    [_rs(tr, D), _rs(tr, D), _rs(tr, D), _rs(tr, D, 3), _rs(tr, D, 4)],
                    (_sds((S, D), BF), _sds((S, D), BF), _sds((S, 2 * D), BF)),
                    (_rs(tr, D), _rs(tr, D), _rs(tr, 2 * D)))


def ple_fin(x1, pe, gp, tr):
    S = x1.shape[0]

    def body(x_ref, pe_ref, gp_ref, o_ref):
        o_ref[...] = x_ref[...] + pe_ref[...] * _sig(gp_ref[...])

    return _rowcall("ple_fin", body, S, tr, [x1, pe, gp], [_rs(tr, D)] * 3, _sds((S, D), F32), _rs(tr, D))


def ple_bwd(dx2, pe, gp, tr):
    S = dx2.shape[0]

    def body(d_ref, pe_ref, gp_ref, dpe_ref, dgp_ref):
        d = d_ref[...]
        sg = _sig(gp_ref[...])
        dpe_ref[...] = (d * sg).astype(BF)
        dgp_ref[...] = (d * pe_ref[...] * (sg * (1.0 - sg))).astype(BF)

    return _rowcall("ple_bwd", body, S, tr, [dx2, pe, gp], [_rs(tr, D)] * 3,
                    (_sds((S, D), BF), _sds((S, D), BF)), (_rs(tr, D), _rs(tr, D)))


def final_loss(x, g, tgt, tr):
    S, W = x.shape
    n = S // tr

    def body(x_ref, g_ref, t_ref, ls_ref, dx_ref, dg_ref, accl, accg):
        i = pl.program_id(0)
        xv = x_ref[...]
        gv = g_ref[...]
        e = _rms_fwd(xv, gv) - t_ref[...]
        e2 = _fold8(e * e)
        l8 = e2[:, 0:128]
        for k in range(1, W // 128):
            l8 = l8 + e2[:, k * 128:(k + 1) * 128]
        dx, dgp = _rms_bwd(xv, e * (1.0 / W), gv)
        dx_ref[...] = dx
        _acc_rows(i, n, accg, dg_ref, _fold8(dgp))

        @pl.when(i == 0)
        def _():
            accl[...] = l8

        @pl.when(i > 0)
        def _():
            accl[...] += l8

        @pl.when(i == n - 1)
        def _():
            tot = jnp.sum(jnp.sum(accl[...], axis=0, keepdims=True), axis=1, keepdims=True)
            ls_ref[...] = jnp.broadcast_to(tot, (1, 128))

    return _rowcall("final_loss", body, S, tr, [x, g, tgt], [_rs(tr, W), _fs((1, W)), _rs(tr, W)],
                    (_sds((1, 128), F32), _sds((S, W), F32), _sds((1, W), F32)),
                    (_fs((1, 128)), _rs(tr, W), _fs((1, W))),
                    scratch=[pltpu.VMEM((8, 128), F32), pltpu.VMEM((8, W), F32)])


def _chunk_mask(qi, kj, tq, tk):
    rows = qi * tq + lax.broadcasted_iota(jnp.int32, (tq, tk), 0)
    cols = kj * tk + lax.broadcasted_iota(jnp.int32, (tq, tk), 1)
    return (cols >> CHUNK_SHIFT) <= (rows >> CHUNK_SHIFT)


def _load_kv(kv_ref, kr_ref, kj, tk):
    off = pl.multiple_of(kj * tk, tk)
    kv = kv_ref[pl.ds(off, tk), :]
    k = jnp.concatenate([kv[:, 0:128], kr_ref[pl.ds(off, tk), :]], axis=1)
    return k, kv[:, 128:256], off


def _split_refs(refs, n_in, n_out, comm):
    nc = comm.n if comm is not None else 0
    ins = refs[:n_in]
    c_in = refs[n_in:n_in + nc]
    outs = refs[n_in + nc:n_in + nc + n_out]
    c_out = refs[n_in + nc + n_out:n_in + 2 * nc + n_out]
    rest = refs[n_in + 2 * nc + n_out:]
    if comm is None:
        return ins, outs, rest, None
    return ins, outs, rest[:len(rest) - 3], comm.copies(c_in, c_out, *rest[len(rest) - 3:])


def _side_start(cps, first):
    if cps is None:
        return

    @pl.when(first)
    def _():
        for cp in cps:
            cp.start()


def _side_wait(cps, last):
    if cps is None:
        return

    @pl.when(last)
    def _():
        for cp in cps:
            cp.wait()


def attn_fwd(q_ext, kv_ext, kr, cs, tq, comm=None):
    H, S, _ = q_ext.shape
    nq = S // tq
    tk = tq
    nc = comm.n if comm is not None else 0

    def body(*refs):
        (q_ref, kv_ref, kr_ref, cs_ref), (o_ref, lse_ref, qf_ref), _, cps = _split_refs(refs, 4, 3, comm)
        hh = pl.program_id(0)
        qi = pl.program_id(1)
        _side_start(cps, jnp.logical_and(hh == 0, qi == 0))
        q = q_ref[...]
        hi = q[:, 128:256] * cs_ref[...]
        hi = hi + pltpu.roll(hi, 64, 1)
        hi = jnp.where(_lane_lt(hi.shape, 64), hi, 0.0)
        qf = jnp.concatenate([q[:, 0:128], hi], axis=1).astype(BF)
        qf_ref[...] = qf

        def step(kj, carry, masked):
            m, l, acc = carry
            k, v, _ = _load_kv(kv_ref, kr_ref, kj, tk)
            s = _dot(qf, k, NT) * SCALE
            if masked:
                s = jnp.where(_chunk_mask(qi, kj, tq, tk), s, -jnp.inf)
            m_new = jnp.maximum(m, jnp.max(s, axis=1, keepdims=True))
            alpha = jnp.exp(m - m_new)
            p = jnp.exp(s - m_new)
            l = alpha * l + jnp.sum(p, axis=1, keepdims=True)
            acc = alpha * acc + _dot(p.astype(BF), v, NN)
            return m_new, l, acc

        init = (jnp.full((tq, 1), -jnp.inf, F32), jnp.zeros((tq, 1), F32), jnp.zeros((tq, 128), F32))
        carry = lax.fori_loop(0, qi, lambda kj, c: step(kj, c, False), init)
        m, l, acc = step(qi, carry, True)
        o_ref[...] = acc / l
        lse_ref[...] = jnp.broadcast_to(m + jnp.log(l), (tq, 128))
        _side_wait(cps, jnp.logical_and(hh == H - 1, qi == nq - 1))

    res = pl.pallas_call(
        body, name="attn_fwd" if comm is None else "attn_fwd_comm", grid=(H, nq),
        in_specs=[pl.BlockSpec((None, tq, 256), lambda h, i: (h, i, 0)),
                  pl.BlockSpec((None, S, 256), lambda h, i: (h, 0, 0)),
                  pl.BlockSpec((S, 128), lambda h, i: (0, 0)),
                  pl.BlockSpec((tq, 128), lambda h, i: (i, 0))] + [ANY_SPEC] * nc,
        out_specs=[pl.BlockSpec((tq, 128), lambda h, i: (i, h)),
                   pl.BlockSpec((None, tq, 128), lambda h, i: (h, i, 0)),
                   pl.BlockSpec((None, tq, 256), lambda h, i: (h, i, 0))] + [ANY_SPEC] * nc,
        out_shape=[_sds((S, H * 128), F32), _sds((H, S, 128), F32), _sds((H, S, 256), BF)]
        + (comm.out_shapes() if comm is not None else []),
        scratch_shapes=comm.scratch() if comm is not None else [],
        compiler_params=_params(("arbitrary", "arbitrary")),
    )(q_ext, kv_ext, kr, cs, *(comm.arrays if comm is not None else []))
    return res[:3], res[3:]


def attn_bwd(qf, kv_ext, kr, o, do, lse, cs, tq, comm=None):
    H, S, _ = qf.shape
    nq = S // tq
    tk = tq
    nc = comm.n if comm is not None else 0

    def body(*refs):
        ((q_ref, kv_ref, kr_ref, o_ref, do_ref, lse_ref, cs_ref), (dq_ref, dkv_ref, dkr_ref),
         (dk_acc, dv_acc), cps) = _split_refs(refs, 7, 3, comm)
        h = pl.program_id(0)
        qi = pl.program_id(1)
        _side_start(cps, jnp.logical_and(h == 0, qi == 0))

        @pl.when(qi == 0)
        def _():
            dk_acc[...] = jnp.zeros_like(dk_acc)
            dv_acc[...] = jnp.zeros_like(dv_acc)

        @pl.when(jnp.logical_and(h == 0, qi == 0))
        def _():
            dkr_ref[...] = jnp.zeros_like(dkr_ref)

        q = q_ref[...]
        dov = do_ref[...]
        delta = jnp.sum(dov * o_ref[...], axis=1, keepdims=True)
        lse_v = lse_ref[:, 0:1]
        do_b = dov.astype(BF)

        def step(kj, dq, masked):
            k, v, off = _load_kv(kv_ref, kr_ref, kj, tk)
            s = _dot(q, k, NT) * SCALE
            if masked:
                s = jnp.where(_chunk_mask(qi, kj, tq, tk), s, -jnp.inf)
            p = jnp.exp(s - lse_v)
            dp = _dot(do_b, v, NT)
            ds = (p * (dp - delta) * SCALE).astype(BF)
            dv_acc[pl.ds(off, tk), :] += _dot_tn(p.astype(BF), do_b)
            dk_acc[pl.ds(off, tk), :] += _dot_tn(ds, q)
            return dq + _dot(ds, k, NN)

        dq = lax.fori_loop(0, qi, lambda kj, c: step(kj, c, False), jnp.zeros((tq, 256), F32))
        dq = step(qi, dq, True)
        hi = dq[:, 128:256]
        hi = (hi + pltpu.roll(hi, 64, 1)) * cs_ref[...]
        dq_ref[...] = jnp.concatenate([dq[:, 0:128], hi], axis=1).astype(BF)

        @pl.when(qi == nq - 1)
        def _():
            dkv_ref[...] = jnp.concatenate([dk_acc[:, 0:128], dv_acc[...]], axis=1).astype(BF)
            dkr_ref[...] += dk_acc[:, 128:256]

        _side_wait(cps, jnp.logical_and(h == H - 1, qi == nq - 1))

    res = pl.pallas_call(
        body, name="attn_bwd" if comm is None else "attn_bwd_comm", grid=(H, nq),
        in_specs=[pl.BlockSpec((None, tq, 256), lambda h, i: (h, i, 0)),
                  pl.BlockSpec((None, S, 256), lambda h, i: (h, 0, 0)),
                  pl.BlockSpec((S, 128), lambda h, i: (0, 0)),
                  pl.BlockSpec((tq, 128), lambda h, i: (i, h)),
                  pl.BlockSpec((tq, 128), lambda h, i: (i, h)),
                  pl.BlockSpec((None, tq, 128), lambda h, i: (h, i, 0)),
                  pl.BlockSpec((tq, 128), lambda h, i: (i, 0))] + [ANY_SPEC] * nc,
        out_specs=[pl.BlockSpec((None, tq, 256), lambda h, i: (h, i, 0)),
                   pl.BlockSpec((None, S, 256), lambda h, i: (h, 0, 0)),
                   pl.BlockSpec((S, 128), lambda h, i: (0, 0))] + [ANY_SPEC] * nc,
        out_shape=[_sds((H, S, 256), BF), _sds((H, S, 256), BF), _sds((S, 128), F32)]
        + (comm.out_shapes() if comm is not None else []),
        scratch_shapes=[pltpu.VMEM((S, 256), F32), pltpu.VMEM((S, 128), F32)]
        + (comm.scratch() if comm is not None else []),
        compiler_params=_params(("arbitrary", "arbitrary")),
    )(qf, kv_ext, kr, o, do, lse, cs, *(comm.arrays if comm is not None else []))
    return res[:3], res[3:]


def _log1p(y):
    w = 1.0 + y
    return jnp.where(w == 1.0, y, jnp.log(w) * (y / (w - 1.0)))


def _expm1(x):
    u = jnp.exp(x)
    return jnp.where(u == 1.0, x, (u - 1.0) * (x / jnp.log(u)))


def _softplus(x):
    return jnp.maximum(x, 0.0) + _log1p(jnp.exp(-jnp.abs(x)))


def _blockdiag(xb, w_ref):
    return jnp.concatenate(
        [_dot(xb[:, k * BD:(k + 1) * BD], w_ref[k], NN) for k in range(NBLK)], axis=1)


def _gates(xc, wr_ref, br_ref, wi_ref, bi_ref, sp):
    xb = xc.astype(BF)
    r = _sig(_blockdiag(xb, wr_ref) + br_ref[...])
    ig = _sig(_blockdiag(xb, wi_ref) + bi_ref[...])
    log_a = (-LRU_C * r) * sp
    mult = jnp.sqrt(-_expm1(2.0 * log_a))
    return xb, r, ig, log_a, mult


def _prev8_spec(tr, w, cb):
    return pl.BlockSpec((8, w), lambda i: (jnp.maximum(i * (tr // 8) - 1, 0), cb))


def lru_fwd(z, conv_w, conv_b, w_rg, b_rg, w_ig, b_ig, lam, tr):
    S = z.shape[0]
    W = D

    def body(u_ref, up_ref, cw_ref, cb_ref, wr_ref, br_ref, wi_ref, bi_ref, lam_ref,
             xc_ref, a_ref, h_ref, buf, bt, hcar):
        i = pl.program_id(0)

        @pl.when(i == 0)
        def _():
            buf[0:8, :] = jnp.zeros((8, W), F32)
            hcar[...] = jnp.zeros_like(hcar)

        @pl.when(i > 0)
        def _():
            buf[0:8, :] = up_ref[...]

        buf[8:8 + tr, :] = u_ref[...]
        cw = cw_ref[...]
        xc = buf[pl.ds(5, tr), :] * cw[0:1, :]
        for kk in range(1, 4):
            xc = xc + buf[pl.ds(5 + kk, tr), :] * cw[kk:kk + 1, :]
        xc = xc + cb_ref[...]
        xc_ref[...] = xc
        sp = _softplus(-lam_ref[...])
        _, _, ig, log_a, mult = _gates(xc, wr_ref, br_ref, wi_ref, bi_ref, sp)
        a_ref[...] = jnp.exp(log_a)
        bt[...] = mult * (ig * xc)
        row = lax.broadcasted_iota(jnp.int32, (8, W), 0)

        def grp(g, hp):
            off = pl.multiple_of(g * 8, 8)
            A = a_ref[pl.ds(off, 8), :]
            B = bt[pl.ds(off, 8), :]
            for d in (1, 2, 4):
                ok = row >= d
                B = jnp.where(ok, A * pltpu.roll(B, d, 0) + B, B)
                A = jnp.where(ok, A * pltpu.roll(A, d, 0), A)
            hh = A * hp + B
            h_ref[pl.ds(off, 8), :] = hh
            return hh[7:8, :]

        hcar[...] = lax.fori_loop(0, tr // 8, grp, hcar[...])

    return _rowcall(
        "lru_fwd", body, S, tr, [z, z, conv_w, conv_b, w_rg, b_rg, w_ig, b_ig, lam],
        [_rs(tr, W, 1), _prev8_spec(tr, W, 1), _fs((4, W)), _fs((1, W)), _fs((NBLK, BD, BD)), _fs((1, W)),
         _fs((NBLK, BD, BD)), _fs((1, W)), _fs((1, W))],
        (_sds((S, W), F32),) * 3, (_rs(tr, W),) * 3,
        scratch=[pltpu.VMEM((tr + 8, W), F32), pltpu.VMEM((tr, W), F32), pltpu.VMEM((1, W), F32)])


def lru_bwd(dh, a, hs, xc, w_rg, b_rg, w_ig, b_ig, lam, tr):
    S, W = dh.shape
    n = S // tr
    nb8 = S // 8

    def rev(i):
        return n - 1 - i

    row_spec = pl.BlockSpec((tr, W), lambda i: (rev(i), 0))
    next8 = pl.BlockSpec((8, W), lambda i: (jnp.minimum((rev(i) + 1) * (tr // 8), nb8 - 1), 0))
    prev8 = pl.BlockSpec((8, W), lambda i: (jnp.maximum(rev(i) * (tr // 8) - 1, 0), 0))

    def body(dh_ref, a_ref, an_ref, h_ref, hp_ref, xc_ref, wr_ref, br_ref, wi_ref, bi_ref, lam_ref,
             dxc_ref, dwr_ref, dwi_ref, dbr_ref, dbi_ref, dlam_ref, dcb_ref,
             bufa, bufh, apr, gsc, gcar, acc_br, acc_bi, acc_sp, acc_cb):
        i = pl.program_id(0)
        first = i == 0
        last_tile = i == n - 1

        bufa[0:tr, :] = a_ref[...]
        bufa[tr:tr + 8, :] = an_ref[...]
        apr[...] = bufa[pl.ds(1, tr), :]
        bufh[8:8 + tr, :] = h_ref[...]

        @pl.when(last_tile)
        def _():
            bufh[0:8, :] = jnp.zeros((8, W), F32)

        @pl.when(jnp.logical_not(last_tile))
        def _():
            bufh[0:8, :] = hp_ref[...]

        @pl.when(first)
        def _():
            gcar[...] = jnp.zeros_like(gcar)

        row = lax.broadcasted_iota(jnp.int32, (8, W), 0)
        ng = tr // 8

        def grp(t, gn):
            g = ng - 1 - t
            off = pl.multiple_of(g * 8, 8)
            A = apr[pl.ds(off, 8), :]
            B = dh_ref[pl.ds(off, 8), :]
            for d in (1, 2, 4):
                ok = row < 8 - d
                B = jnp.where(ok, B + A * pltpu.roll(B, 8 - d, 0), B)
                A = jnp.where(ok, A * pltpu.roll(A, 8 - d, 0), A)
            gg = B + A * gn
            gsc[pl.ds(off, 8), :] = gg
            return gg[0:1, :]

        gcar[...] = lax.fori_loop(0, ng, grp, gcar[...])

        G = gsc[...]
        hprev = bufh[pl.ds(7, tr), :]
        xcv = xc_ref[...]
        lam_v = lam_ref[...]
        sp = _softplus(-lam_v)
        xb, r, ig, log_a, mult = _gates(xcv, wr_ref, br_ref, wi_ref, bi_ref, sp)
        av = a_ref[...]
        d_a = G * hprev
        d_mult = G * (ig * xcv)
        d_ig = G * (mult * xcv)
        d_log_a = d_a * av - d_mult * (av * av / mult)
        d_gr = (d_log_a * (-LRU_C * sp)) * (r * (1.0 - r))
        d_gi = d_ig * (ig * (1.0 - ig))
        gr_b = d_gr.astype(BF)
        gi_b = d_gi.astype(BF)
        dxc = G * (mult * ig)
        dxc = dxc + jnp.concatenate(
            [_dot(gr_b[:, k * BD:(k + 1) * BD], wr_ref[k], NT)
             + _dot(gi_b[:, k * BD:(k + 1) * BD], wi_ref[k], NT) for k in range(NBLK)], axis=1)
        dxc_ref[...] = dxc

        @pl.when(first)
        def _():
            dwr_ref[...] = jnp.zeros_like(dwr_ref)
            dwi_ref[...] = jnp.zeros_like(dwi_ref)

        for k in range(NBLK):
            sl = slice(k * BD, (k + 1) * BD)
            dwr_ref[k] += _dot_tn(xb[:, sl], gr_b[:, sl])
            dwi_ref[k] += _dot_tn(xb[:, sl], gi_b[:, sl])

        _acc_rows(i, n, acc_br, dbr_ref, _fold8(d_gr))
        _acc_rows(i, n, acc_bi, dbi_ref, _fold8(d_gi))
        _acc_rows(i, n, acc_cb, dcb_ref, _fold8(dxc))

        part = _fold8(d_log_a * (-LRU_C * r))

        @pl.when(first)
        def _():
            acc_sp[...] = part

        @pl.when(jnp.logical_not(first))
        def _():
            acc_sp[...] += part

        @pl.when(last_tile)
        def _():
            dsp = jnp.sum(acc_sp[...], axis=0, keepdims=True)
            dlam_ref[...] = dsp * (-_sig(-lam_v))

    vec = _sds((1, W), F32)
    wsh = _sds((NBLK, BD, BD), F32)
    return _rowcall(
        "lru_bwd", body, S, tr, [dh, a, a, hs, hs, xc, w_rg, b_rg, w_ig, b_ig, lam],
        [row_spec, row_spec, next8, row_spec, prev8, row_spec, _fs((NBLK, BD, BD)), _fs((1, W)),
         _fs((NBLK, BD, BD)), _fs((1, W)), _fs((1, W))],
        (_sds((S, W), F32), wsh, wsh, vec, vec, vec, vec),
        (row_spec, _fs((NBLK, BD, BD)), _fs((NBLK, BD, BD)), _fs((1, W)), _fs((1, W)), _fs((1, W)), _fs((1, W))),
        scratch=[pltpu.VMEM((tr + 8, W), F32), pltpu.VMEM((tr + 8, W), F32), pltpu.VMEM((tr, W), F32),
                 pltpu.VMEM((tr, W), F32), pltpu.VMEM((1, W), F32), pltpu.VMEM((8, W), F32),
                 pltpu.VMEM((8, W), F32), pltpu.VMEM((8, W), F32), pltpu.VMEM((8, W), F32)])


def conv_bwd(dxc, z, conv_w, tr):
    S, W = dxc.shape
    n = S // tr
    nb8 = S // 8

    def body(d_ref, dn_ref, u_ref, up_ref, cw_ref, du_ref, dcw_ref, bufd, bufu, acc):
        i = pl.program_id(0)
        bufd[0:tr, :] = d_ref[...]

        @pl.when(i == n - 1)
        def _():
            bufd[tr:tr + 8, :] = jnp.zeros((8, W), F32)

        @pl.when(i < n - 1)
        def _():
            bufd[tr:tr + 8, :] = dn_ref[...]

        @pl.when(i == 0)
        def _():
            bufu[0:8, :] = jnp.zeros((8, W), F32)

        @pl.when(i > 0)
        def _():
            bufu[0:8, :] = up_ref[...]

        bufu[8:8 + tr, :] = u_ref[...]
        cw = cw_ref[...]
        dv = d_ref[...]
        du = dv * cw[3:4, :]
        for j in range(1, 4):
            du = du + bufd[pl.ds(j, tr), :] * cw[3 - j:4 - j, :]
        du_ref[...] = du.astype(BF)
        parts = [jnp.sum(_fold8(dv * bufu[pl.ds(5 + kk, tr), :]), axis=0, keepdims=True) for kk in range(4)]
        part = jnp.concatenate(parts, axis=0)

        @pl.when(i == 0)
        def _():
            acc[...] = part

        @pl.when(i > 0)
        def _():
            acc[...] += part

        @pl.when(i == n - 1)
        def _():
            dcw_ref[...] = acc[...]

    next8 = pl.BlockSpec((8, W), lambda i: (jnp.minimum((i + 1) * (tr // 8), nb8 - 1), 0))
    return _rowcall(
        "conv_bwd", body, S, tr, [dxc, dxc, z, z, conv_w],
        [_rs(tr, W), next8, _rs(tr, W, 1), _prev8_spec(tr, W, 1), _fs((4, W))],
        (_sds((S, W), BF), _sds((4, W), F32)), (_rs(tr, W), _fs((4, W))),
        scratch=[pltpu.VMEM((tr + 8, W), F32), pltpu.VMEM((tr + 8, W), F32), pltpu.VMEM((4, W), F32)])


def _me():
    x = lax.axis_index("x")
    y = lax.axis_index("y")
    c = lax.axis_index("c")
    return x, y, c


def _peer(r):
    x, y, c = _me()
    px = jnp.bitwise_xor(x, (r >> 2) & 1)
    py = jnp.bitwise_xor(y, (r >> 1) & 1)
    pc = jnp.bitwise_xor(c, r & 1)
    return (px, py, pc), 4 * px + 2 * py + pc


class Comm:
    def __init__(self, arrays, modes):
        self.arrays = list(arrays)
        self.modes = list(modes)
        self.n = len(self.arrays)

    def out_shapes(self):
        return [_sds(((NDEV,) + a.shape) if md == "gather" else a.shape, a.dtype)
                for a, md in zip(self.arrays, self.modes)]

    def scratch(self):
        return [pltpu.SemaphoreType.DMA((self.n * (NDEV - 1),)),
                pltpu.SemaphoreType.DMA((self.n * (NDEV - 1),)),
                pltpu.SemaphoreType.DMA((self.n,))]

    def copies(self, ins, outs, send_sems, recv_sems, loc_sems):
        x, y, c = _me()
        me = 4 * x + 2 * y + c
        cps = []
        for ai, md in enumerate(self.modes):
            src = ins[ai] if md == "gather" else ins[ai].at[me]
            cps.append(pltpu.make_async_copy(src, outs[ai].at[me], loc_sems.at[ai]))
        for r in range(1, NDEV):
            dev, idx = _peer(r)
            for ai, md in enumerate(self.modes):
                src = ins[ai] if md == "gather" else ins[ai].at[idx]
                k = ai * (NDEV - 1) + r - 1
                cps.append(pltpu.make_async_remote_copy(
                    src_ref=src, dst_ref=outs[ai].at[me], send_sem=send_sems.at[k], recv_sem=recv_sems.at[k],
                    device_id=dev, device_id_type=pl.DeviceIdType.MESH))
        return cps


ANY_SPEC = pl.BlockSpec(memory_space=pl.ANY)


def comm_call(name, comm):
    na = comm.n

    def body(*refs):
        cps = comm.copies(refs[:na], refs[na:2 * na], *refs[2 * na:])
        for cp in cps:
            cp.start()
        for cp in cps:
            cp.wait()

    return pl.pallas_call(
        body, name=name, in_specs=[ANY_SPEC] * na, out_specs=[ANY_SPEC] * na, out_shape=comm.out_shapes(),
        scratch_shapes=comm.scratch(), compiler_params=pltpu.CompilerParams(has_side_effects=True),
    )(*comm.arrays)


def adamw(name, parts, w, m, v, tr):
    L = len(parts)
    _, R, C = parts[0].shape
    tr = _tile(R, tr)

    def body(*refs):
        p_refs = refs[:L]
        w_ref, m_ref, v_ref, g_ref, d_ref, nm_ref, nv_ref = refs[L:]
        for l in range(L):
            g = p_refs[l][0].astype(F32)
            for k in range(1, NDEV):
                g = g + p_refs[l][k].astype(F32)
            g_ref[l] = g
            mn = ADAM_B1 * m_ref[l] + (1.0 - ADAM_B1) * g
            vn = ADAM_B2 * v_ref[l] + (1.0 - ADAM_B2) * (g * g)
            m_hat = mn / (1.0 - ADAM_B1 ** ADAM_STEP)
            v_hat = vn / (1.0 - ADAM_B2 ** ADAM_STEP)
            d_ref[l] = -ADAM_LR * (m_hat / (jnp.sqrt(v_hat) + ADAM_EPS) + ADAM_WD * w_ref[l])
            nm_ref[l] = mn
            nv_ref[l] = vn

    blk = pl.BlockSpec((L, tr, C), lambda i: (0, i, 0))
    pblk = pl.BlockSpec((NDEV, tr, C), lambda i: (0, i, 0))
    return pl.pallas_call(
        body, name=name, grid=(R // tr,), in_specs=[pblk] * L + [blk, blk, blk],
        out_specs=(blk,) * 4, out_shape=(_sds((L, R, C), F32),) * 4,
        compiler_params=_params(("parallel",)),
    )(*parts, w, m, v)


def _rot(w):
    h = w.shape[-1] // 2
    return jnp.concatenate([-w[..., h:], w[..., :h]], axis=-1)


def _unrot(dw):
    h = dw.shape[-1] // 2
    return jnp.concatenate([dw[..., h:], -dw[..., :h]], axis=-1)


def _cols(g):
    n, R, C = g.shape
    return g.transpose(1, 0, 2).reshape(R, n * C)


def _rows(g):
    n, R, C = g.shape
    return g.reshape(n * R, C)


def _split_cols(dw):
    R, NC = dw.shape
    return dw.reshape(R, NDEV, NC // NDEV).transpose(1, 0, 2)


REPL = ["attn_norm", "q_a_norm", "kv_a_norm", "conv_b", "w_rg", "b_rg", "w_ig", "b_ig", "lru_lambda",
        "ple_norm", "final_norm"]
SHARDED = ["w_in", "w_q_b", "w_kv_b", "conv_w", "w_o_mla", "w_o_lru", "w_out", "w_ple_gate", "w_ple"]
WEIGHTS = ["attn_norm", "w_in", "q_a_norm", "w_q_b", "kv_a_norm", "w_kv_b", "conv_w", "conv_b", "w_rg", "b_rg",
           "w_ig", "b_ig", "lru_lambda", "w_o_mla", "w_o_lru", "w_out", "ple_norm", "w_ple_gate", "w_ple",
           "final_norm"]


REPL_LAYER = REPL[:-1]


def _pack(vals):
    flat = jnp.concatenate([v.reshape(-1) for v in vals])
    n = flat.shape[0]
    rows = -(-n // (128 * 256)) * 256
    return jnp.pad(flat, (0, rows * 128 - n)).reshape(rows, 128)


def _unpack(packed, shapes):
    flat = packed.reshape(-1)
    out = []
    off = 0
    for s in shapes:
        n = int(np.prod(s))
        out.append(flat[off:off + n].reshape(s))
        off += n
    return out


def kernel(x, p, positions, attn_norm, w_in, q_a_norm, w_q_b, kv_a_norm, w_kv_b, conv_w, conv_b, w_rg, b_rg, w_ig, b_ig, lru_lambda, w_o_mla, w_o_lru, w_out, ple_norm, w_ple_gate, w_ple, final_norm, loss_target, m_attn_norm, m_w_in, m_q_a_norm, m_w_q_b, m_kv_a_norm, m_w_kv_b, m_conv_w, m_conv_b, m_w_rg, m_b_rg, m_w_ig, m_b_ig, m_lru_lambda, m_w_o_mla, m_w_o_lru, m_w_out, m_ple_norm, m_w_ple_gate, m_w_ple, m_final_norm, v_attn_norm, v_w_in, v_q_a_norm, v_w_q_b, v_kv_a_norm, v_w_kv_b, v_conv_w, v_conv_b, v_w_rg, v_b_rg, v_w_ig, v_b_ig, v_lru_lambda, v_w_o_mla, v_w_o_lru, v_w_out, v_ple_norm, v_w_ple_gate, v_w_ple, v_final_norm):
    W = dict(attn_norm=attn_norm, w_in=w_in, q_a_norm=q_a_norm, w_q_b=w_q_b, kv_a_norm=kv_a_norm, w_kv_b=w_kv_b,
             conv_w=conv_w, conv_b=conv_b, w_rg=w_rg, b_rg=b_rg, w_ig=w_ig, b_ig=b_ig, lru_lambda=lru_lambda,
             w_o_mla=w_o_mla, w_o_lru=w_o_lru, w_out=w_out, ple_norm=ple_norm, w_ple_gate=w_ple_gate, w_ple=w_ple,
             final_norm=final_norm)
    M = dict(attn_norm=m_attn_norm, w_in=m_w_in, q_a_norm=m_q_a_norm, w_q_b=m_w_q_b, kv_a_norm=m_kv_a_norm,
             w_kv_b=m_w_kv_b, conv_w=m_conv_w, conv_b=m_conv_b, w_rg=m_w_rg, b_rg=m_b_rg, w_ig=m_w_ig, b_ig=m_b_ig,
             lru_lambda=m_lru_lambda, w_o_mla=m_w_o_mla, w_o_lru=m_w_o_lru, w_out=m_w_out, ple_norm=m_ple_norm,
             w_ple_gate=m_w_ple_gate, w_ple=m_w_ple, final_norm=m_final_norm)
    V = dict(attn_norm=v_attn_norm, w_in=v_w_in, q_a_norm=v_q_a_norm, w_q_b=v_w_q_b, kv_a_norm=v_kv_a_norm,
             w_kv_b=v_w_kv_b, conv_w=v_conv_w, conv_b=v_conv_b, w_rg=v_w_rg, b_rg=v_b_rg, w_ig=v_w_ig, b_ig=v_b_ig,
             lru_lambda=v_lru_lambda, w_o_mla=v_w_o_mla, w_o_lru=v_w_o_lru, w_out=v_w_out, ple_norm=v_ple_norm,
             w_ple_gate=v_w_ple_gate, w_ple=v_w_ple, final_norm=v_final_norm)

    L = w_in.shape[0]
    S = x.shape[1]
    xs = x[0]
    tgt = loss_target[0]
    tr = _tile(S, max(8, min(256, S // 2)))
    tq = _tile(S, max(128, min(512, S // 2)))

    inv_freq = ROPE_THETA ** (-jnp.arange(0, ROPE, 2, dtype=F32) / ROPE)
    ang = positions[0].astype(F32)[:, None] * inv_freq
    cs = jnp.concatenate([jnp.cos(ang), jnp.cos(ang), jnp.sin(ang), jnp.sin(ang)], axis=1)

    GATHERED = ["w_in", "w_q_b", "w_kv_b", "w_o_mla", "w_o_lru", "w_out", "w_ple_gate", "w_ple"]

    def gather_set(l):
        return Comm([W[k][l].astype(BF) for k in GATHERED] + [conv_w[l]], ["gather"] * (len(GATHERED) + 1))

    def prep(g):
        g_in, g_qb, g_kvb, g_om, g_ol, g_out, g_pg, g_ple, g_cw = g
        win = _cols(g_in)
        kr_w = win[:, 1024:1088]
        wq = _cols(g_qb).reshape(QL, NH, 192)
        return dict(
            win=jnp.concatenate([win[:, 1088:], win[:, 0:1024], kr_w, _rot(kr_w),
                                 jnp.zeros((D, ZW - Z_KR - 128), BF)], axis=-1),
            wq=jnp.concatenate([wq, _rot(wq[..., 128:])], axis=-1).reshape(QL, NH * 256),
            wkv=_cols(g_kvb), wom=_rows(g_om), wol=_rows(g_ol), wout=_rows(g_out), wpg=_rows(g_pg),
            wple=_cols(g_ple), cw=_cols(g_cw))

    wts = [None] * L
    wts[0] = prep(comm_call("gather_w0", gather_set(0)))
    wrg_b = w_rg.astype(BF)
    wig_b = w_ig.astype(BF)

    def vec(a, l):
        return a[l][None, :]

    saved = []
    xcur = xs
    for l in range(L):
        wl = wts[l]
        h = norm_fwd("norm_in", xcur, vec(attn_norm, l), tr)
        z = mm("mm_in", h, wl["win"], "nn", tn=1280, tk=2048)
        qn, kvn, kr = qkv_prep(z, vec(q_a_norm, l), vec(kv_a_norm, l), cs, tr)
        q_ext = mm("mm_q", qn, wl["wq"], "nn", o_heads=True, tk=512)
        kv_ext = mm("mm_kv", kvn, wl["wkv"], "nn", o_heads=True, tk=512, out_dtype=BF)
        (o, lse, qf), got = attn_fwd(q_ext, kv_ext, kr, cs, tq, gather_set(l + 1) if l + 1 < L else None)
        if l + 1 < L:
            wts[l + 1] = prep(got)
        xc, a, hs = lru_fwd(z, wl["cw"], vec(conv_b, l), wrg_b[l], vec(b_rg, l), wig_b[l], vec(b_ig, l),
                            vec(lru_lambda, l), tr)
        om, hl = gate_out(o, z, hs, tr)
        ym = mm("mm_om", om, wl["wom"], "nn", tk=2048)
        yl = mm("mm_ol", hl, wl["wol"], "nn", tk=2048)
        mg = merge_fwd(ym, yl, z, tr)
        x1 = mm("mm_out", mg, wl["wout"], "nn", tk=2048, res=xcur)
        hp = norm_fwd("norm_ple", x1, vec(ple_norm, l), tr)
        gp = mm("mm_pg", hp, wl["wpg"], "nn", tk=2048)
        pe = mm("mm_ple", p[l, 0], wl["wple"], "nn")
        x2 = ple_fin(x1, pe, gp, tr)
        saved.append(dict(x=xcur, h=h, z=z, qn=qn, kvn=kvn, kr=kr, kv_ext=kv_ext, o=o, lse=lse, qf=qf, xc=xc, a=a,
                          hs=hs, om=om, hl=hl, ym=ym, yl=yl, mg=mg, x1=x1, hp=hp, gp=gp, pe=pe))
        xcur = x2

    lsum, dx, d_final = final_loss(xcur, final_norm[None, :], tgt, tr)
    loss = lax.psum(0.5 * lsum[0, 0] / D, MESH_AXES)

    gr = {k: [None] * L for k in WEIGHTS if k != "final_norm"}
    landed = [None] * L
    zero_fn = jnp.zeros_like(final_norm)

    def grad_set(l):
        rep = _pack([gr[k][l] for k in REPL_LAYER] + [d_final[0] if l == L - 1 else zero_fn])
        return Comm(
            [_split_cols(gr["w_in"][l]), _split_cols(gr["w_q_b"][l]), _split_cols(gr["w_kv_b"][l]),
             _split_cols(gr["conv_w"][l]), gr["w_o_mla"][l].reshape(NDEV, D // NDEV, D),
             gr["w_o_lru"][l].reshape(NDEV, D // NDEV, D), gr["w_out"][l].reshape(NDEV, D // NDEV, D),
             gr["w_ple_gate"][l].reshape(NDEV, D // NDEV, D), _split_cols(gr["w_ple"][l]), rep],
            ["a2a"] * len(SHARDED) + ["gather"])

    for l in reversed(range(L)):
        sv = saved[l]
        wl = wts[l]
        z = sv["z"]
        dpe, dgp = ple_bwd(dx, sv["pe"], sv["gp"], tr)
        gr["w_ple"][l] = mm("mm_dple", p[l, 0], dpe, "tn", tm=256, out_dtype=BF)
        gr["w_ple_gate"][l] = mm("mm_dpg", sv["hp"], dgp, "tn", out_dtype=BF)
        dhp = mm("mm_dhp", dgp, wl["wpg"], "nt", tk=2048)
        dx1, gr["ple_norm"][l] = norm_bwd("norm_ple_bwd", sv["x1"], dhp, vec(ple_norm, l), dx, tr)
        dmg = mm("mm_dmg", dx1, wl["wout"], "nt", tk=2048)
        gr["w_out"][l] = mm("mm_dwout", sv["mg"], dx1, "tn", out_dtype=BF)
        dym, dyl, dz_m = merge_bwd(dmg, sv["ym"], sv["yl"], z, tr)
        dom = mm("mm_dom", dym, wl["wom"], "nt", tk=2048)
        gr["w_o_mla"][l] = mm("mm_dwom", sv["om"], dym, "tn", out_dtype=BF)
        dhl = mm("mm_dhl", dyl, wl["wol"], "nt", tk=2048)
        gr["w_o_lru"][l] = mm("mm_dwol", sv["hl"], dyl, "tn", out_dtype=BF)
        do, dz_gm, dhs, dz_gl = gate_bwd(dom, sv["o"], dhl, sv["hs"], z, tr)
        (dq_ext, dkv_ext, dkr), got = attn_bwd(sv["qf"], sv["kv_ext"], sv["kr"], sv["o"], do, sv["lse"], cs, tq,
                                               grad_set(l + 1) if l + 1 < L else None)
        if l + 1 < L:
            landed[l + 1] = got
        dqn = mm("mm_dqn", dq_ext, wl["wq"], "nt", a_heads=True, tn=512)
        dwq_ext = mm("mm_dwq", sv["qn"], dq_ext, "tn", b_heads=True, tm=512, out_dtype=BF)
        dkn = mm("mm_dkn", dkv_ext, wl["wkv"], "nt", a_heads=True, tn=512)
        gr["w_kv_b"][l] = mm("mm_dwkv", sv["kvn"], dkv_ext, "tn", b_heads=True, tm=512, out_dtype=BF)
        dz_t, gr["q_a_norm"][l], gr["kv_a_norm"][l] = qkv_prep_bwd(
            z, dqn, dkn, dkr, vec(q_a_norm, l), vec(kv_a_norm, l), cs, tr)
        dwq4 = dwq_ext.reshape(QL, NH, 256)
        gr["w_q_b"][l] = jnp.concatenate(
            [dwq4[..., 0:128], dwq4[..., 128:192] + _unrot(dwq4[..., 192:256])], axis=-1).reshape(QL, NH * 192)
        (dxc, gr["w_rg"][l], gr["w_ig"][l], gr["b_rg"][l], gr["b_ig"][l], gr["lru_lambda"][l],
         gr["conv_b"][l]) = lru_bwd(dhs, sv["a"], sv["hs"], sv["xc"], wrg_b[l], vec(b_rg, l), wig_b[l],
                                    vec(b_ig, l), vec(lru_lambda, l), tr)
        dz_u, gr["conv_w"][l] = conv_bwd(dxc, z, wl["cw"], tr)
        dz = jnp.concatenate([dz_gm, dz_u, dz_gl, dz_m, dz_t], axis=1)
        dwin_ext = mm("mm_dwin", sv["h"], dz, "tn", tn=1280, out_dtype=BF)
        dh = mm("mm_dh", dz, wl["win"], "nt", tk=1280)
        dx, gr["attn_norm"][l] = norm_bwd("norm_in_bwd", sv["x"], dh, vec(attn_norm, l), dx1, tr)
        gr["w_in"][l] = jnp.concatenate(
            [dwin_ext[:, Z_Q:Z_KR], dwin_ext[:, Z_KR:Z_KR + 64] + _unrot(dwin_ext[:, Z_KR + 64:Z_KR + 128]),
             dwin_ext[:, 0:Z_Q]], axis=1)
    grad_x = dx[None]
    landed[0] = comm_call("exchange_grads0", grad_set(0))

    res = {}
    for ki, k in enumerate(SHARDED):
        res[k] = adamw("adamw_" + k, [landed[l][ki] for l in range(L)], W[k], M[k], V[k], 64)

    def rep_pack(T):
        return jnp.stack([_pack([T[k][l] for k in REPL_LAYER] + [T["final_norm"] if l == L - 1 else zero_fn])
                          for l in range(L)])

    rep = adamw("adamw_rep", [landed[l][len(SHARDED)] for l in range(L)], rep_pack(W), rep_pack(M), rep_pack(V), 256)
    shapes = [W[k].shape[1:] for k in REPL_LAYER] + [final_norm.shape]
    per_layer = [[_unpack(t[l], shapes) for l in range(L)] for t in rep]
    for i, k in enumerate(REPL_LAYER):
        res[k] = tuple(jnp.stack([per_layer[t][l][i] for l in range(L)]) for t in range(4))
    res["final_norm"] = tuple(per_layer[t][L - 1][-1] for t in range(4))

    outs = [loss, grad_x]
    for t in range(4):
        outs += [res[k][t] for k in WEIGHTS]
    return tuple(outs)
```

```python
import functools
import math

import numpy as np
import jax
import jax.numpy as jnp
from jax import lax
from jax.experimental import pallas as pl
from jax.experimental.pallas import tpu as pltpu

F32 = jnp.float32
BF = jnp.bfloat16
MESH_AXES = ("x", "y", "c")
NDEV = 8

D = 2048
NH = 16
QL = 512
KVL = 512
ROPE = 64
PLE = 256
NBLK = 16
BD = 128
CHUNK_SHIFT = 6
EPS = 1e-6
LRU_C = 8.0
ROPE_THETA = 10000.0
IN_TOTAL = 11328
ZW = 11520
Z_Q = 10240
Z_KV = 10752
Z_KR = 11264
SCALE = 1.0 / math.sqrt(128 + 64)
EXP2_SCALE = SCALE * math.log2(math.e)
LOG2E = math.log2(math.e)

ADAM_LR = 0.001
ADAM_B1 = 0.9
ADAM_B2 = 0.999
ADAM_EPS = 1e-08
ADAM_WD = 0.01
ADAM_STEP = 10

VMEM_LIMIT = 60 * 1024 * 1024

NN = (((1,), (0,)), ((), ()))
NT = (((1,), (1,)), ((), ()))


def _tile(n, pref):
    t = min(n, pref)
    while n % t:
        t //= 2
    return t


def _params(sem):
    return pltpu.CompilerParams(dimension_semantics=sem, vmem_limit_bytes=VMEM_LIMIT)


def _dot(a, b, dims=NN):
    return lax.dot_general(a, b, dims, preferred_element_type=F32)


def _dot_tn(a, b):
    return lax.dot_general(a.T, b, NN, preferred_element_type=F32)


def _bf(v):
    return v if v.dtype == BF else v.astype(BF)


def mm(name, a, b, mode, *, out_dtype=F32, tm=1024, tn=1024, tk=1024, res=None,
       a_heads=False, b_heads=False, o_heads=False, comm=None):
    if mode == "nn":
        M = a.shape[0]
        K = a.shape[1]
        N = b.shape[1]
    elif mode == "nt":
        M = a.shape[1] if a_heads else a.shape[0]
        K = a.shape[0] * a.shape[2] if a_heads else a.shape[1]
        N = b.shape[0]
    else:
        K = a.shape[0]
        M = a.shape[1]
        N = b.shape[0] * b.shape[2] if b_heads else b.shape[1]
    tm = _tile(M, tm)
    tn = _tile(N, tn)
    tk = _tile(K, tk)
    nk = K // tk
    grid = (M // tm, N // tn, nk)
    nc = comm.n if comm is not None else 0

    if mode == "nn":
        a_spec = pl.BlockSpec((tm, tk), lambda i, j, k: (i, k))
        b_spec = pl.BlockSpec((tk, tn), lambda i, j, k: (k, j))
    elif mode == "nt":
        if a_heads:
            a_spec = pl.BlockSpec((tk // 256, tm, 256), lambda i, j, k: (k, i, 0))
        else:
            a_spec = pl.BlockSpec((tm, tk), lambda i, j, k: (i, k))
        b_spec = pl.BlockSpec((tn, tk), lambda i, j, k: (j, k))
    else:
        a_spec = pl.BlockSpec((tk, tm), lambda i, j, k: (k, i))
        if b_heads:
            b_spec = pl.BlockSpec((tn // 256, tk, 256), lambda i, j, k: (j, k, 0))
        else:
            b_spec = pl.BlockSpec((tk, tn), lambda i, j, k: (k, j))
    if o_heads:
        o_spec = pl.BlockSpec((tn // 256, tm, 256), lambda i, j, k: (j, i, 0))
        o_shape = jax.ShapeDtypeStruct((N // 256, M, 256), out_dtype)
    else:
        o_spec = pl.BlockSpec((tm, tn), lambda i, j, k: (i, j))
        o_shape = jax.ShapeDtypeStruct((M, N), out_dtype)
    in_specs = [a_spec, b_spec]
    args = [a, b]
    if res is not None:
        in_specs.append(pl.BlockSpec((tm, tn), lambda i, j, k: (i, j)))
        args.append(res)
    n_in = len(args)

    def heads(ref):
        return jnp.concatenate([ref[e] for e in range(ref.shape[0])], axis=1)

    def body(*refs):
        ins, (o_ref,), scr, cps = _split_refs(refs, n_in, 1, comm)
        a_ref, b_ref = ins[0], ins[1]
        r_ref = ins[2] if res is not None else None
        ids = [pl.program_id(d) for d in range(3)]
        _side_start(cps, functools.reduce(jnp.logical_and, [ids[d] == 0 for d in range(3)]))
        av = _bf(heads(a_ref) if a_heads else a_ref[...])
        bv = _bf(heads(b_ref) if b_heads else b_ref[...])
        if mode == "nn":
            p = _dot(av, bv, NN)
        elif mode == "nt":
            p = _dot(av, bv, NT)
        else:
            p = _dot_tn(av, bv)

        def fin(acc):
            if r_ref is not None:
                acc = acc + r_ref[...]
            acc = acc.astype(o_ref.dtype)
            if o_heads:
                for e in range(tn // 256):
                    o_ref[e] = acc[:, e * 256:(e + 1) * 256]
            else:
                o_ref[...] = acc

        if nk == 1:
            fin(p)
        else:
            acc_ref = scr[0]
            k = ids[2]

            @pl.when(k == 0)
            def _():
                acc_ref[...] = p

            @pl.when(k > 0)
            def _():
                acc_ref[...] += p

            @pl.when(k == nk - 1)
            def _():
                fin(acc_ref[...])

        _side_wait(cps, functools.reduce(jnp.logical_and, [ids[d] == grid[d] - 1 for d in range(3)]))

    scratch = ([] if nk == 1 else [pltpu.VMEM((tm, tn), F32)]) + (comm.scratch() if comm is not None else [])
    sem = ("arbitrary",) * 3 if comm is not None else ("parallel", "parallel", "arbitrary")
    out = pl.pallas_call(
        body, name=name, grid=grid, in_specs=in_specs + [ANY_SPEC] * nc, out_specs=[o_spec] + [ANY_SPEC] * nc,
        out_shape=[o_shape] + (comm.out_shapes() if comm is not None else []),
        scratch_shapes=scratch, compiler_params=_params(sem),
    )(*args, *(comm.arrays if comm is not None else []))
    return out[0] if comm is None else (out[0], out[1:])


def _rs(tr, w, cb=0):
    return pl.BlockSpec((tr, w), lambda i: (i, cb))


def _fs(shape):
    nd = len(shape)
    return pl.BlockSpec(shape, lambda i: (0,) * nd)


def _rowcall(name, body, S, tr, ins, in_specs, outs, out_specs, scratch=()):
    return pl.pallas_call(
        body, name=name, grid=(S // tr,), in_specs=in_specs, out_specs=out_specs, out_shape=outs,
        scratch_shapes=list(scratch), compiler_params=_params(("arbitrary",)),
    )(*ins)


def _sds(shape, dt):
    return jax.ShapeDtypeStruct(shape, dt)


def _fold8(v):
    tr, w = v.shape
    return jnp.sum(v.reshape(tr // 8, 8, w), axis=0)


def _acc_rows(i, n, acc_ref, out_ref, part):
    @pl.when(i == 0)
    def _():
        acc_ref[...] = part

    @pl.when(i > 0)
    def _():
        acc_ref[...] += part

    @pl.when(i == n - 1)
    def _():
        out_ref[...] = jnp.sum(acc_ref[...], axis=0, keepdims=True)


def _rms_fwd(x, g):
    r = lax.rsqrt(jnp.mean(x * x, axis=-1, keepdims=True) + EPS)
    return x * r * g


def _rms_bwd(x, dy, g):
    r = lax.rsqrt(jnp.mean(x * x, axis=-1, keepdims=True) + EPS)
    xh = x * r
    dxh = dy * g
    dx = r * (dxh - xh * jnp.mean(dxh * xh, axis=-1, keepdims=True))
    return dx, dy * xh


def _sig(x):
    return jax.nn.sigmoid(x)


def norm_fwd(name, x, g, tr):
    S, W = x.shape

    def body(x_ref, g_ref, o_ref):
        o_ref[...] = _rms_fwd(x_ref[...], g_ref[...]).astype(BF)

    return _rowcall(name, body, S, tr, [x, g], [_rs(tr, W), _fs((1, W))], _sds((S, W), BF), _rs(tr, W))


def norm_bwd(name, x, dy, g, dres, tr):
    S, W = x.shape
    n = S // tr

    def body(x_ref, dy_ref, g_ref, dr_ref, dx_ref, dg_ref, acc_ref):
        i = pl.program_id(0)
        dx, dgp = _rms_bwd(x_ref[...], dy_ref[...], g_ref[...])
        dx_ref[...] = dr_ref[...] + dx
        _acc_rows(i, n, acc_ref, dg_ref, _fold8(dgp))

    return _rowcall(name, body, S, tr, [x, dy, g, dres],
                    [_rs(tr, W), _rs(tr, W), _fs((1, W)), _rs(tr, W)],
                    (_sds((S, W), F32), _sds((1, W), F32)), (_rs(tr, W), _fs((1, W))),
                    scratch=[pltpu.VMEM((8, W), F32)])


def _lane_lt(shape, n):
    return lax.broadcasted_iota(jnp.int32, shape, 1) < n


def qkv_prep(z, qg, kvg, cs, tr):
    S = z.shape[0]

    def body(ql_ref, kl_ref, zk_ref, qg_ref, kg_ref, cs_ref, qn_ref, kn_ref, kr_ref):
        qn_ref[...] = _rms_fwd(ql_ref[...], qg_ref[...]).astype(BF)
        kn_ref[...] = _rms_fwd(kl_ref[...], kg_ref[...]).astype(BF)
        t = zk_ref[...] * cs_ref[...]
        t = t + pltpu.roll(t, 64, 1)
        kr_ref[...] = jnp.where(_lane_lt(t.shape, 64), t, 0.0).astype(BF)

    return _rowcall(
        "qkv_prep", body, S, tr, [z, z, z, qg, kvg, cs],
        [_rs(tr, QL, Z_Q // QL), _rs(tr, KVL, Z_KV // KVL), _rs(tr, 128, Z_KR // 128),
         _fs((1, QL)), _fs((1, KVL)), _rs(tr, 128)],
        (_sds((S, QL), BF), _sds((S, KVL), BF), _sds((S, 128), BF)),
        (_rs(tr, QL), _rs(tr, KVL), _rs(tr, 128)))


def qkv_prep_bwd(z, dqn, dkn, dkr, qg, kvg, cs, tr):
    S = z.shape[0]
    n = S // tr

    def body(ql_ref, kl_ref, dq_ref, dk_ref, dkr_ref, qg_ref, kg_ref, cs_ref,
             dz_ref, dqg_ref, dkg_ref, accq, acck):
        i = pl.program_id(0)
        dql, gq = _rms_bwd(ql_ref[...], dq_ref[...], qg_ref[...])
        dkl, gk = _rms_bwd(kl_ref[...], dk_ref[...], kg_ref[...])
        t = jnp.where(_lane_lt((tr, 128), 64), dkr_ref[...], 0.0)
        t = (t + pltpu.roll(t, 64, 1)) * cs_ref[...]
        dz_ref[...] = jnp.concatenate(
            [dql.astype(BF), dkl.astype(BF), t.astype(BF), jnp.zeros((tr, 128), BF)], axis=1)
        _acc_rows(i, n, accq, dqg_ref, _fold8(gq))
        _acc_rows(i, n, acck, dkg_ref, _fold8(gk))

    return _rowcall(
        "qkv_prep_bwd", body, S, tr, [z, z, dqn, dkn, dkr, qg, kvg, cs],
        [_rs(tr, QL, Z_Q // QL), _rs(tr, KVL, Z_KV // KVL), _rs(tr, QL), _rs(tr, KVL), _rs(tr, 128),
         _fs((1, QL)), _fs((1, KVL)), _rs(tr, 128)],
        (_sds((S, ZW - Z_Q), BF), _sds((1, QL), F32), _sds((1, KVL), F32)),
        (_rs(tr, ZW - Z_Q), _fs((1, QL)), _fs((1, KVL))),
        scratch=[pltpu.VMEM((8, QL), F32), pltpu.VMEM((8, KVL), F32)])


def gate_out(o, z, hs, tr):
    S = o.shape[0]

    def body(o_ref, gm_ref, h_ref, gl_ref, om_ref, hl_ref):
        gm = gm_ref[...]
        gl = gl_ref[...]
        om_ref[...] = (o_ref[...] * (gm * _sig(gm))).astype(BF)
        hl_ref[...] = (h_ref[...] * (gl * _sig(gl))).astype(BF)

    return _rowcall("gate_out", body, S, tr, [o, z, hs, z],
                    [_rs(tr, D), _rs(tr, D, 0), _rs(tr, D), _rs(tr, D, 2)],
                    (_sds((S, D), BF), _sds((S, D), BF)), (_rs(tr, D), _rs(tr, D)))


def gate_bwd(dom, o, dhl, hs, z, tr):
    S = o.shape[0]

    def body(dom_ref, o_ref, dhl_ref, h_ref, gm_ref, gl_ref, do_ref, dgm_ref, dh_ref, dgl_ref):
        gm = gm_ref[...]
        sm = _sig(gm)
        dom_v = dom_ref[...]
        do_ref[...] = dom_v * (gm * sm)
        dgm_ref[...] = (dom_v * o_ref[...] * (sm * (1.0 + gm * (1.0 - sm)))).astype(BF)
        gl = gl_ref[...]
        sl = _sig(gl)
        dhl_v = dhl_ref[...]
        dh_ref[...] = dhl_v * (gl * sl)
        dgl_ref[...] = (dhl_v * h_ref[...] * (sl * (1.0 + gl * (1.0 - sl)))).astype(BF)

    return _rowcall("gate_bwd", body, S, tr, [dom, o, dhl, hs, z, z],
                    [_rs(tr, D), _rs(tr, D), _rs(tr, D), _rs(tr, D), _rs(tr, D, 0), _rs(tr, D, 2)],
                    (_sds((S, D), F32), _sds((S, D), BF), _sds((S, D), F32), _sds((S, D), BF)),
                    (_rs(tr, D),) * 4)


def merge_fwd(ym, yl, z, tr):
    S = ym.shape[0]

    def body(ym_ref, yl_ref, mm_ref, ml_ref, o_ref):
        o_ref[...] = (_sig(mm_ref[...]) * ym_ref[...] + _sig(ml_ref[...]) * yl_ref[...]).astype(BF)

    return _rowcall("merge_fwd", body, S, tr, [ym, yl, z, z],
                    [_rs(tr, D), _rs(tr, D), _rs(tr, D, 3), _rs(tr, D, 4)],
                    _sds((S, D), BF), _rs(tr, D))


def merge_bwd(dmg, ym, yl, z, tr):
    S = ym.shape[0]

    def body(d_ref, ym_ref, yl_ref, mm_ref, ml_ref, dym_ref, dyl_ref, dz_ref):
        d = d_ref[...]
        sm = _sig(mm_ref[...])
        sl = _sig(ml_ref[...])
        dym_ref[...] = (d * sm).astype(BF)
        dyl_ref[...] = (d * sl).astype(BF)
        dz_ref[:, 0:D] = (d * ym_ref[...] * (sm * (1.0 - sm))).astype(BF)
        dz_ref[:, D:2 * D] = (d * yl_ref[...] * (sl * (1.0 - sl))).astype(BF)

    return _rowcall("merge_bwd", body, S, tr, [dmg, ym, yl, z, z],
                    [_rs(tr, D), _rs(tr, D), _rs(tr, D), _rs(tr, D, 3), _rs(tr, D, 4)],
                    (_sds((S, D), BF), _sds((S, D), BF), _sds((S, 2 * D), BF)),
                    (_rs(tr, D), _rs(tr, D), _rs(tr, 2 * D)))


def ple_fin(x1, pe, gp, tr):
    S = x1.shape[0]

    def body(x_ref, pe_ref, gp_ref, o_ref):
        o_ref[...] = x_ref[...] + pe_ref[...] * _sig(gp_ref[...])

    return _rowcall("ple_fin", body, S, tr, [x1, pe, gp], [_rs(tr, D)] * 3, _sds((S, D), F32), _rs(tr, D))


def ple_bwd(dx2, pe, gp, tr):
    S = dx2.shape[0]

    def body(d_ref, pe_ref, gp_ref, dpe_ref, dgp_ref):
        d = d_ref[...]
        sg = _sig(gp_ref[...])
        dpe_ref[...] = (d * sg).astype(BF)
        dgp_ref[...] = (d * pe_ref[...] * (sg * (1.0 - sg))).astype(BF)

    return _rowcall("ple_bwd", body, S, tr, [dx2, pe, gp], [_rs(tr, D)] * 3,
                    (_sds((S, D), BF), _sds((S, D), BF)), (_rs(tr, D), _rs(tr, D)))


def final_loss(x, g, tgt, tr):
    S, W = x.shape
    n = S // tr

    def body(x_ref, g_ref, t_ref, ls_ref, dx_ref, dg_ref, accl, accg):
        i = pl.program_id(0)
        xv = x_ref[...]
        gv = g_ref[...]
        e = _rms_fwd(xv, gv) - t_ref[...]
        e2 = _fold8(e * e)
        l8 = e2[:, 0:128]
        for k in range(1, W // 128):
            l8 = l8 + e2[:, k * 128:(k + 1) * 128]
        dx, dgp = _rms_bwd(xv, e * (1.0 / W), gv)
        dx_ref[...] = dx
        _acc_rows(i, n, accg, dg_ref, _fold8(dgp))

        @pl.when(i == 0)
        def _():
            accl[...] = l8

        @pl.when(i > 0)
        def _():
            accl[...] += l8

        @pl.when(i == n - 1)
        def _():
            tot = jnp.sum(jnp.sum(accl[...], axis=0, keepdims=True), axis=1, keepdims=True)
            ls_ref[...] = jnp.broadcast_to(tot, (1, 128))

    return _rowcall("final_loss", body, S, tr, [x, g, tgt], [_rs(tr, W), _fs((1, W)), _rs(tr, W)],
                    (_sds((1, 128), F32), _sds((S, W), F32), _sds((1, W), F32)),
                    (_fs((1, 128)), _rs(tr, W), _fs((1, W))),
                    scratch=[pltpu.VMEM((8, 128), F32), pltpu.VMEM((8, W), F32)])


def _chunk_mask(qi, kj, tq, tk):
    rows = qi * tq + lax.broadcasted_iota(jnp.int32, (tq, tk), 0)
    cols = kj * tk + lax.broadcasted_iota(jnp.int32, (tq, tk), 1)
    return (cols >> CHUNK_SHIFT) <= (rows >> CHUNK_SHIFT)


def _load_kv(kv_ref, kr_ref, kj, tk):
    off = pl.multiple_of(kj * tk, tk)
    kv = kv_ref[pl.ds(off, tk), :]
    k = jnp.concatenate([kv[:, 0:128], kr_ref[pl.ds(off, tk), :]], axis=1)
    return k, kv[:, 128:256], off


def _split_refs(refs, n_in, n_out, comm):
    nc = comm.n if comm is not None else 0
    ins = refs[:n_in]
    c_in = refs[n_in:n_in + nc]
    outs = refs[n_in + nc:n_in + nc + n_out]
    c_out = refs[n_in + nc + n_out:n_in + 2 * nc + n_out]
    rest = refs[n_in + 2 * nc + n_out:]
    if comm is None:
        return ins, outs, rest, None
    return ins, outs, rest[:len(rest) - 3], comm.copies(c_in, c_out, *rest[len(rest) - 3:])


def _side_start(cps, first):
    if cps is None:
        return

    @pl.when(first)
    def _():
        for cp in cps:
            cp.start()


def _side_wait(cps, last):
    if cps is None:
        return

    @pl.when(last)
    def _():
        for cp in cps:
            cp.wait()


def attn_fwd(q_ext, kv_ext, kr, cs, tq, comm=None):
    H, S, _ = q_ext.shape
    nq = S // tq
    tk = tq
    nc = comm.n if comm is not None else 0
    HP = 2

    SL = min(64, tq)
    rep = tk // 128

    def body(*refs):
        ((q_ref, kv_ref, kr_ref, cs_ref), (o_ref, lse_ref, qf_ref),
         (s_scr, p_scr, m_scr, l_scr, a_scr, acc_scr), cps) = _split_refs(refs, 4, 3, comm)
        hh = pl.program_id(0)
        qi = pl.program_id(1)
        _side_start(cps, jnp.logical_and(hh == 0, qi == 0))
        for e in range(HP):
            q = q_ref[e]
            hi = q[:, 128:256] * cs_ref[...]
            hi = hi + pltpu.roll(hi, 64, 1)
            hi = jnp.where(_lane_lt(hi.shape, 64), hi, 0.0)
            qf_ref[e] = jnp.concatenate([q[:, 0:128], hi], axis=1).astype(BF)
        m_scr[...] = jnp.full(m_scr.shape, -jnp.inf, F32)
        l_scr[...] = jnp.zeros_like(l_scr)
        acc_scr[...] = jnp.zeros_like(acc_scr)

        def step(kj, masked):
            off = pl.multiple_of(kj * tk, tk)
            kr_t = kr_ref[pl.ds(off, tk), :]
            for e in range(HP):
                k = jnp.concatenate([kv_ref[e, pl.ds(off, tk), 0:128], kr_t], axis=1)
                s_scr[e] = _dot(qf_ref[e], k, NT)
            for e in range(HP):
                for r in range(tq // SL):
                    rows = pl.ds(r * SL, SL)
                    s = s_scr[e, rows, :]
                    if masked:
                        rws = qi * tq + r * SL + lax.broadcasted_iota(jnp.int32, (SL, tk), 0)
                        cls = kj * tk + lax.broadcasted_iota(jnp.int32, (SL, tk), 1)
                        s = jnp.where((cls >> CHUNK_SHIFT) <= (rws >> CHUNK_SHIFT), s, -jnp.inf)
                    m_old = m_scr[e, rows, :]
                    m_new = jnp.maximum(m_old, jnp.max(s, axis=1, keepdims=True))
                    alpha = jnp.exp2((m_old - m_new) * EXP2_SCALE)
                    p = jnp.exp2(s * EXP2_SCALE - jnp.tile(m_new * EXP2_SCALE, (1, rep)))
                    l_scr[e, rows, :] = alpha * l_scr[e, rows, :] + jnp.sum(p, axis=1, keepdims=True)
                    m_scr[e, rows, :] = m_new
                    a_scr[e, rows, :] = alpha
                    p_scr[e, rows, :] = p.astype(BF)
            for e in range(HP):
                pv = _dot(p_scr[e], kv_ref[e, pl.ds(off, tk), 128:256], NN)
                acc_scr[e] = a_scr[e] * acc_scr[e] + pv

        def trip(kj, c):
            step(kj, False)
            return c

        lax.fori_loop(0, qi, trip, 0)
        step(qi, True)
        for e in range(HP):
            l = l_scr[e]
            o_ref[:, e * 128:(e + 1) * 128] = acc_scr[e] / l
            lse_ref[e] = m_scr[e] * SCALE + jnp.log(l)
        _side_wait(cps, jnp.logical_and(hh == H // HP - 1, qi == nq - 1))

    res = pl.pallas_call(
        body, name="attn_fwd" if comm is None else "attn_fwd_comm", grid=(H // HP, nq),
        in_specs=[pl.BlockSpec((HP, tq, 256), lambda h, i: (h, i, 0)),
                  pl.BlockSpec((HP, S, 256), lambda h, i: (h, 0, 0)),
                  pl.BlockSpec((S, 128), lambda h, i: (0, 0)),
                  pl.BlockSpec((tq, 128), lambda h, i: (i, 0))] + [ANY_SPEC] * nc,
        out_specs=[pl.BlockSpec((tq, HP * 128), lambda h, i: (i, h)),
                   pl.BlockSpec((HP, tq, 128), lambda h, i: (h, i, 0)),
                   pl.BlockSpec((HP, tq, 256), lambda h, i: (h, i, 0))] + [ANY_SPEC] * nc,
        out_shape=[_sds((S, H * 128), F32), _sds((H, S, 128), F32), _sds((H, S, 256), BF)]
        + (comm.out_shapes() if comm is not None else []),
        scratch_shapes=[pltpu.VMEM((HP, tq, tk), F32), pltpu.VMEM((HP, tq, tk), BF)]
        + [pltpu.VMEM((HP, tq, 128), F32)] * 4 + (comm.scratch() if comm is not None else []),
        compiler_params=_params(("arbitrary", "arbitrary")),
    )(q_ext, kv_ext, kr, cs, *(comm.arrays if comm is not None else []))
    return res[:3], res[3:]


def attn_bwd(qf, kv_ext, kr, o, do, lse, cs, tq, comm=None):
    H, S, _ = qf.shape
    nq = S // tq
    tk = tq
    nc = comm.n if comm is not None else 0

    SL = min(64, tq)
    rep = tk // 128

    def body(*refs):
        ((q_ref, kv_ref, kr_ref, o_ref, do_ref, lse_ref, cs_ref), (dq_ref, dkv_ref, dkr_ref),
         (dk_acc, dv_acc, s_scr, dp_scr, p_scr, ds_scr, lse_scr, dl_scr, dob_scr, dq_scr), cps) = _split_refs(
             refs, 7, 3, comm)
        h = pl.program_id(0)
        qi = pl.program_id(1)
        _side_start(cps, jnp.logical_and(h == 0, qi == 0))

        @pl.when(qi == 0)
        def _():
            dk_acc[...] = jnp.zeros_like(dk_acc)
            dv_acc[...] = jnp.zeros_like(dv_acc)

        @pl.when(jnp.logical_and(h == 0, qi == 0))
        def _():
            dkr_ref[...] = jnp.zeros_like(dkr_ref)

        dov = do_ref[...]
        dl_scr[...] = jnp.broadcast_to(jnp.sum(dov * o_ref[...], axis=1, keepdims=True), (tq, 128))
        lse_scr[...] = lse_ref[...] * LOG2E
        dob_scr[...] = dov.astype(BF)
        dq_scr[...] = jnp.zeros_like(dq_scr)

        def step(kj, masked):
            k, v, off = _load_kv(kv_ref, kr_ref, kj, tk)
            s_scr[...] = _dot(q_ref[...], k, NT)
            dp_scr[...] = _dot(dob_scr[...], v, NT)
            for r in range(tq // SL):
                rows = pl.ds(r * SL, SL)
                s = s_scr[rows, :]
                if masked:
                    rws = qi * tq + r * SL + lax.broadcasted_iota(jnp.int32, (SL, tk), 0)
                    cls = kj * tk + lax.broadcasted_iota(jnp.int32, (SL, tk), 1)
                    s = jnp.where((cls >> CHUNK_SHIFT) <= (rws >> CHUNK_SHIFT), s, -jnp.inf)
                p = jnp.exp2(s * EXP2_SCALE - jnp.tile(lse_scr[rows, :], (1, rep)))
                p_scr[rows, :] = p.astype(BF)
                ds_scr[rows, :] = (p * (dp_scr[rows, :] - jnp.tile(dl_scr[rows, :], (1, rep)))).astype(BF)
            dv_acc[pl.ds(off, tk), :] += _dot_tn(p_scr[...], dob_scr[...])
            dk_acc[pl.ds(off, tk), :] += _dot_tn(ds_scr[...], q_ref[...])
            k2, _, _ = _load_kv(kv_ref, kr_ref, kj, tk)
            dq_scr[...] += _dot(ds_scr[...], k2, NN)

        def trip(kj, c):
            step(kj, False)
            return c

        lax.fori_loop(0, qi, trip, 0)
        step(qi, True)
        dq = dq_scr[...] * SCALE
        hi = dq[:, 128:256]
        hi = (hi + pltpu.roll(hi, 64, 1)) * cs_ref[...]
        dq_ref[...] = jnp.concatenate([dq[:, 0:128], hi], axis=1).astype(BF)

        @pl.when(qi == nq - 1)
        def _():
            dkv_ref[...] = jnp.concatenate([dk_acc[:, 0:128] * SCALE, dv_acc[...]], axis=1).astype(BF)
            dkr_ref[...] += dk_acc[:, 128:256] * SCALE

        _side_wait(cps, jnp.logical_and(h == H - 1, qi == nq - 1))

    res = pl.pallas_call(
        body, name="attn_bwd" if comm is None else "attn_bwd_comm", grid=(H, nq),
        in_specs=[pl.BlockSpec((None, tq, 256), lambda h, i: (h, i, 0)),
                  pl.BlockSpec((None, S, 256), lambda h, i: (h, 0, 0)),
                  pl.BlockSpec((S, 128), lambda h, i: (0, 0)),
                  pl.BlockSpec((tq, 128), lambda h, i: (i, h)),
                  pl.BlockSpec((tq, 128), lambda h, i: (i, h)),
                  pl.BlockSpec((None, tq, 128), lambda h, i: (h, i, 0)),
                  pl.BlockSpec((tq, 128), lambda h, i: (i, 0))] + [ANY_SPEC] * nc,
        out_specs=[pl.BlockSpec((None, tq, 256), lambda h, i: (h, i, 0)),
                   pl.BlockSpec((None, S, 256), lambda h, i: (h, 0, 0)),
                   pl.BlockSpec((S, 128), lambda h, i: (0, 0))] + [ANY_SPEC] * nc,
        out_shape=[_sds((H, S, 256), BF), _sds((H, S, 256), BF), _sds((S, 128), F32)]
        + (comm.out_shapes() if comm is not None else []),
        scratch_shapes=[pltpu.VMEM((S, 256), F32), pltpu.VMEM((S, 128), F32),
                        pltpu.VMEM((tq, tk), F32), pltpu.VMEM((tq, tk), F32),
                        pltpu.VMEM((tq, tk), BF), pltpu.VMEM((tq, tk), BF),
                        pltpu.VMEM((tq, 128), F32), pltpu.VMEM((tq, 128), F32),
                        pltpu.VMEM((tq, 128), BF), pltpu.VMEM((tq, 256), F32)]
        + (comm.scratch() if comm is not None else []),
        compiler_params=_params(("arbitrary", "arbitrary")),
    )(qf, kv_ext, kr, o, do, lse, cs, *(comm.arrays if comm is not None else []))
    return res[:3], res[3:]


def _log1p(y):
    w = 1.0 + y
    return jnp.where(w == 1.0, y, jnp.log(w) * (y / (w - 1.0)))


def _expm1(x):
    u = jnp.exp(x)
    return jnp.where(u == 1.0, x, (u - 1.0) * (x / jnp.log(u)))


def _softplus(x):
    return jnp.maximum(x, 0.0) + _log1p(jnp.exp(-jnp.abs(x)))


def _blockdiag(xb, w_ref):
    return jnp.concatenate(
        [_dot(xb[:, k * BD:(k + 1) * BD], w_ref[k], NN) for k in range(NBLK)], axis=1)


def _gates(xc, wr_ref, br_ref, wi_ref, bi_ref, sp):
    xb = xc.astype(BF)
    r = _sig(_blockdiag(xb, wr_ref) + br_ref[...])
    ig = _sig(_blockdiag(xb, wi_ref) + bi_ref[...])
    log_a = (-LRU_C * r) * sp
    mult = jnp.sqrt(-_expm1(2.0 * log_a))
    return xb, r, ig, log_a, mult


def _prev8_spec(tr, w, cb):
    return pl.BlockSpec((8, w), lambda i: (jnp.maximum(i * (tr // 8) - 1, 0), cb))


def lru_fwd(z, conv_w, conv_b, w_rg, b_rg, w_ig, b_ig, lam, tr):
    S = z.shape[0]
    W = D

    def body(u_ref, up_ref, cw_ref, cb_ref, wr_ref, br_ref, wi_ref, bi_ref, lam_ref,
             xc_ref, a_ref, h_ref, buf, bt, hcar):
        i = pl.program_id(0)

        @pl.when(i == 0)
        def _():
            buf[0:8, :] = jnp.zeros((8, W), F32)
            hcar[...] = jnp.zeros_like(hcar)

        @pl.when(i > 0)
        def _():
            buf[0:8, :] = up_ref[...]

        buf[8:8 + tr, :] = u_ref[...]
        cw = cw_ref[...]
        xc = buf[pl.ds(5, tr), :] * cw[0:1, :]
        for kk in range(1, 4):
            xc = xc + buf[pl.ds(5 + kk, tr), :] * cw[kk:kk + 1, :]
        xc = xc + cb_ref[...]
        xc_ref[...] = xc
        sp = _softplus(-lam_ref[...])
        _, _, ig, log_a, mult = _gates(xc, wr_ref, br_ref, wi_ref, bi_ref, sp)
        a_ref[...] = jnp.exp(log_a)
        bt[...] = mult * (ig * xc)
        row = lax.broadcasted_iota(jnp.int32, (8, W), 0)

        def grp(g, hp):
            off = pl.multiple_of(g * 8, 8)
            A = a_ref[pl.ds(off, 8), :]
            B = bt[pl.ds(off, 8), :]
            for d in (1, 2, 4):
                ok = row >= d
                B = jnp.where(ok, A * pltpu.roll(B, d, 0) + B, B)
                A = jnp.where(ok, A * pltpu.roll(A, d, 0), A)
            hh = A * hp + B
            h_ref[pl.ds(off, 8), :] = hh
            return hh[7:8, :]

        hcar[...] = lax.fori_loop(0, tr // 8, grp, hcar[...])

    return _rowcall(
        "lru_fwd", body, S, tr, [z, z, conv_w, conv_b, w_rg, b_rg, w_ig, b_ig, lam],
        [_rs(tr, W, 1), _prev8_spec(tr, W, 1), _fs((4, W)), _fs((1, W)), _fs((NBLK, BD, BD)), _fs((1, W)),
         _fs((NBLK, BD, BD)), _fs((1, W)), _fs((1, W))],
        (_sds((S, W), F32),) * 3, (_rs(tr, W),) * 3,
        scratch=[pltpu.VMEM((tr + 8, W), F32), pltpu.VMEM((tr, W), F32), pltpu.VMEM((1, W), F32)])


def lru_bwd(dh, a, hs, xc, w_rg, b_rg, w_ig, b_ig, lam, tr):
    S, W = dh.shape
    n = S // tr
    nb8 = S // 8

    def rev(i):
        return n - 1 - i

    row_spec = pl.BlockSpec((tr, W), lambda i: (rev(i), 0))
    next8 = pl.BlockSpec((8, W), lambda i: (jnp.minimum((rev(i) + 1) * (tr // 8), nb8 - 1), 0))
    prev8 = pl.BlockSpec((8, W), lambda i: (jnp.maximum(rev(i) * (tr // 8) - 1, 0), 0))

    def body(dh_ref, a_ref, an_ref, h_ref, hp_ref, xc_ref, wr_ref, br_ref, wi_ref, bi_ref, lam_ref,
             dxc_ref, dwr_ref, dwi_ref, dbr_ref, dbi_ref, dlam_ref, dcb_ref,
             bufa, bufh, apr, gsc, gcar, acc_br, acc_bi, acc_sp, acc_cb):
        i = pl.program_id(0)
        first = i == 0
        last_tile = i == n - 1

        bufa[0:tr, :] = a_ref[...]
        bufa[tr:tr + 8, :] = an_ref[...]
        apr[...] = bufa[pl.ds(1, tr), :]
        bufh[8:8 + tr, :] = h_ref[...]

        @pl.when(last_tile)
        def _():
            bufh[0:8, :] = jnp.zeros((8, W), F32)

        @pl.when(jnp.logical_not(last_tile))
        def _():
            bufh[0:8, :] = hp_ref[...]

        @pl.when(first)
        def _():
            gcar[...] = jnp.zeros_like(gcar)

        row = lax.broadcasted_iota(jnp.int32, (8, W), 0)
        ng = tr // 8

        def grp(t, gn):
            g = ng - 1 - t
            off = pl.multiple_of(g * 8, 8)
            A = apr[pl.ds(off, 8), :]
            B = dh_ref[pl.ds(off, 8), :]
            for d in (1, 2, 4):
                ok = row < 8 - d
                B = jnp.where(ok, B + A * pltpu.roll(B, 8 - d, 0), B)
                A = jnp.where(ok, A * pltpu.roll(A, 8 - d, 0), A)
            gg = B + A * gn
            gsc[pl.ds(off, 8), :] = gg
            return gg[0:1, :]

        gcar[...] = lax.fori_loop(0, ng, grp, gcar[...])

        G = gsc[...]
        hprev = bufh[pl.ds(7, tr), :]
        xcv = xc_ref[...]
        lam_v = lam_ref[...]
        sp = _softplus(-lam_v)
        xb, r, ig, log_a, mult = _gates(xcv, wr_ref, br_ref, wi_ref, bi_ref, sp)
        av = a_ref[...]
        d_a = G * hprev
        d_mult = G * (ig * xcv)
        d_ig = G * (mult * xcv)
        d_log_a = d_a * av - d_mult * (av * av / mult)
        d_gr = (d_log_a * (-LRU_C * sp)) * (r * (1.0 - r))
        d_gi = d_ig * (ig * (1.0 - ig))
        gr_b = d_gr.astype(BF)
        gi_b = d_gi.astype(BF)
        dxc = G * (mult * ig)
        dxc = dxc + jnp.concatenate(
            [_dot(gr_b[:, k * BD:(k + 1) * BD], wr_ref[k], NT)
             + _dot(gi_b[:, k * BD:(k + 1) * BD], wi_ref[k], NT) for k in range(NBLK)], axis=1)
        dxc_ref[...] = dxc

        @pl.when(first)
        def _():
            dwr_ref[...] = jnp.zeros_like(dwr_ref)
            dwi_ref[...] = jnp.zeros_like(dwi_ref)

        for k in range(NBLK):
            sl = slice(k * BD, (k + 1) * BD)
            dwr_ref[k] += _dot_tn(xb[:, sl], gr_b[:, sl])
            dwi_ref[k] += _dot_tn(xb[:, sl], gi_b[:, sl])

        _acc_rows(i, n, acc_br, dbr_ref, _fold8(d_gr))
        _acc_rows(i, n, acc_bi, dbi_ref, _fold8(d_gi))
        _acc_rows(i, n, acc_cb, dcb_ref, _fold8(dxc))

        part = _fold8(d_log_a * (-LRU_C * r))

        @pl.when(first)
        def _():
            acc_sp[...] = part

        @pl.when(jnp.logical_not(first))
        def _():
            acc_sp[...] += part

        @pl.when(last_tile)
        def _():
            dsp = jnp.sum(acc_sp[...], axis=0, keepdims=True)
            dlam_ref[...] = dsp * (-_sig(-lam_v))

    vec = _sds((1, W), F32)
    wsh = _sds((NBLK, BD, BD), F32)
    return _rowcall(
        "lru_bwd", body, S, tr, [dh, a, a, hs, hs, xc, w_rg, b_rg, w_ig, b_ig, lam],
        [row_spec, row_spec, next8, row_spec, prev8, row_spec, _fs((NBLK, BD, BD)), _fs((1, W)),
         _fs((NBLK, BD, BD)), _fs((1, W)), _fs((1, W))],
        (_sds((S, W), F32), wsh, wsh, vec, vec, vec, vec),
        (row_spec, _fs((NBLK, BD, BD)), _fs((NBLK, BD, BD)), _fs((1, W)), _fs((1, W)), _fs((1, W)), _fs((1, W))),
        scratch=[pltpu.VMEM((tr + 8, W), F32), pltpu.VMEM((tr + 8, W), F32), pltpu.VMEM((tr, W), F32),
                 pltpu.VMEM((tr, W), F32), pltpu.VMEM((1, W), F32), pltpu.VMEM((8, W), F32),
                 pltpu.VMEM((8, W), F32), pltpu.VMEM((8, W), F32), pltpu.VMEM((8, W), F32)])


def conv_bwd(dxc, z, conv_w, tr):
    S, W = dxc.shape
    n = S // tr
    nb8 = S // 8

    def body(d_ref, dn_ref, u_ref, up_ref, cw_ref, du_ref, dcw_ref, bufd, bufu, acc):
        i = pl.program_id(0)
        bufd[0:tr, :] = d_ref[...]

        @pl.when(i == n - 1)
        def _():
            bufd[tr:tr + 8, :] = jnp.zeros((8, W), F32)

        @pl.when(i < n - 1)
        def _():
            bufd[tr:tr + 8, :] = dn_ref[...]

        @pl.when(i == 0)
        def _():
            bufu[0:8, :] = jnp.zeros((8, W), F32)

        @pl.when(i > 0)
        def _():
            bufu[0:8, :] = up_ref[...]

        bufu[8:8 + tr, :] = u_ref[...]
        cw = cw_ref[...]
        dv = d_ref[...]
        du = dv * cw[3:4, :]
        for j in range(1, 4):
            du = du + bufd[pl.ds(j, tr), :] * cw[3 - j:4 - j, :]
        du_ref[...] = du.astype(BF)
        parts = [jnp.sum(_fold8(dv * bufu[pl.ds(5 + kk, tr), :]), axis=0, keepdims=True) for kk in range(4)]
        part = jnp.concatenate(parts, axis=0)

        @pl.when(i == 0)
        def _():
            acc[...] = part

        @pl.when(i > 0)
        def _():
            acc[...] += part

        @pl.when(i == n - 1)
        def _():
            dcw_ref[...] = acc[...]

    next8 = pl.BlockSpec((8, W), lambda i: (jnp.minimum((i + 1) * (tr // 8), nb8 - 1), 0))
    return _rowcall(
        "conv_bwd", body, S, tr, [dxc, dxc, z, z, conv_w],
        [_rs(tr, W), next8, _rs(tr, W, 1), _prev8_spec(tr, W, 1), _fs((4, W))],
        (_sds((S, W), BF), _sds((4, W), F32)), (_rs(tr, W), _fs((4, W))),
        scratch=[pltpu.VMEM((tr + 8, W), F32), pltpu.VMEM((tr + 8, W), F32), pltpu.VMEM((4, W), F32)])


def _me():
    x = lax.axis_index("x")
    y = lax.axis_index("y")
    c = lax.axis_index("c")
    return x, y, c


def _peer(r):
    x, y, c = _me()
    px = jnp.bitwise_xor(x, (r >> 2) & 1)
    py = jnp.bitwise_xor(y, (r >> 1) & 1)
    pc = jnp.bitwise_xor(c, r & 1)
    return (px, py, pc), 4 * px + 2 * py + pc


class Comm:
    def __init__(self, arrays, modes):
        self.arrays = list(arrays)
        self.modes = list(modes)
        self.n = len(self.arrays)

    def out_shapes(self):
        return [_sds(((NDEV,) + a.shape) if md == "gather" else a.shape, a.dtype)
                for a, md in zip(self.arrays, self.modes)]

    def scratch(self):
        return [pltpu.SemaphoreType.DMA((self.n * (NDEV - 1),)),
                pltpu.SemaphoreType.DMA((self.n * (NDEV - 1),)),
                pltpu.SemaphoreType.DMA((self.n,))]

    def copies(self, ins, outs, send_sems, recv_sems, loc_sems):
        x, y, c = _me()
        me = 4 * x + 2 * y + c
        cps = []
        for ai, md in enumerate(self.modes):
            src = ins[ai] if md == "gather" else ins[ai].at[me]
            cps.append(pltpu.make_async_copy(src, outs[ai].at[me], loc_sems.at[ai]))
        for r in range(1, NDEV):
            dev, idx = _peer(r)
            for ai, md in enumerate(self.modes):
                src = ins[ai] if md == "gather" else ins[ai].at[idx]
                k = ai * (NDEV - 1) + r - 1
                cps.append(pltpu.make_async_remote_copy(
                    src_ref=src, dst_ref=outs[ai].at[me], send_sem=send_sems.at[k], recv_sem=recv_sems.at[k],
                    device_id=dev, device_id_type=pl.DeviceIdType.MESH))
        return cps


ANY_SPEC = pl.BlockSpec(memory_space=pl.ANY)


def comm_call(name, comm):
    na = comm.n

    def body(*refs):
        cps = comm.copies(refs[:na], refs[na:2 * na], *refs[2 * na:])
        for cp in cps:
            cp.start()
        for cp in cps:
            cp.wait()

    return pl.pallas_call(
        body, name=name, in_specs=[ANY_SPEC] * na, out_specs=[ANY_SPEC] * na, out_shape=comm.out_shapes(),
        scratch_shapes=comm.scratch(), compiler_params=pltpu.CompilerParams(has_side_effects=True),
    )(*comm.arrays)


def adamw(name, parts, w, m, v, tr):
    L = len(parts)
    _, R, C = parts[0].shape
    tr = _tile(R, tr)

    def body(*refs):
        p_refs = refs[:L]
        w_ref, m_ref, v_ref, g_ref, d_ref, nm_ref, nv_ref = refs[L:]
        for l in range(L):
            g = p_refs[l][0].astype(F32)
            for k in range(1, NDEV):
                g = g + p_refs[l][k].astype(F32)
            g_ref[l] = g
            mn = ADAM_B1 * m_ref[l] + (1.0 - ADAM_B1) * g
            vn = ADAM_B2 * v_ref[l] + (1.0 - ADAM_B2) * (g * g)
            m_hat = mn / (1.0 - ADAM_B1 ** ADAM_STEP)
            v_hat = vn / (1.0 - ADAM_B2 ** ADAM_STEP)
            d_ref[l] = -ADAM_LR * (m_hat / (jnp.sqrt(v_hat) + ADAM_EPS) + ADAM_WD * w_ref[l])
            nm_ref[l] = mn
            nv_ref[l] = vn

    blk = pl.BlockSpec((L, tr, C), lambda i: (0, i, 0))
    pblk = pl.BlockSpec((NDEV, tr, C), lambda i: (0, i, 0))
    return pl.pallas_call(
        body, name=name, grid=(R // tr,), in_specs=[pblk] * L + [blk, blk, blk],
        out_specs=(blk,) * 4, out_shape=(_sds((L, R, C), F32),) * 4,
        compiler_params=_params(("parallel",)),
    )(*parts, w, m, v)


def _rot(w):
    h = w.shape[-1] // 2
    return jnp.concatenate([-w[..., h:], w[..., :h]], axis=-1)


def _unrot(dw):
    h = dw.shape[-1] // 2
    return jnp.concatenate([dw[..., h:], -dw[..., :h]], axis=-1)


def _cols(g):
    n, R, C = g.shape
    return g.transpose(1, 0, 2).reshape(R, n * C)


def _rows(g):
    n, R, C = g.shape
    return g.reshape(n * R, C)


def _split_cols(dw):
    R, NC = dw.shape
    return dw.reshape(R, NDEV, NC // NDEV).transpose(1, 0, 2)


REPL = ["attn_norm", "q_a_norm", "kv_a_norm", "conv_b", "w_rg", "b_rg", "w_ig", "b_ig", "lru_lambda",
        "ple_norm", "final_norm"]
SHARDED = ["w_in", "w_q_b", "w_kv_b", "conv_w", "w_o_mla", "w_o_lru", "w_out", "w_ple_gate", "w_ple"]
WEIGHTS = ["attn_norm", "w_in", "q_a_norm", "w_q_b", "kv_a_norm", "w_kv_b", "conv_w", "conv_b", "w_rg", "b_rg",
           "w_ig", "b_ig", "lru_lambda", "w_o_mla", "w_o_lru", "w_out", "ple_norm", "w_ple_gate", "w_ple",
           "final_norm"]


REPL_LAYER = REPL[1:-1]


def _pack(vals):
    flat = jnp.concatenate([v.reshape(-1) for v in vals])
    n = flat.shape[0]
    rows = -(-n // (128 * 256)) * 256
    return jnp.pad(flat, (0, rows * 128 - n)).reshape(rows, 128)


def _unpack(packed, shapes):
    flat = packed.reshape(-1)
    out = []
    off = 0
    for s in shapes:
        n = int(np.prod(s))
        out.append(flat[off:off + n].reshape(s))
        off += n
    return out


def kernel(x, p, positions, attn_norm, w_in, q_a_norm, w_q_b, kv_a_norm, w_kv_b, conv_w, conv_b, w_rg, b_rg, w_ig, b_ig, lru_lambda, w_o_mla, w_o_lru, w_out, ple_norm, w_ple_gate, w_ple, final_norm, loss_target, m_attn_norm, m_w_in, m_q_a_norm, m_w_q_b, m_kv_a_norm, m_w_kv_b, m_conv_w, m_conv_b, m_w_rg, m_b_rg, m_w_ig, m_b_ig, m_lru_lambda, m_w_o_mla, m_w_o_lru, m_w_out, m_ple_norm, m_w_ple_gate, m_w_ple, m_final_norm, v_attn_norm, v_w_in, v_q_a_norm, v_w_q_b, v_kv_a_norm, v_w_kv_b, v_conv_w, v_conv_b, v_w_rg, v_b_rg, v_w_ig, v_b_ig, v_lru_lambda, v_w_o_mla, v_w_o_lru, v_w_out, v_ple_norm, v_w_ple_gate, v_w_ple, v_final_norm):
    W = dict(attn_norm=attn_norm, w_in=w_in, q_a_norm=q_a_norm, w_q_b=w_q_b, kv_a_norm=kv_a_norm, w_kv_b=w_kv_b,
             conv_w=conv_w, conv_b=conv_b, w_rg=w_rg, b_rg=b_rg, w_ig=w_ig, b_ig=b_ig, lru_lambda=lru_lambda,
             w_o_mla=w_o_mla, w_o_lru=w_o_lru, w_out=w_out, ple_norm=ple_norm, w_ple_gate=w_ple_gate, w_ple=w_ple,
             final_norm=final_norm)
    M = dict(attn_norm=m_attn_norm, w_in=m_w_in, q_a_norm=m_q_a_norm, w_q_b=m_w_q_b, kv_a_norm=m_kv_a_norm,
             w_kv_b=m_w_kv_b, conv_w=m_conv_w, conv_b=m_conv_b, w_rg=m_w_rg, b_rg=m_b_rg, w_ig=m_w_ig, b_ig=m_b_ig,
             lru_lambda=m_lru_lambda, w_o_mla=m_w_o_mla, w_o_lru=m_w_o_lru, w_out=m_w_out, ple_norm=m_ple_norm,
             w_ple_gate=m_w_ple_gate, w_ple=m_w_ple, final_norm=m_final_norm)
    V = dict(attn_norm=v_attn_norm, w_in=v_w_in, q_a_norm=v_q_a_norm, w_q_b=v_w_q_b, kv_a_norm=v_kv_a_norm,
             w_kv_b=v_w_kv_b, conv_w=v_conv_w, conv_b=v_conv_b, w_rg=v_w_rg, b_rg=v_b_rg, w_ig=v_w_ig, b_ig=v_b_ig,
             lru_lambda=v_lru_lambda, w_o_mla=v_w_o_mla, w_o_lru=v_w_o_lru, w_out=v_w_out, ple_norm=v_ple_norm,
             w_ple_gate=v_w_ple_gate, w_ple=v_w_ple, final_norm=v_final_norm)

    L = w_in.shape[0]
    S = x.shape[1]
    xs = x[0]
    tgt = loss_target[0]
    tr = _tile(S, max(8, min(256, S // 2)))
    tq = _tile(S, max(128, min(512, S // 2)))

    inv_freq = ROPE_THETA ** (-jnp.arange(0, ROPE, 2, dtype=F32) / ROPE)
    ang = positions[0].astype(F32)[:, None] * inv_freq
    cs = jnp.concatenate([jnp.cos(ang), jnp.cos(ang), jnp.sin(ang), jnp.sin(ang)], axis=1)

    GATHERED = ["w_in", "w_q_b", "w_kv_b", "w_o_mla", "w_o_lru", "w_out", "w_ple_gate", "w_ple"]

    def gather_set(l):
        return Comm([W[k][l].astype(BF) for k in GATHERED] + [conv_w[l]], ["gather"] * (len(GATHERED) + 1))

    def prep(g):
        g_in, g_qb, g_kvb, g_om, g_ol, g_out, g_pg, g_ple, g_cw = g
        win = _cols(g_in)
        kr_w = win[:, 1024:1088]
        wq = _cols(g_qb).reshape(QL, NH, 192)
        return dict(
            win=jnp.concatenate([win[:, 1088:], win[:, 0:1024], kr_w, _rot(kr_w),
                                 jnp.zeros((D, ZW - Z_KR - 128), BF)], axis=-1),
            wq=jnp.concatenate([wq, _rot(wq[..., 128:])], axis=-1).reshape(QL, NH * 256),
            wkv=_cols(g_kvb), wom=_rows(g_om), wol=_rows(g_ol), wout=_rows(g_out), wpg=_rows(g_pg),
            wple=_cols(g_ple), cw=_cols(g_cw))

    wts = [None] * L
    wts[0] = prep(comm_call("gather_w0", gather_set(0)))
    wrg_b = w_rg.astype(BF)
    wig_b = w_ig.astype(BF)

    def vec(a, l):
        return a[l][None, :]

    saved = []
    xcur = xs
    for l in range(L):
        wl = wts[l]
        h = norm_fwd("norm_in", xcur, vec(attn_norm, l), tr)
        z = mm("mm_in", h, wl["win"], "nn", tn=1280, tk=2048)
        qn, kvn, kr = qkv_prep(z, vec(q_a_norm, l), vec(kv_a_norm, l), cs, tr)
        q_ext = mm("mm_q", qn, wl["wq"], "nn", o_heads=True, tk=512)
        kv_ext = mm("mm_kv", kvn, wl["wkv"], "nn", o_heads=True, tk=512, out_dtype=BF)
        (o, lse, qf), got = attn_fwd(q_ext, kv_ext, kr, cs, tq, gather_set(l + 1) if l + 1 < L else None)
        if l + 1 < L:
            wts[l + 1] = prep(got)
        xc, a, hs = lru_fwd(z, wl["cw"], vec(conv_b, l), wrg_b[l], vec(b_rg, l), wig_b[l], vec(b_ig, l),
                            vec(lru_lambda, l), tr)
        om, hl = gate_out(o, z, hs, tr)
        ym = mm("mm_om", om, wl["wom"], "nn", tk=2048)
        yl = mm("mm_ol", hl, wl["wol"], "nn", tk=2048)
        mg = merge_fwd(ym, yl, z, tr)
        x1 = mm("mm_out", mg, wl["wout"], "nn", tk=2048, res=xcur)
        hp = norm_fwd("norm_ple", x1, vec(ple_norm, l), tr)
        gp = mm("mm_pg", hp, wl["wpg"], "nn", tk=2048)
        pe = mm("mm_ple", p[l, 0], wl["wple"], "nn")
        x2 = ple_fin(x1, pe, gp, tr)
        saved.append(dict(x=xcur, h=h, z=z, qn=qn, kvn=kvn, kr=kr, kv_ext=kv_ext, o=o, lse=lse, qf=qf, xc=xc, a=a,
                          hs=hs, om=om, hl=hl, ym=ym, yl=yl, mg=mg, x1=x1, hp=hp, gp=gp, pe=pe))
        xcur = x2

    lsum, dx, d_final = final_loss(xcur, final_norm[None, :], tgt, tr)
    loss = lax.psum(0.5 * lsum[0, 0] / D, MESH_AXES)

    gr = {k: [None] * L for k in WEIGHTS if k != "final_norm"}
    landed_early = [None] * L
    landed_late = [None] * L
    zero_fn = jnp.zeros_like(final_norm)

    def early_grads(l):
        return ([gr[k][l].reshape(NDEV, D // NDEV, D) for k in ("w_o_mla", "w_o_lru", "w_out", "w_ple_gate")]
                + [_split_cols(gr["w_ple"][l])], ["a2a"] * 5)

    def late_grads(l):
        rep = _pack([gr[k][l] for k in REPL_LAYER] + [d_final[0] if l == L - 1 else zero_fn])
        return ([_split_cols(gr[k][l]) for k in ("w_in", "w_q_b", "w_kv_b", "conv_w")] + [rep],
                ["a2a"] * 4 + ["gather"])

    for l in reversed(range(L)):
        sv = saved[l]
        wl = wts[l]
        z = sv["z"]
        dpe, dgp = ple_bwd(dx, sv["pe"], sv["gp"], tr)
        gr["w_ple"][l] = mm("mm_dple", p[l, 0], dpe, "tn", tm=256, out_dtype=BF)
        gr["w_ple_gate"][l] = mm("mm_dpg", sv["hp"], dgp, "tn", out_dtype=BF)
        dhp = mm("mm_dhp", dgp, wl["wpg"], "nt", tk=2048)
        dx1, gr["ple_norm"][l] = norm_bwd("norm_ple_bwd", sv["x1"], dhp, vec(ple_norm, l), dx, tr)
        dmg = mm("mm_dmg", dx1, wl["wout"], "nt", tk=2048)
        gr["w_out"][l] = mm("mm_dwout", sv["mg"], dx1, "tn", out_dtype=BF)
        dym, dyl, dz_m = merge_bwd(dmg, sv["ym"], sv["yl"], z, tr)
        dom = mm("mm_dom", dym, wl["wom"], "nt", tk=2048)
        gr["w_o_mla"][l] = mm("mm_dwom", sv["om"], dym, "tn", out_dtype=BF)
        dhl = mm("mm_dhl", dyl, wl["wol"], "nt", tk=2048)
        gr["w_o_lru"][l] = mm("mm_dwol", sv["hl"], dyl, "tn", out_dtype=BF)
        do, dz_gm, dhs, dz_gl = gate_bwd(dom, sv["o"], dhl, sv["hs"], z, tr)
        arrs, modes = early_grads(l)
        if l + 1 < L:
            arrs2, modes2 = late_grads(l + 1)
            arrs, modes = arrs + arrs2, modes + modes2
        (dq_ext, dkv_ext, dkr), got = attn_bwd(sv["qf"], sv["kv_ext"], sv["kr"], sv["o"], do, sv["lse"], cs, tq,
                                               Comm(arrs, modes))
        landed_early[l] = got[:5]
        if l + 1 < L:
            landed_late[l + 1] = got[5:]
        dqn = mm("mm_dqn", dq_ext, wl["wq"], "nt", a_heads=True, tn=512)
        dwq_ext = mm("mm_dwq", sv["qn"], dq_ext, "tn", b_heads=True, tm=512, out_dtype=BF)
        dkn = mm("mm_dkn", dkv_ext, wl["wkv"], "nt", a_heads=True, tn=512)
        gr["w_kv_b"][l] = mm("mm_dwkv", sv["kvn"], dkv_ext, "tn", b_heads=True, tm=512, out_dtype=BF)
        dz_t, gr["q_a_norm"][l], gr["kv_a_norm"][l] = qkv_prep_bwd(
            z, dqn, dkn, dkr, vec(q_a_norm, l), vec(kv_a_norm, l), cs, tr)
        dwq4 = dwq_ext.reshape(QL, NH, 256)
        gr["w_q_b"][l] = jnp.concatenate(
            [dwq4[..., 0:128], dwq4[..., 128:192] + _unrot(dwq4[..., 192:256])], axis=-1).reshape(QL, NH * 192)
        (dxc, gr["w_rg"][l], gr["w_ig"][l], gr["b_rg"][l], gr["b_ig"][l], gr["lru_lambda"][l],
         gr["conv_b"][l]) = lru_bwd(dhs, sv["a"], sv["hs"], sv["xc"], wrg_b[l], vec(b_rg, l), wig_b[l],
                                    vec(b_ig, l), vec(lru_lambda, l), tr)
        dz_u, gr["conv_w"][l] = conv_bwd(dxc, z, wl["cw"], tr)
        dz = jnp.concatenate([dz_gm, dz_u, dz_gl, dz_m, dz_t], axis=1)
        dwin_ext = mm("mm_dwin", sv["h"], dz, "tn", tn=1280, out_dtype=BF)
        gr["w_in"][l] = jnp.concatenate(
            [dwin_ext[:, Z_Q:Z_KR], dwin_ext[:, Z_KR:Z_KR + 64] + _unrot(dwin_ext[:, Z_KR + 64:Z_KR + 128]),
             dwin_ext[:, 0:Z_Q]], axis=1)
        if l == 0:
            dh, landed_late[0] = mm("mm_dh_comm", dz, wl["win"], "nt", tk=1280, comm=Comm(*late_grads(0)))
        else:
            dh = mm("mm_dh", dz, wl["win"], "nt", tk=1280)
        dx, gr["attn_norm"][l] = norm_bwd("norm_in_bwd", sv["x"], dh, vec(attn_norm, l), dx1, tr)
    grad_x = dx[None]
    (landed_an,) = comm_call("gather_attn_norm", Comm([jnp.concatenate(gr["attn_norm"]).reshape(L * D // 128, 128)],
                                                      ["gather"]))

    res = {}
    for ki, k in enumerate(SHARDED):
        parts = [landed_late[l][ki] if ki < 4 else landed_early[l][ki - 4] for l in range(L)]
        res[k] = adamw("adamw_" + k, parts, W[k], M[k], V[k], 64)

    def rep_pack(T):
        fn = jnp.concatenate([jnp.zeros((L - 1, D), F32), T["final_norm"][None]], axis=0)
        flat = jnp.concatenate([T[k].reshape(L, -1) for k in REPL_LAYER] + [fn], axis=1)
        rows = landed_late[0][4].shape[1]
        return jnp.pad(flat, ((0, 0), (0, rows * 128 - flat.shape[1]))).reshape(L, rows, 128)

    rep = adamw("adamw_rep", [landed_late[l][4] for l in range(L)], rep_pack(W), rep_pack(M), rep_pack(V), 256)
    shapes = [W[k].shape[1:] for k in REPL_LAYER] + [final_norm.shape]
    per_layer = [[_unpack(t[l], shapes) for l in range(L)] for t in rep]
    for i, k in enumerate(REPL_LAYER):
        res[k] = tuple(jnp.stack([per_layer[t][l][i] for l in range(L)]) for t in range(4))
    res["final_norm"] = tuple(per_layer[t][L - 1][-1] for t in range(4))
    an = adamw("adamw_attn_norm", [landed_an], *(T["attn_norm"].reshape(1, L * D // 128, 128) for T in (W, M, V)), 64)
    res["attn_norm"] = tuple(t.reshape(L, D) for t in an)

    outs = [loss, grad_x]
    for t in range(4):
        outs += [res[k][t] for k in WEIGHTS]
    return tuple(outs)
```

```python
import functools
import math

import numpy as np
import jax
import jax.numpy as jnp
from jax import lax
from jax.experimental import pallas as pl
from jax.experimental.pallas import tpu as pltpu

F32 = jnp.float32
BF = jnp.bfloat16
MESH_AXES = ("x", "y", "c")
NDEV = 8

D = 2048
NH = 16
QL = 512
KVL = 512
ROPE = 64
PLE = 256
NBLK = 16
BD = 128
CHUNK_SHIFT = 6
EPS = 1e-6
LRU_C = 8.0
ROPE_THETA = 10000.0
IN_TOTAL = 11328
ZW = 11520
Z_Q = 10240
Z_KV = 10752
Z_KR = 11264
SCALE = 1.0 / math.sqrt(128 + 64)
EXP2_SCALE = SCALE * math.log2(math.e)
LOG2E = math.log2(math.e)

ADAM_LR = 0.001
ADAM_B1 = 0.9
ADAM_B2 = 0.999
ADAM_EPS = 1e-08
ADAM_WD = 0.01
ADAM_STEP = 10

VMEM_LIMIT = 60 * 1024 * 1024

NN = (((1,), (0,)), ((), ()))
NT = (((1,), (1,)), ((), ()))


def _tile(n, pref):
    t = min(n, pref)
    while n % t:
        t //= 2
    return t


def _params(sem):
    return pltpu.CompilerParams(dimension_semantics=sem, vmem_limit_bytes=VMEM_LIMIT)


def _dot(a, b, dims=NN):
    return lax.dot_general(a, b, dims, preferred_element_type=F32)


def _dot_tn(a, b):
    return lax.dot_general(a.T, b, NN, preferred_element_type=F32)


def _bf(v):
    return v if v.dtype == BF else v.astype(BF)


def mm(name, a, b, mode, *, out_dtype=F32, tm=1024, tn=1024, tk=1024, res=None,
       a_heads=False, b_heads=False, o_heads=False, comm=None):
    if mode == "nn":
        M = a.shape[0]
        K = a.shape[1]
        N = b.shape[1]
    elif mode == "nt":
        M = a.shape[1] if a_heads else a.shape[0]
        K = a.shape[0] * a.shape[2] if a_heads else a.shape[1]
        N = b.shape[0]
    else:
        K = a.shape[0]
        M = a.shape[1]
        N = b.shape[0] * b.shape[2] if b_heads else b.shape[1]
    tm = _tile(M, tm)
    tn = _tile(N, tn)
    tk = _tile(K, tk)
    nk = K // tk
    grid = (M // tm, N // tn, nk)
    nc = comm.n if comm is not None else 0

    if mode == "nn":
        a_spec = pl.BlockSpec((tm, tk), lambda i, j, k: (i, k))
        b_spec = pl.BlockSpec((tk, tn), lambda i, j, k: (k, j))
    elif mode == "nt":
        if a_heads:
            a_spec = pl.BlockSpec((tk // 256, tm, 256), lambda i, j, k: (k, i, 0))
        else:
            a_spec = pl.BlockSpec((tm, tk), lambda i, j, k: (i, k))
        b_spec = pl.BlockSpec((tn, tk), lambda i, j, k: (j, k))
    else:
        a_spec = pl.BlockSpec((tk, tm), lambda i, j, k: (k, i))
        if b_heads:
            b_spec = pl.BlockSpec((tn // 256, tk, 256), lambda i, j, k: (j, k, 0))
        else:
            b_spec = pl.BlockSpec((tk, tn), lambda i, j, k: (k, j))
    if o_heads:
        o_spec = pl.BlockSpec((tn // 256, tm, 256), lambda i, j, k: (j, i, 0))
        o_shape = jax.ShapeDtypeStruct((N // 256, M, 256), out_dtype)
    else:
        o_spec = pl.BlockSpec((tm, tn), lambda i, j, k: (i, j))
        o_shape = jax.ShapeDtypeStruct((M, N), out_dtype)
    in_specs = [a_spec, b_spec]
    args = [a, b]
    if res is not None:
        in_specs.append(pl.BlockSpec((tm, tn), lambda i, j, k: (i, j)))
        args.append(res)
    n_in = len(args)

    def heads(ref):
        return jnp.concatenate([ref[e] for e in range(ref.shape[0])], axis=1)

    def body(*refs):
        ins, (o_ref,), scr, cps = _split_refs(refs, n_in, 1, comm)
        a_ref, b_ref = ins[0], ins[1]
        r_ref = ins[2] if res is not None else None
        ids = [pl.program_id(d) for d in range(3)]
        _side_start(cps, functools.reduce(jnp.logical_and, [ids[d] == 0 for d in range(3)]))
        av = _bf(heads(a_ref) if a_heads else a_ref[...])
        bv = _bf(heads(b_ref) if b_heads else b_ref[...])
        if mode == "nn":
            p = _dot(av, bv, NN)
        elif mode == "nt":
            p = _dot(av, bv, NT)
        else:
            p = _dot_tn(av, bv)

        def fin(acc):
            if r_ref is not None:
                acc = acc + r_ref[...]
            acc = acc.astype(o_ref.dtype)
            if o_heads:
                for e in range(tn // 256):
                    o_ref[e] = acc[:, e * 256:(e + 1) * 256]
            else:
                o_ref[...] = acc

        if nk == 1:
            fin(p)
        else:
            acc_ref = scr[0]
            k = ids[2]

            @pl.when(k == 0)
            def _():
                acc_ref[...] = p

            @pl.when(k > 0)
            def _():
                acc_ref[...] += p

            @pl.when(k == nk - 1)
            def _():
                fin(acc_ref[...])

        _side_wait(cps, functools.reduce(jnp.logical_and, [ids[d] == grid[d] - 1 for d in range(3)]))

    scratch = ([] if nk == 1 else [pltpu.VMEM((tm, tn), F32)]) + (comm.scratch() if comm is not None else [])
    sem = ("arbitrary",) * 3 if comm is not None else ("parallel", "parallel", "arbitrary")
    out = pl.pallas_call(
        body, name=name, grid=grid, in_specs=in_specs + [ANY_SPEC] * nc, out_specs=[o_spec] + [ANY_SPEC] * nc,
        out_shape=[o_shape] + (comm.out_shapes() if comm is not None else []),
        scratch_shapes=scratch, compiler_params=_params(sem),
    )(*args, *(comm.arrays if comm is not None else []))
    return out[0] if comm is None else (out[0], out[1:])


def _rs(tr, w, cb=0):
    return pl.BlockSpec((tr, w), lambda i: (i, cb))


def _fs(shape):
    nd = len(shape)
    return pl.BlockSpec(shape, lambda i: (0,) * nd)


def _rowcall(name, body, S, tr, ins, in_specs, outs, out_specs, scratch=()):
    return pl.pallas_call(
        body, name=name, grid=(S // tr,), in_specs=in_specs, out_specs=out_specs, out_shape=outs,
        scratch_shapes=list(scratch), compiler_params=_params(("arbitrary",)),
    )(*ins)


def _sds(shape, dt):
    return jax.ShapeDtypeStruct(shape, dt)


def _fold8(v):
    tr, w = v.shape
    return jnp.sum(v.reshape(tr // 8, 8, w), axis=0)


def _acc_rows(i, n, acc_ref, out_ref, part):
    @pl.when(i == 0)
    def _():
        acc_ref[...] = part

    @pl.when(i > 0)
    def _():
        acc_ref[...] += part

    @pl.when(i == n - 1)
    def _():
        out_ref[...] = jnp.sum(acc_ref[...], axis=0, keepdims=True)


def _rms_fwd(x, g):
    r = lax.rsqrt(jnp.mean(x * x, axis=-1, keepdims=True) + EPS)
    return x * r * g


def _rms_bwd(x, dy, g):
    r = lax.rsqrt(jnp.mean(x * x, axis=-1, keepdims=True) + EPS)
    xh = x * r
    dxh = dy * g
    dx = r * (dxh - xh * jnp.mean(dxh * xh, axis=-1, keepdims=True))
    return dx, dy * xh


def _sig(x):
    return jax.nn.sigmoid(x)


def norm_fwd(name, x, g, tr):
    S, W = x.shape

    def body(x_ref, g_ref, o_ref):
        o_ref[...] = _rms_fwd(x_ref[...], g_ref[...]).astype(BF)

    return _rowcall(name, body, S, tr, [x, g], [_rs(tr, W), _fs((1, W))], _sds((S, W), BF), _rs(tr, W))


def norm_bwd(name, x, dy, g, dres, tr):
    S, W = x.shape
    n = S // tr

    def body(x_ref, dy_ref, g_ref, dr_ref, dx_ref, dg_ref, acc_ref):
        i = pl.program_id(0)
        dx, dgp = _rms_bwd(x_ref[...], dy_ref[...], g_ref[...])
        dx_ref[...] = dr_ref[...] + dx
        _acc_rows(i, n, acc_ref, dg_ref, _fold8(dgp))

    return _rowcall(name, body, S, tr, [x, dy, g, dres],
                    [_rs(tr, W), _rs(tr, W), _fs((1, W)), _rs(tr, W)],
                    (_sds((S, W), F32), _sds((1, W), F32)), (_rs(tr, W), _fs((1, W))),
                    scratch=[pltpu.VMEM((8, W), F32)])


def _lane_lt(shape, n):
    return lax.broadcasted_iota(jnp.int32, shape, 1) < n


def qkv_prep(z, qg, kvg, cs, tr):
    S = z.shape[0]

    def body(ql_ref, kl_ref, zk_ref, qg_ref, kg_ref, cs_ref, qn_ref, kn_ref, kr_ref):
        qn_ref[...] = _rms_fwd(ql_ref[...], qg_ref[...]).astype(BF)
        kn_ref[...] = _rms_fwd(kl_ref[...], kg_ref[...]).astype(BF)
        t = zk_ref[...] * cs_ref[...]
        t = t + pltpu.roll(t, 64, 1)
        kr_ref[...] = jnp.where(_lane_lt(t.shape, 64), t, 0.0).astype(BF)

    return _rowcall(
        "qkv_prep", body, S, tr, [z, z, z, qg, kvg, cs],
        [_rs(tr, QL, Z_Q // QL), _rs(tr, KVL, Z_KV // KVL), _rs(tr, 128, Z_KR // 128),
         _fs((1, QL)), _fs((1, KVL)), _rs(tr, 128)],
        (_sds((S, QL), BF), _sds((S, KVL), BF), _sds((S, 128), BF)),
        (_rs(tr, QL), _rs(tr, KVL), _rs(tr, 128)))


def qkv_prep_bwd(z, dqn, dkn, dkr, qg, kvg, cs, tr):
    S = z.shape[0]
    n = S // tr

    def body(ql_ref, kl_ref, dq_ref, dk_ref, dkr_ref, qg_ref, kg_ref, cs_ref,
             dz_ref, dqg_ref, dkg_ref, accq, acck):
        i = pl.program_id(0)
        dql, gq = _rms_bwd(ql_ref[...], dq_ref[...], qg_ref[...])
        dkl, gk = _rms_bwd(kl_ref[...], dk_ref[...], kg_ref[...])
        t = jnp.where(_lane_lt((tr, 128), 64), dkr_ref[...], 0.0)
        t = (t + pltpu.roll(t, 64, 1)) * cs_ref[...]
        dz_ref[...] = jnp.concatenate(
            [dql.astype(BF), dkl.astype(BF), t.astype(BF), jnp.zeros((tr, 128), BF)], axis=1)
        _acc_rows(i, n, accq, dqg_ref, _fold8(gq))
        _acc_rows(i, n, acck, dkg_ref, _fold8(gk))

    return _rowcall(
        "qkv_prep_bwd", body, S, tr, [z, z, dqn, dkn, dkr, qg, kvg, cs],
        [_rs(tr, QL, Z_Q // QL), _rs(tr, KVL, Z_KV // KVL), _rs(tr, QL), _rs(tr, KVL), _rs(tr, 128),
         _fs((1, QL)), _fs((1, KVL)), _rs(tr, 128)],
        (_sds((S, ZW - Z_Q), BF), _sds((1, QL), F32), _sds((1, KVL), F32)),
        (_rs(tr, ZW - Z_Q), _fs((1, QL)), _fs((1, KVL))),
        scratch=[pltpu.VMEM((8, QL), F32), pltpu.VMEM((8, KVL), F32)])


def gate_out(o, z, hs, tr):
    S = o.shape[0]

    def body(o_ref, gm_ref, h_ref, gl_ref, om_ref, hl_ref):
        gm = gm_ref[...]
        gl = gl_ref[...]
        om_ref[...] = (o_ref[...] * (gm * _sig(gm))).astype(BF)
        hl_ref[...] = (h_ref[...] * (gl * _sig(gl))).astype(BF)

    return _rowcall("gate_out", body, S, tr, [o, z, hs, z],
                    [_rs(tr, D), _rs(tr, D, 0), _rs(tr, D), _rs(tr, D, 2)],
                    (_sds((S, D), BF), _sds((S, D), BF)), (_rs(tr, D), _rs(tr, D)))


def gate_bwd(dom, o, dhl, hs, z, tr):
    S = o.shape[0]

    def body(dom_ref, o_ref, dhl_ref, h_ref, gm_ref, gl_ref, do_ref, dgm_ref, dh_ref, dgl_ref):
        gm = gm_ref[...]
        sm = _sig(gm)
        dom_v = dom_ref[...]
        do_ref[...] = dom_v * (gm * sm)
        dgm_ref[...] = (dom_v * o_ref[...] * (sm * (1.0 + gm * (1.0 - sm)))).astype(BF)
        gl = gl_ref[...]
        sl = _sig(gl)
        dhl_v = dhl_ref[...]
        dh_ref[...] = dhl_v * (gl * sl)
        dgl_ref[...] = (dhl_v * h_ref[...] * (sl * (1.0 + gl * (1.0 - sl)))).astype(BF)

    return _rowcall("gate_bwd", body, S, tr, [dom, o, dhl, hs, z, z],
                    [_rs(tr, D), _rs(tr, D), _rs(tr, D), _rs(tr, D), _rs(tr, D, 0), _rs(tr, D, 2)],
                    (_sds((S, D), F32), _sds((S, D), BF), _sds((S, D), F32), _sds((S, D), BF)),
                    (_rs(tr, D),) * 4)


def merge_fwd(ym, yl, z, tr):
    S = ym.shape[0]

    def body(ym_ref, yl_ref, mm_ref, ml_ref, o_ref):
        o_ref[...] = (_sig(mm_ref[...]) * ym_ref[...] + _sig(ml_ref[...]) * yl_ref[...]).astype(BF)

    return _rowcall("merge_fwd", body, S, tr, [ym, yl, z, z],
                    [_rs(tr, D), _rs(tr, D), _rs(tr, D, 3), _rs(tr, D, 4)],
                    _sds((S, D), BF), _rs(tr, D))


def merge_bwd(dmg, ym, yl, z, tr):
    S = ym.shape[0]

    def body(d_ref, ym_ref, yl_ref, mm_ref, ml_ref, dym_ref, dyl_ref, dz_ref):
        d = d_ref[...]
        sm = _sig(mm_ref[...])
        sl = _sig(ml_ref[...])
        dym_ref[...] = (d * sm).astype(BF)
        dyl_ref[...] = (d * sl).astype(BF)
        dz_ref[:, 0:D] = (d * ym_ref[...] * (sm * (1.0 - sm))).astype(BF)
        dz_ref[:, D:2 * D] = (d * yl_ref[...] * (sl * (1.0 - sl))).astype(BF)

    return _rowcall("merge_bwd", body, S, tr, [dmg, ym, yl, z, z],
                    [_rs(tr, D), _rs(tr, D), _rs(tr, D), _rs(tr, D, 3), _rs(tr, D, 4)],
                    (_sds((S, D), BF), _sds((S, D), BF), _sds((S, 2 * D), BF)),
                    (_rs(tr, D), _rs(tr, D), _rs(tr, 2 * D)))


def ple_fin(x1, pe, gp, tr):
    S = x1.shape[0]

    def body(x_ref, pe_ref, gp_ref, o_ref):
        o_ref[...] = x_ref[...] + pe_ref[...] * _sig(gp_ref[...])

    return _rowcall("ple_fin", body, S, tr, [x1, pe, gp], [_rs(tr, D)] * 3, _sds((S, D), F32), _rs(tr, D))


def ple_bwd(dx2, pe, gp, tr):
    S = dx2.shape[0]

    def body(d_ref, pe_ref, gp_ref, dpe_ref, dgp_ref):
        d = d_ref[...]
        sg = _sig(gp_ref[...])
        dpe_ref[...] = (d * sg).astype(BF)
        dgp_ref[...] = (d * pe_ref[...] * (sg * (1.0 - sg))).astype(BF)

    return _rowcall("ple_bwd", body, S, tr, [dx2, pe, gp], [_rs(tr, D)] * 3,
                    (_sds((S, D), BF), _sds((S, D), BF)), (_rs(tr, D), _rs(tr, D)))


def final_loss(x, g, tgt, tr):
    S, W = x.shape
    n = S // tr

    def body(x_ref, g_ref, t_ref, ls_ref, dx_ref, dg_ref, accl, accg):
        i = pl.program_id(0)
        xv = x_ref[...]
        gv = g_ref[...]
        e = _rms_fwd(xv, gv) - t_ref[...]
        e2 = _fold8(e * e)
        l8 = e2[:, 0:128]
        for k in range(1, W // 128):
            l8 = l8 + e2[:, k * 128:(k + 1) * 128]
        dx, dgp = _rms_bwd(xv, e * (1.0 / W), gv)
        dx_ref[...] = dx
        _acc_rows(i, n, accg, dg_ref, _fold8(dgp))

        @pl.when(i == 0)
        def _():
            accl[...] = l8

        @pl.when(i > 0)
        def _():
            accl[...] += l8

        @pl.when(i == n - 1)
        def _():
            tot = jnp.sum(jnp.sum(accl[...], axis=0, keepdims=True), axis=1, keepdims=True)
            ls_ref[...] = jnp.broadcast_to(tot, (1, 128))

    return _rowcall("final_loss", body, S, tr, [x, g, tgt], [_rs(tr, W), _fs((1, W)), _rs(tr, W)],
                    (_sds((1, 128), F32), _sds((S, W), F32), _sds((1, W), F32)),
                    (_fs((1, 128)), _rs(tr, W), _fs((1, W))),
                    scratch=[pltpu.VMEM((8, 128), F32), pltpu.VMEM((8, W), F32)])


def _chunk_mask(qi, kj, tq, tk):
    rows = qi * tq + lax.broadcasted_iota(jnp.int32, (tq, tk), 0)
    cols = kj * tk + lax.broadcasted_iota(jnp.int32, (tq, tk), 1)
    return (cols >> CHUNK_SHIFT) <= (rows >> CHUNK_SHIFT)


def _load_kv(kv_ref, kr_ref, kj, tk):
    off = pl.multiple_of(kj * tk, tk)
    kv = kv_ref[pl.ds(off, tk), :]
    k = jnp.concatenate([kv[:, 0:128], kr_ref[pl.ds(off, tk), :]], axis=1)
    return k, kv[:, 128:256], off


def _split_refs(refs, n_in, n_out, comm):
    nc = comm.n if comm is not None else 0
    ins = refs[:n_in]
    c_in = refs[n_in:n_in + nc]
    outs = refs[n_in + nc:n_in + nc + n_out]
    c_out = refs[n_in + nc + n_out:n_in + 2 * nc + n_out]
    rest = refs[n_in + 2 * nc + n_out:]
    if comm is None:
        return ins, outs, rest, None
    return ins, outs, rest[:len(rest) - 3], comm.copies(c_in, c_out, *rest[len(rest) - 3:])


def _side_start(cps, first):
    if cps is None:
        return

    @pl.when(first)
    def _():
        for cp in cps:
            cp.start()


def _side_wait(cps, last):
    if cps is None:
        return

    @pl.when(last)
    def _():
        for cp in cps:
            cp.wait()


def attn_fwd(q_ext, kv_ext, kr, cs, tq, tk, comm=None):
    H, S, _ = q_ext.shape
    nq = S // tq
    nd = tq // tk
    nc = comm.n if comm is not None else 0
    HP = 2

    SL = min(64, tq)
    rep = tk // 128

    def body(*refs):
        ((q_ref, kv_ref, kr_ref, cs_ref), (o_ref, lse_ref, qf_ref),
         (s_scr, p_scr, m_scr, l_scr, a_scr, acc_scr), cps) = _split_refs(refs, 4, 3, comm)
        hh = pl.program_id(0)
        qi = pl.program_id(1)
        _side_start(cps, jnp.logical_and(hh == 0, qi == 0))
        for e in range(HP):
            q = q_ref[e]
            hi = q[:, 128:256] * cs_ref[...]
            hi = hi + pltpu.roll(hi, 64, 1)
            hi = jnp.where(_lane_lt(hi.shape, 64), hi, 0.0)
            qf_ref[e] = jnp.concatenate([q[:, 0:128], hi], axis=1).astype(BF)
        m_scr[...] = jnp.full(m_scr.shape, -jnp.inf, F32)
        l_scr[...] = jnp.zeros_like(l_scr)
        acc_scr[...] = jnp.zeros_like(acc_scr)

        def step(kj, masked):
            off = pl.multiple_of(kj * tk, tk)
            kr_t = kr_ref[pl.ds(off, tk), :]
            for e in range(HP):
                k = jnp.concatenate([kv_ref[e, pl.ds(off, tk), 0:128], kr_t], axis=1)
                s_scr[e] = _dot(qf_ref[e], k, NT)
            for e in range(HP):
                for r in range(tq // SL):
                    rows = pl.ds(r * SL, SL)
                    s = s_scr[e, rows, :]
                    if masked:
                        rws = qi * tq + r * SL + lax.broadcasted_iota(jnp.int32, (SL, tk), 0)
                        cls = kj * tk + lax.broadcasted_iota(jnp.int32, (SL, tk), 1)
                        s = jnp.where((cls >> CHUNK_SHIFT) <= (rws >> CHUNK_SHIFT), s, -jnp.inf)
                    m_old = m_scr[e, rows, :]
                    m_new = jnp.maximum(m_old, jnp.max(s, axis=1, keepdims=True))
                    alpha = jnp.exp2((m_old - m_new) * EXP2_SCALE)
                    p = jnp.exp2(s * EXP2_SCALE - jnp.tile(m_new * EXP2_SCALE, (1, rep)))
                    l_scr[e, rows, :] = alpha * l_scr[e, rows, :] + jnp.sum(p, axis=1, keepdims=True)
                    m_scr[e, rows, :] = m_new
                    a_scr[e, rows, :] = alpha
                    p_scr[e, rows, :] = p.astype(BF)
            for e in range(HP):
                pv = _dot(p_scr[e], kv_ref[e, pl.ds(off, tk), 128:256], NN)
                acc_scr[e] = a_scr[e] * acc_scr[e] + pv

        def trip(kj, c):
            step(kj, False)
            return c

        lax.fori_loop(0, qi * nd, trip, 0)
        for d in range(nd):
            step(qi * nd + d, True)
        for e in range(HP):
            l = l_scr[e]
            o_ref[:, e * 128:(e + 1) * 128] = acc_scr[e] / l
            lse_ref[e] = m_scr[e] * SCALE + jnp.log(l)
        _side_wait(cps, jnp.logical_and(hh == H // HP - 1, qi == nq - 1))

    res = pl.pallas_call(
        body, name="attn_fwd" if comm is None else "attn_fwd_comm", grid=(H // HP, nq),
        in_specs=[pl.BlockSpec((HP, tq, 256), lambda h, i: (h, i, 0)),
                  pl.BlockSpec((HP, S, 256), lambda h, i: (h, 0, 0)),
                  pl.BlockSpec((S, 128), lambda h, i: (0, 0)),
                  pl.BlockSpec((tq, 128), lambda h, i: (i, 0))] + [ANY_SPEC] * nc,
        out_specs=[pl.BlockSpec((tq, HP * 128), lambda h, i: (i, h)),
                   pl.BlockSpec((HP, tq, 128), lambda h, i: (h, i, 0)),
                   pl.BlockSpec((HP, tq, 256), lambda h, i: (h, i, 0))] + [ANY_SPEC] * nc,
        out_shape=[_sds((S, H * 128), F32), _sds((H, S, 128), F32), _sds((H, S, 256), BF)]
        + (comm.out_shapes() if comm is not None else []),
        scratch_shapes=[pltpu.VMEM((HP, tq, tk), F32), pltpu.VMEM((HP, tq, tk), BF)]
        + [pltpu.VMEM((HP, tq, 128), F32)] * 4 + (comm.scratch() if comm is not None else []),
        compiler_params=_params(("arbitrary", "arbitrary")),
    )(q_ext, kv_ext, kr, cs, *(comm.arrays if comm is not None else []))
    return res[:3], res[3:]


def attn_bwd(qf, kv_ext, kr, o, do, lse, cs, tq, tk, comm=None):
    H, S, _ = qf.shape
    nq = S // tq
    nd = tq // tk
    nc = comm.n if comm is not None else 0

    SL = min(64, tq)
    rep = tk // 128

    def body(*refs):
        ((q_ref, kv_ref, kr_ref, o_ref, do_ref, lse_ref, cs_ref), (dq_ref, dkv_ref, dkr_ref),
         (dk_acc, dv_acc, s_scr, dp_scr, p_scr, ds_scr, lse_scr, dl_scr, dob_scr, dq_scr), cps) = _split_refs(
             refs, 7, 3, comm)
        h = pl.program_id(0)
        qi = pl.program_id(1)
        _side_start(cps, jnp.logical_and(h == 0, qi == 0))

        @pl.when(qi == 0)
        def _():
            dk_acc[...] = jnp.zeros_like(dk_acc)
            dv_acc[...] = jnp.zeros_like(dv_acc)

        @pl.when(jnp.logical_and(h == 0, qi == 0))
        def _():
            dkr_ref[...] = jnp.zeros_like(dkr_ref)

        dov = do_ref[...]
        dl_scr[...] = jnp.broadcast_to(jnp.sum(dov * o_ref[...], axis=1, keepdims=True), (tq, 128))
        lse_scr[...] = lse_ref[...] * LOG2E
        dob_scr[...] = dov.astype(BF)
        dq_scr[...] = jnp.zeros_like(dq_scr)

        def step(kj, masked):
            k, v, off = _load_kv(kv_ref, kr_ref, kj, tk)
            s_scr[...] = _dot(q_ref[...], k, NT)
            dp_scr[...] = _dot(dob_scr[...], v, NT)
            for r in range(tq // SL):
                rows = pl.ds(r * SL, SL)
                s = s_scr[rows, :]
                if masked:
                    rws = qi * tq + r * SL + lax.broadcasted_iota(jnp.int32, (SL, tk), 0)
                    cls = kj * tk + lax.broadcasted_iota(jnp.int32, (SL, tk), 1)
                    s = jnp.where((cls >> CHUNK_SHIFT) <= (rws >> CHUNK_SHIFT), s, -jnp.inf)
                p = jnp.exp2(s * EXP2_SCALE - jnp.tile(lse_scr[rows, :], (1, rep)))
                p_scr[rows, :] = p.astype(BF)
                ds_scr[rows, :] = (p * (dp_scr[rows, :] - jnp.tile(dl_scr[rows, :], (1, rep)))).astype(BF)
            dv_acc[pl.ds(off, tk), :] += _dot_tn(p_scr[...], dob_scr[...])
            dk_acc[pl.ds(off, tk), :] += _dot_tn(ds_scr[...], q_ref[...])
            k2, _, _ = _load_kv(kv_ref, kr_ref, kj, tk)
            dq_scr[...] += _dot(ds_scr[...], k2, NN)

        def trip(kj, c):
            step(kj, False)
            return c

        lax.fori_loop(0, qi * nd, trip, 0)
        for d in range(nd):
            step(qi * nd + d, True)
        dq = dq_scr[...] * SCALE
        hi = dq[:, 128:256]
        hi = (hi + pltpu.roll(hi, 64, 1)) * cs_ref[...]
        dq_ref[...] = jnp.concatenate([dq[:, 0:128], hi], axis=1).astype(BF)

        @pl.when(qi == nq - 1)
        def _():
            dkv_ref[...] = jnp.concatenate([dk_acc[:, 0:128] * SCALE, dv_acc[...]], axis=1).astype(BF)
            dkr_ref[...] += dk_acc[:, 128:256] * SCALE

        _side_wait(cps, jnp.logical_and(h == H - 1, qi == nq - 1))

    res = pl.pallas_call(
        body, name="attn_bwd" if comm is None else "attn_bwd_comm", grid=(H, nq),
        in_specs=[pl.BlockSpec((None, tq, 256), lambda h, i: (h, i, 0)),
                  pl.BlockSpec((None, S, 256), lambda h, i: (h, 0, 0)),
                  pl.BlockSpec((S, 128), lambda h, i: (0, 0)),
                  pl.BlockSpec((tq, 128), lambda h, i: (i, h)),
                  pl.BlockSpec((tq, 128), lambda h, i: (i, h)),
                  pl.BlockSpec((None, tq, 128), lambda h, i: (h, i, 0)),
                  pl.BlockSpec((tq, 128), lambda h, i: (i, 0))] + [ANY_SPEC] * nc,
        out_specs=[pl.BlockSpec((None, tq, 256), lambda h, i: (h, i, 0)),
                   pl.BlockSpec((None, S, 256), lambda h, i: (h, 0, 0)),
                   pl.BlockSpec((S, 128), lambda h, i: (0, 0))] + [ANY_SPEC] * nc,
        out_shape=[_sds((H, S, 256), BF), _sds((H, S, 256), BF), _sds((S, 128), F32)]
        + (comm.out_shapes() if comm is not None else []),
        scratch_shapes=[pltpu.VMEM((S, 256), F32), pltpu.VMEM((S, 128), F32),
                        pltpu.VMEM((tq, tk), F32), pltpu.VMEM((tq, tk), F32),
                        pltpu.VMEM((tq, tk), BF), pltpu.VMEM((tq, tk), BF),
                        pltpu.VMEM((tq, 128), F32), pltpu.VMEM((tq, 128), F32),
                        pltpu.VMEM((tq, 128), BF), pltpu.VMEM((tq, 256), F32)]
        + (comm.scratch() if comm is not None else []),
        compiler_params=_params(("arbitrary", "arbitrary")),
    )(qf, kv_ext, kr, o, do, lse, cs, *(comm.arrays if comm is not None else []))
    return res[:3], res[3:]


def _log1p(y):
    w = 1.0 + y
    return jnp.where(w == 1.0, y, jnp.log(w) * (y / (w - 1.0)))


def _expm1(x):
    u = jnp.exp(x)
    return jnp.where(u == 1.0, x, (u - 1.0) * (x / jnp.log(u)))


def _softplus(x):
    return jnp.maximum(x, 0.0) + _log1p(jnp.exp(-jnp.abs(x)))


def _blockdiag(xb, w_ref):
    return jnp.concatenate(
        [_dot(xb[:, k * BD:(k + 1) * BD], w_ref[k], NN) for k in range(NBLK)], axis=1)


def _gates(xc, wr_ref, br_ref, wi_ref, bi_ref, sp):
    xb = xc.astype(BF)
    r = _sig(_blockdiag(xb, wr_ref) + br_ref[...])
    ig = _sig(_blockdiag(xb, wi_ref) + bi_ref[...])
    log_a = (-LRU_C * r) * sp
    mult = jnp.sqrt(-_expm1(2.0 * log_a))
    return xb, r, ig, log_a, mult


def _prev8_spec(tr, w, cb):
    return pl.BlockSpec((8, w), lambda i: (jnp.maximum(i * (tr // 8) - 1, 0), cb))


def lru_fwd(z, conv_w, conv_b, w_rg, b_rg, w_ig, b_ig, lam, tr):
    S = z.shape[0]
    W = D

    def body(u_ref, up_ref, cw_ref, cb_ref, wr_ref, br_ref, wi_ref, bi_ref, lam_ref,
             xc_ref, a_ref, h_ref, buf, bt, hcar):
        i = pl.program_id(0)

        @pl.when(i == 0)
        def _():
            buf[0:8, :] = jnp.zeros((8, W), F32)
            hcar[...] = jnp.zeros_like(hcar)

        @pl.when(i > 0)
        def _():
            buf[0:8, :] = up_ref[...]

        buf[8:8 + tr, :] = u_ref[...]
        cw = cw_ref[...]
        xc = buf[pl.ds(5, tr), :] * cw[0:1, :]
        for kk in range(1, 4):
            xc = xc + buf[pl.ds(5 + kk, tr), :] * cw[kk:kk + 1, :]
        xc = xc + cb_ref[...]
        xc_ref[...] = xc
        sp = _softplus(-lam_ref[...])
        _, _, ig, log_a, mult = _gates(xc, wr_ref, br_ref, wi_ref, bi_ref, sp)
        a_ref[...] = jnp.exp(log_a)
        bt[...] = mult * (ig * xc)
        row = lax.broadcasted_iota(jnp.int32, (8, W), 0)

        def grp(g, hp):
            off = pl.multiple_of(g * 8, 8)
            A = a_ref[pl.ds(off, 8), :]
            B = bt[pl.ds(off, 8), :]
            for d in (1, 2, 4):
                ok = row >= d
                B = jnp.where(ok, A * pltpu.roll(B, d, 0) + B, B)
                A = jnp.where(ok, A * pltpu.roll(A, d, 0), A)
            hh = A * hp + B
            h_ref[pl.ds(off, 8), :] = hh
            return hh[7:8, :]

        hcar[...] = lax.fori_loop(0, tr // 8, grp, hcar[...])

    return _rowcall(
        "lru_fwd", body, S, tr, [z, z, conv_w, conv_b, w_rg, b_rg, w_ig, b_ig, lam],
        [_rs(tr, W, 1), _prev8_spec(tr, W, 1), _fs((4, W)), _fs((1, W)), _fs((NBLK, BD, BD)), _fs((1, W)),
         _fs((NBLK, BD, BD)), _fs((1, W)), _fs((1, W))],
        (_sds((S, W), F32),) * 3, (_rs(tr, W),) * 3,
        scratch=[pltpu.VMEM((tr + 8, W), F32), pltpu.VMEM((tr, W), F32), pltpu.VMEM((1, W), F32)])


def lru_bwd(dh, a, hs, xc, w_rg, b_rg, w_ig, b_ig, lam, tr):
    S, W = dh.shape
    n = S // tr
    nb8 = S // 8

    def rev(i):
        return n - 1 - i

    row_spec = pl.BlockSpec((tr, W), lambda i: (rev(i), 0))
    next8 = pl.BlockSpec((8, W), lambda i: (jnp.minimum((rev(i) + 1) * (tr // 8), nb8 - 1), 0))
    prev8 = pl.BlockSpec((8, W), lambda i: (jnp.maximum(rev(i) * (tr // 8) - 1, 0), 0))

    def body(dh_ref, a_ref, an_ref, h_ref, hp_ref, xc_ref, wr_ref, br_ref, wi_ref, bi_ref, lam_ref,
             dxc_ref, dwr_ref, dwi_ref, dbr_ref, dbi_ref, dlam_ref, dcb_ref,
             bufa, bufh, apr, gsc, gcar, acc_br, acc_bi, acc_sp, acc_cb):
        i = pl.program_id(0)
        first = i == 0
        last_tile = i == n - 1

        bufa[0:tr, :] = a_ref[...]
        bufa[tr:tr + 8, :] = an_ref[...]
        apr[...] = bufa[pl.ds(1, tr), :]
        bufh[8:8 + tr, :] = h_ref[...]

        @pl.when(last_tile)
        def _():
            bufh[0:8, :] = jnp.zeros((8, W), F32)

        @pl.when(jnp.logical_not(last_tile))
        def _():
            bufh[0:8, :] = hp_ref[...]

        @pl.when(first)
        def _():
            gcar[...] = jnp.zeros_like(gcar)

        row = lax.broadcasted_iota(jnp.int32, (8, W), 0)
        ng = tr // 8

        def grp(t, gn):
            g = ng - 1 - t
            off = pl.multiple_of(g * 8, 8)
            A = apr[pl.ds(off, 8), :]
            B = dh_ref[pl.ds(off, 8), :]
            for d in (1, 2, 4):
                ok = row < 8 - d
                B = jnp.where(ok, B + A * pltpu.roll(B, 8 - d, 0), B)
                A = jnp.where(ok, A * pltpu.roll(A, 8 - d, 0), A)
            gg = B + A * gn
            gsc[pl.ds(off, 8), :] = gg
            return gg[0:1, :]

        gcar[...] = lax.fori_loop(0, ng, grp, gcar[...])

        G = gsc[...]
        hprev = bufh[pl.ds(7, tr), :]
        xcv = xc_ref[...]
        lam_v = lam_ref[...]
        sp = _softplus(-lam_v)
        xb, r, ig, log_a, mult = _gates(xcv, wr_ref, br_ref, wi_ref, bi_ref, sp)
        av = a_ref[...]
        d_a = G * hprev
        d_mult = G * (ig * xcv)
        d_ig = G * (mult * xcv)
        d_log_a = d_a * av - d_mult * (av * av / mult)
        d_gr = (d_log_a * (-LRU_C * sp)) * (r * (1.0 - r))
        d_gi = d_ig * (ig * (1.0 - ig))
        gr_b = d_gr.astype(BF)
        gi_b = d_gi.astype(BF)
        dxc = G * (mult * ig)
        dxc = dxc + jnp.concatenate(
            [_dot(gr_b[:, k * BD:(k + 1) * BD], wr_ref[k], NT)
             + _dot(gi_b[:, k * BD:(k + 1) * BD], wi_ref[k], NT) for k in range(NBLK)], axis=1)
        dxc_ref[...] = dxc

        @pl.when(first)
        def _():
            dwr_ref[...] = jnp.zeros_like(dwr_ref)
            dwi_ref[...] = jnp.zeros_like(dwi_ref)

        for k in range(NBLK):
            sl = slice(k * BD, (k + 1) * BD)
            dwr_ref[k] += _dot_tn(xb[:, sl], gr_b[:, sl])
            dwi_ref[k] += _dot_tn(xb[:, sl], gi_b[:, sl])

        _acc_rows(i, n, acc_br, dbr_ref, _fold8(d_gr))
        _acc_rows(i, n, acc_bi, dbi_ref, _fold8(d_gi))
        _acc_rows(i, n, acc_cb, dcb_ref, _fold8(dxc))

        part = _fold8(d_log_a * (-LRU_C * r))

        @pl.when(first)
        def _():
            acc_sp[...] = part

        @pl.when(jnp.logical_not(first))
        def _():
            acc_sp[...] += part

        @pl.when(last_tile)
        def _():
            dsp = jnp.sum(acc_sp[...], axis=0, keepdims=True)
            dlam_ref[...] = dsp * (-_sig(-lam_v))

    vec = _sds((1, W), F32)
    wsh = _sds((NBLK, BD, BD), F32)
    return _rowcall(
        "lru_bwd", body, S, tr, [dh, a, a, hs, hs, xc, w_rg, b_rg, w_ig, b_ig, lam],
        [row_spec, row_spec, next8, row_spec, prev8, row_spec, _fs((NBLK, BD, BD)), _fs((1, W)),
         _fs((NBLK, BD, BD)), _fs((1, W)), _fs((1, W))],
        (_sds((S, W), F32), wsh, wsh, vec, vec, vec, vec),
        (row_spec, _fs((NBLK, BD, BD)), _fs((NBLK, BD, BD)), _fs((1, W)), _fs((1, W)), _fs((1, W)), _fs((1, W))),
        scratch=[pltpu.VMEM((tr + 8, W), F32), pltpu.VMEM((tr + 8, W), F32), pltpu.VMEM((tr, W), F32),
                 pltpu.VMEM((tr, W), F32), pltpu.VMEM((1, W), F32), pltpu.VMEM((8, W), F32),
                 pltpu.VMEM((8, W), F32), pltpu.VMEM((8, W), F32), pltpu.VMEM((8, W), F32)])


def conv_bwd(dxc, z, conv_w, tr):
    S, W = dxc.shape
    n = S // tr
    nb8 = S // 8

    def body(d_ref, dn_ref, u_ref, up_ref, cw_ref, du_ref, dcw_ref, bufd, bufu, acc):
        i = pl.program_id(0)
        bufd[0:tr, :] = d_ref[...]

        @pl.when(i == n - 1)
        def _():
            bufd[tr:tr + 8, :] = jnp.zeros((8, W), F32)

        @pl.when(i < n - 1)
        def _():
            bufd[tr:tr + 8, :] = dn_ref[...]

        @pl.when(i == 0)
        def _():
            bufu[0:8, :] = jnp.zeros((8, W), F32)

        @pl.when(i > 0)
        def _():
            bufu[0:8, :] = up_ref[...]

        bufu[8:8 + tr, :] = u_ref[...]
        cw = cw_ref[...]
        dv = d_ref[...]
        du = dv * cw[3:4, :]
        for j in range(1, 4):
            du = du + bufd[pl.ds(j, tr), :] * cw[3 - j:4 - j, :]
        du_ref[...] = du.astype(BF)
        parts = [jnp.sum(_fold8(dv * bufu[pl.ds(5 + kk, tr), :]), axis=0, keepdims=True) for kk in range(4)]
        part = jnp.concatenate(parts, axis=0)

        @pl.when(i == 0)
        def _():
            acc[...] = part

        @pl.when(i > 0)
        def _():
            acc[...] += part

        @pl.when(i == n - 1)
        def _():
            dcw_ref[...] = acc[...]

    next8 = pl.BlockSpec((8, W), lambda i: (jnp.minimum((i + 1) * (tr // 8), nb8 - 1), 0))
    return _rowcall(
        "conv_bwd", body, S, tr, [dxc, dxc, z, z, conv_w],
        [_rs(tr, W), next8, _rs(tr, W, 1), _prev8_spec(tr, W, 1), _fs((4, W))],
        (_sds((S, W), BF), _sds((4, W), F32)), (_rs(tr, W), _fs((4, W))),
        scratch=[pltpu.VMEM((tr + 8, W), F32), pltpu.VMEM((tr + 8, W), F32), pltpu.VMEM((4, W), F32)])


def _me():
    x = lax.axis_index("x")
    y = lax.axis_index("y")
    c = lax.axis_index("c")
    return x, y, c


def _peer(r):
    x, y, c = _me()
    px = jnp.bitwise_xor(x, (r >> 2) & 1)
    py = jnp.bitwise_xor(y, (r >> 1) & 1)
    pc = jnp.bitwise_xor(c, r & 1)
    return (px, py, pc), 4 * px + 2 * py + pc


class Comm:
    def __init__(self, arrays, modes):
        self.arrays = list(arrays)
        self.modes = list(modes)
        self.n = len(self.arrays)

    def out_shapes(self):
        return [_sds(((NDEV,) + a.shape) if md == "gather" else a.shape, a.dtype)
                for a, md in zip(self.arrays, self.modes)]

    def scratch(self):
        return [pltpu.SemaphoreType.DMA((self.n * (NDEV - 1),)),
                pltpu.SemaphoreType.DMA((self.n * (NDEV - 1),)),
                pltpu.SemaphoreType.DMA((self.n,))]

    def copies(self, ins, outs, send_sems, recv_sems, loc_sems):
        x, y, c = _me()
        me = 4 * x + 2 * y + c
        cps = []
        for ai, md in enumerate(self.modes):
            src = ins[ai] if md == "gather" else ins[ai].at[me]
            cps.append(pltpu.make_async_copy(src, outs[ai].at[me], loc_sems.at[ai]))
        for r in range(1, NDEV):
            dev, idx = _peer(r)
            for ai, md in enumerate(self.modes):
                src = ins[ai] if md == "gather" else ins[ai].at[idx]
                k = ai * (NDEV - 1) + r - 1
                cps.append(pltpu.make_async_remote_copy(
                    src_ref=src, dst_ref=outs[ai].at[me], send_sem=send_sems.at[k], recv_sem=recv_sems.at[k],
                    device_id=dev, device_id_type=pl.DeviceIdType.MESH))
        return cps


ANY_SPEC = pl.BlockSpec(memory_space=pl.ANY)


def comm_call(name, comm):
    na = comm.n

    def body(*refs):
        cps = comm.copies(refs[:na], refs[na:2 * na], *refs[2 * na:])
        for cp in cps:
            cp.start()
        for cp in cps:
            cp.wait()

    return pl.pallas_call(
        body, name=name, in_specs=[ANY_SPEC] * na, out_specs=[ANY_SPEC] * na, out_shape=comm.out_shapes(),
        scratch_shapes=comm.scratch(), compiler_params=pltpu.CompilerParams(has_side_effects=True),
    )(*comm.arrays)


def adamw(name, parts, w, m, v, tr):
    L = len(parts)
    _, R, C = parts[0].shape
    tr = _tile(R, tr)

    def body(*refs):
        p_refs = refs[:L]
        w_ref, m_ref, v_ref, g_ref, d_ref, nm_ref, nv_ref = refs[L:]
        for l in range(L):
            g = p_refs[l][0].astype(F32)
            for k in range(1, NDEV):
                g = g + p_refs[l][k].astype(F32)
            g_ref[l] = g
            mn = ADAM_B1 * m_ref[l] + (1.0 - ADAM_B1) * g
            vn = ADAM_B2 * v_ref[l] + (1.0 - ADAM_B2) * (g * g)
            m_hat = mn / (1.0 - ADAM_B1 ** ADAM_STEP)
            v_hat = vn / (1.0 - ADAM_B2 ** ADAM_STEP)
            d_ref[l] = -ADAM_LR * (m_hat / (jnp.sqrt(v_hat) + ADAM_EPS) + ADAM_WD * w_ref[l])
            nm_ref[l] = mn
            nv_ref[l] = vn

    blk = pl.BlockSpec((L, tr, C), lambda i: (0, i, 0))
    pblk = pl.BlockSpec((NDEV, tr, C), lambda i: (0, i, 0))
    return pl.pallas_call(
        body, name=name, grid=(R // tr,), in_specs=[pblk] * L + [blk, blk, blk],
        out_specs=(blk,) * 4, out_shape=(_sds((L, R, C), F32),) * 4,
        compiler_params=_params(("parallel",)),
    )(*parts, w, m, v)


def _rot(w):
    h = w.shape[-1] // 2
    return jnp.concatenate([-w[..., h:], w[..., :h]], axis=-1)


def _unrot(dw):
    h = dw.shape[-1] // 2
    return jnp.concatenate([dw[..., h:], -dw[..., :h]], axis=-1)


def _cols(g):
    n, R, C = g.shape
    return g.transpose(1, 0, 2).reshape(R, n * C)


def _rows(g):
    n, R, C = g.shape
    return g.reshape(n * R, C)


def _split_cols(dw):
    R, NC = dw.shape
    return dw.reshape(R, NDEV, NC // NDEV).transpose(1, 0, 2)


REPL = ["attn_norm", "q_a_norm", "kv_a_norm", "conv_b", "w_rg", "b_rg", "w_ig", "b_ig", "lru_lambda",
        "ple_norm", "final_norm"]
SHARDED = ["w_in", "w_q_b", "w_kv_b", "conv_w", "w_o_mla", "w_o_lru", "w_out", "w_ple_gate", "w_ple"]
WEIGHTS = ["attn_norm", "w_in", "q_a_norm", "w_q_b", "kv_a_norm", "w_kv_b", "conv_w", "conv_b", "w_rg", "b_rg",
           "w_ig", "b_ig", "lru_lambda", "w_o_mla", "w_o_lru", "w_out", "ple_norm", "w_ple_gate", "w_ple",
           "final_norm"]


REPL_LAYER = REPL[1:-1]


def _pack(vals):
    flat = jnp.concatenate([v.reshape(-1) for v in vals])
    n = flat.shape[0]
    rows = -(-n // (128 * 256)) * 256
    return jnp.pad(flat, (0, rows * 128 - n)).reshape(rows, 128)


def _unpack(packed, shapes):
    flat = packed.reshape(-1)
    out = []
    off = 0
    for s in shapes:
        n = int(np.prod(s))
        out.append(flat[off:off + n].reshape(s))
        off += n
    return out


def kernel(x, p, positions, attn_norm, w_in, q_a_norm, w_q_b, kv_a_norm, w_kv_b, conv_w, conv_b, w_rg, b_rg, w_ig, b_ig, lru_lambda, w_o_mla, w_o_lru, w_out, ple_norm, w_ple_gate, w_ple, final_norm, loss_target, m_attn_norm, m_w_in, m_q_a_norm, m_w_q_b, m_kv_a_norm, m_w_kv_b, m_conv_w, m_conv_b, m_w_rg, m_b_rg, m_w_ig, m_b_ig, m_lru_lambda, m_w_o_mla, m_w_o_lru, m_w_out, m_ple_norm, m_w_ple_gate, m_w_ple, m_final_norm, v_attn_norm, v_w_in, v_q_a_norm, v_w_q_b, v_kv_a_norm, v_w_kv_b, v_conv_w, v_conv_b, v_w_rg, v_b_rg, v_w_ig, v_b_ig, v_lru_lambda, v_w_o_mla, v_w_o_lru, v_w_out, v_ple_norm, v_w_ple_gate, v_w_ple, v_final_norm):
    W = dict(attn_norm=attn_norm, w_in=w_in, q_a_norm=q_a_norm, w_q_b=w_q_b, kv_a_norm=kv_a_norm, w_kv_b=w_kv_b,
             conv_w=conv_w, conv_b=conv_b, w_rg=w_rg, b_rg=b_rg, w_ig=w_ig, b_ig=b_ig, lru_lambda=lru_lambda,
             w_o_mla=w_o_mla, w_o_lru=w_o_lru, w_out=w_out, ple_norm=ple_norm, w_ple_gate=w_ple_gate, w_ple=w_ple,
             final_norm=final_norm)
    M = dict(attn_norm=m_attn_norm, w_in=m_w_in, q_a_norm=m_q_a_norm, w_q_b=m_w_q_b, kv_a_norm=m_kv_a_norm,
             w_kv_b=m_w_kv_b, conv_w=m_conv_w, conv_b=m_conv_b, w_rg=m_w_rg, b_rg=m_b_rg, w_ig=m_w_ig, b_ig=m_b_ig,
             lru_lambda=m_lru_lambda, w_o_mla=m_w_o_mla, w_o_lru=m_w_o_lru, w_out=m_w_out, ple_norm=m_ple_norm,
             w_ple_gate=m_w_ple_gate, w_ple=m_w_ple, final_norm=m_final_norm)
    V = dict(attn_norm=v_attn_norm, w_in=v_w_in, q_a_norm=v_q_a_norm, w_q_b=v_w_q_b, kv_a_norm=v_kv_a_norm,
             w_kv_b=v_w_kv_b, conv_w=v_conv_w, conv_b=v_conv_b, w_rg=v_w_rg, b_rg=v_b_rg, w_ig=v_w_ig, b_ig=v_b_ig,
             lru_lambda=v_lru_lambda, w_o_mla=v_w_o_mla, w_o_lru=v_w_o_lru, w_out=v_w_out, ple_norm=v_ple_norm,
             w_ple_gate=v_w_ple_gate, w_ple=v_w_ple, final_norm=v_final_norm)

    L = w_in.shape[0]
    S = x.shape[1]
    xs = x[0]
    tgt = loss_target[0]
    tr = _tile(S, max(8, min(256, S // 2)))
    tq = _tile(S, max(256, min(1024, S // 2)))
    tk = max(128, min(512, tq // 2))

    inv_freq = ROPE_THETA ** (-jnp.arange(0, ROPE, 2, dtype=F32) / ROPE)
    ang = positions[0].astype(F32)[:, None] * inv_freq
    cs = jnp.concatenate([jnp.cos(ang), jnp.cos(ang), jnp.sin(ang), jnp.sin(ang)], axis=1)

    GATHERED = ["w_in", "w_q_b", "w_kv_b", "w_o_mla", "w_o_lru", "w_out", "w_ple_gate", "w_ple"]

    def gather_set(l, first=0):
        arrs = [W[k][l].astype(BF) for k in GATHERED] + [conv_w[l]]
        return Comm(arrs[first:], ["gather"] * (len(arrs) - first))

    def prep_win(g_in):
        win = _cols(g_in)
        kr_w = win[:, 1024:1088]
        return jnp.concatenate([win[:, 1088:], win[:, 0:1024], kr_w, _rot(kr_w),
                                jnp.zeros((D, ZW - Z_KR - 128), BF)], axis=-1)

    def prep(g):
        g_qb, g_kvb, g_om, g_ol, g_out, g_pg, g_ple, g_cw = g[-8:]
        wq = _cols(g_qb).reshape(QL, NH, 192)
        d = dict(wq=jnp.concatenate([wq, _rot(wq[..., 128:])], axis=-1).reshape(QL, NH * 256),
                 wkv=_cols(g_kvb), wom=_rows(g_om), wol=_rows(g_ol), wout=_rows(g_out), wpg=_rows(g_pg),
                 wple=_cols(g_ple), cw=_cols(g_cw))
        if len(g) == 9:
            d["win"] = prep_win(g[0])
        return d

    wts = [None] * L
    win0 = prep_win(comm_call("gather_win0", Comm([w_in[0].astype(BF)], ["gather"]))[0])
    wrg_b = w_rg.astype(BF)
    wig_b = w_ig.astype(BF)

    def vec(a, l):
        return a[l][None, :]

    saved = []
    xcur = xs
    for l in range(L):
        h = norm_fwd("norm_in", xcur, vec(attn_norm, l), tr)
        if l == 0:
            z, got = mm("mm_in_comm", h, win0, "nn", tn=1280, tk=2048, comm=gather_set(0, first=1))
            wts[0] = dict(prep(got), win=win0)
        else:
            z = mm("mm_in", h, wts[l]["win"], "nn", tn=1280, tk=2048)
        wl = wts[l]
        qn, kvn, kr = qkv_prep(z, vec(q_a_norm, l), vec(kv_a_norm, l), cs, tr)
        q_ext = mm("mm_q", qn, wl["wq"], "nn", o_heads=True, tk=512)
        kv_ext = mm("mm_kv", kvn, wl["wkv"], "nn", o_heads=True, tk=512, out_dtype=BF)
        (o, lse, qf), got = attn_fwd(q_ext, kv_ext, kr, cs, tq, tk, gather_set(l + 1) if l + 1 < L else None)
        if l + 1 < L:
            wts[l + 1] = prep(got)
        xc, a, hs = lru_fwd(z, wl["cw"], vec(conv_b, l), wrg_b[l], vec(b_rg, l), wig_b[l], vec(b_ig, l),
                            vec(lru_lambda, l), tr)
        om, hl = gate_out(o, z, hs, tr)
        ym = mm("mm_om", om, wl["wom"], "nn", tk=2048)
        yl = mm("mm_ol", hl, wl["wol"], "nn", tk=2048)
        mg = merge_fwd(ym, yl, z, tr)
        x1 = mm("mm_out", mg, wl["wout"], "nn", tk=2048, res=xcur)
        hp = norm_fwd("norm_ple", x1, vec(ple_norm, l), tr)
        gp = mm("mm_pg", hp, wl["wpg"], "nn", tk=2048)
        pe = mm("mm_ple", p[l, 0], wl["wple"], "nn")
        x2 = ple_fin(x1, pe, gp, tr)
        saved.append(dict(x=xcur, h=h, z=z, qn=qn, kvn=kvn, kr=kr, kv_ext=kv_ext, o=o, lse=lse, qf=qf, xc=xc, a=a,
                          hs=hs, om=om, hl=hl, ym=ym, yl=yl, mg=mg, x1=x1, hp=hp, gp=gp, pe=pe))
        xcur = x2

    lsum, dx, d_final = final_loss(xcur, final_norm[None, :], tgt, tr)
    loss = lax.psum(0.5 * lsum[0, 0] / D, MESH_AXES)

    gr = {k: [None] * L for k in WEIGHTS if k != "final_norm"}
    landed_early = [None] * L
    landed_late = [None] * L
    zero_fn = jnp.zeros_like(final_norm)

    def early_grads(l):
        return ([gr[k][l].reshape(NDEV, D // NDEV, D) for k in ("w_o_mla", "w_o_lru", "w_out", "w_ple_gate")]
                + [_split_cols(gr["w_ple"][l])], ["a2a"] * 5)

    def late_grads(l):
        rep = _pack([gr[k][l] for k in REPL_LAYER] + [d_final[0] if l == L - 1 else zero_fn])
        return ([_split_cols(gr[k][l]) for k in ("w_in", "w_q_b", "w_kv_b", "conv_w")] + [rep],
                ["a2a"] * 4 + ["gather"])

    for l in reversed(range(L)):
        sv = saved[l]
        wl = wts[l]
        z = sv["z"]
        dpe, dgp = ple_bwd(dx, sv["pe"], sv["gp"], tr)
        gr["w_ple"][l] = mm("mm_dple", p[l, 0], dpe, "tn", tm=256, out_dtype=BF)
        gr["w_ple_gate"][l] = mm("mm_dpg", sv["hp"], dgp, "tn", out_dtype=BF)
        dhp = mm("mm_dhp", dgp, wl["wpg"], "nt", tk=2048)
        dx1, gr["ple_norm"][l] = norm_bwd("norm_ple_bwd", sv["x1"], dhp, vec(ple_norm, l), dx, tr)
        dmg = mm("mm_dmg", dx1, wl["wout"], "nt", tk=2048)
        gr["w_out"][l] = mm("mm_dwout", sv["mg"], dx1, "tn", out_dtype=BF)
        dym, dyl, dz_m = merge_bwd(dmg, sv["ym"], sv["yl"], z, tr)
        dom = mm("mm_dom", dym, wl["wom"], "nt", tk=2048)
        gr["w_o_mla"][l] = mm("mm_dwom", sv["om"], dym, "tn", out_dtype=BF)
        dhl = mm("mm_dhl", dyl, wl["wol"], "nt", tk=2048)
        gr["w_o_lru"][l] = mm("mm_dwol", sv["hl"], dyl, "tn", out_dtype=BF)
        do, dz_gm, dhs, dz_gl = gate_bwd(dom, sv["o"], dhl, sv["hs"], z, tr)
        arrs, modes = early_grads(l)
        if l + 1 < L:
            arrs2, modes2 = late_grads(l + 1)
            arrs, modes = arrs + arrs2, modes + modes2
        (dq_ext, dkv_ext, dkr), got = attn_bwd(sv["qf"], sv["kv_ext"], sv["kr"], sv["o"], do, sv["lse"], cs, tq, tk,
                                               Comm(arrs, modes))
        landed_early[l] = got[:5]
        if l + 1 < L:
            landed_late[l + 1] = got[5:]
        dqn = mm("mm_dqn", dq_ext, wl["wq"], "nt", a_heads=True, tn=512)
        dwq_ext = mm("mm_dwq", sv["qn"], dq_ext, "tn", b_heads=True, tm=512, out_dtype=BF)
        dkn = mm("mm_dkn", dkv_ext, wl["wkv"], "nt", a_heads=True, tn=512)
        gr["w_kv_b"][l] = mm("mm_dwkv", sv["kvn"], dkv_ext, "tn", b_heads=True, tm=512, out_dtype=BF)
        dz_t, gr["q_a_norm"][l], gr["kv_a_norm"][l] = qkv_prep_bwd(
            z, dqn, dkn, dkr, vec(q_a_norm, l), vec(kv_a_norm, l), cs, tr)
        dwq4 = dwq_ext.reshape(QL, NH, 256)
        gr["w_q_b"][l] = jnp.concatenate(
            [dwq4[..., 0:128], dwq4[..., 128:192] + _unrot(dwq4[..., 192:256])], axis=-1).reshape(QL, NH * 192)
        (dxc, gr["w_rg"][l], gr["w_ig"][l], gr["b_rg"][l], gr["b_ig"][l], gr["lru_lambda"][l],
         gr["conv_b"][l]) = lru_bwd(dhs, sv["a"], sv["hs"], sv["xc"], wrg_b[l], vec(b_rg, l), wig_b[l],
                                    vec(b_ig, l), vec(lru_lambda, l), tr)
        dz_u, gr["conv_w"][l] = conv_bwd(dxc, z, wl["cw"], tr)
        dz = jnp.concatenate([dz_gm, dz_u, dz_gl, dz_m, dz_t], axis=1)
        dwin_ext = mm("mm_dwin", sv["h"], dz, "tn", tn=1280, out_dtype=BF)
        gr["w_in"][l] = jnp.concatenate(
            [dwin_ext[:, Z_Q:Z_KR], dwin_ext[:, Z_KR:Z_KR + 64] + _unrot(dwin_ext[:, Z_KR + 64:Z_KR + 128]),
             dwin_ext[:, 0:Z_Q]], axis=1)
        if l == 0:
            dh, landed_late[0] = mm("mm_dh_comm", dz, wl["win"], "nt", tk=1280, comm=Comm(*late_grads(0)))
        else:
            dh = mm("mm_dh", dz, wl["win"], "nt", tk=1280)
        dx, gr["attn_norm"][l] = norm_bwd("norm_in_bwd", sv["x"], dh, vec(attn_norm, l), dx1, tr)
    grad_x = dx[None]
    (landed_an,) = comm_call("gather_attn_norm", Comm([jnp.concatenate(gr["attn_norm"]).reshape(L * D // 128, 128)],
                                                      ["gather"]))

    res = {}
    for ki, k in enumerate(SHARDED):
        parts = [landed_late[l][ki] if ki < 4 else landed_early[l][ki - 4] for l in range(L)]
        res[k] = adamw("adamw_" + k, parts, W[k], M[k], V[k], 64)

    def rep_pack(T):
        fn = jnp.concatenate([jnp.zeros((L - 1, D), F32), T["final_norm"][None]], axis=0)
        flat = jnp.concatenate([T[k].reshape(L, -1) for k in REPL_LAYER] + [fn], axis=1)
        rows = landed_late[0][4].shape[1]
        return jnp.pad(flat, ((0, 0), (0, rows * 128 - flat.shape[1]))).reshape(L, rows, 128)

    rep = adamw("adamw_rep", [landed_late[l][4] for l in range(L)], rep_pack(W), rep_pack(M), rep_pack(V), 256)
    shapes = [W[k].shape[1:] for k in REPL_LAYER] + [final_norm.shape]
    per_layer = [[_unpack(t[l], shapes) for l in range(L)] for t in rep]
    for i, k in enumerate(REPL_LAYER):
        res[k] = tuple(jnp.stack([per_layer[t][l][i] for l in range(L)]) for t in range(4))
    res["final_norm"] = tuple(per_layer[t][L - 1][-1] for t in range(4))
    an = adamw("adamw_attn_norm", [landed_an], *(T["attn_norm"].reshape(1, L * D // 128, 128) for T in (W, M, V)), 64)
    res["attn_norm"] = tuple(t.reshape(L, D) for t in an)

    outs = [loss, grad_x]
    for t in range(4):
        outs += [res[k][t] for k in WEIGHTS]
    return tuple(outs)
```

```python
import functools
import math

import numpy as np
import jax
import jax.numpy as jnp
from jax import lax
from jax.experimental import pallas as pl
from jax.experimental.pallas import tpu as pltpu

F32 = jnp.float32
BF = jnp.bfloat16
MESH_AXES = ("x", "y", "c")
NDEV = 8

D = 2048
NH = 16
QL = 512
KVL = 512
ROPE = 64
PLE = 256
NBLK = 16
BD = 128
CHUNK_SHIFT = 6
EPS = 1e-6
LRU_C = 8.0
ROPE_THETA = 10000.0
IN_TOTAL = 11328
ZW = 11520
ZB_GM, ZB_GL, ZB_MM, ZB_ML, ZB_U = 0, 1, 2, 3, 4
Z_Q = 10240
Z_KV = 10752
Z_KR = 11264
SCALE = 1.0 / math.sqrt(128 + 64)
EXP2_SCALE = SCALE * math.log2(math.e)
LOG2E = math.log2(math.e)

ADAM_LR = 0.001
ADAM_B1 = 0.9
ADAM_B2 = 0.999
ADAM_EPS = 1e-08
ADAM_WD = 0.01
ADAM_STEP = 10

VMEM_LIMIT = 60 * 1024 * 1024

NN = (((1,), (0,)), ((), ()))
NT = (((1,), (1,)), ((), ()))


def _tile(n, pref):
    t = min(n, pref)
    while n % t:
        t //= 2
    return t


def _params(sem):
    return pltpu.CompilerParams(dimension_semantics=sem, vmem_limit_bytes=VMEM_LIMIT)


def _dot(a, b, dims=NN):
    return lax.dot_general(a, b, dims, preferred_element_type=F32)


def _dot_tn(a, b):
    return lax.dot_general(a.T, b, NN, preferred_element_type=F32)


def _bf(v):
    return v if v.dtype == BF else v.astype(BF)


def mm(name, a, b, mode, *, out_dtype=F32, tm=1024, tn=1024, tk=1024, res=None,
       a_heads=False, b_heads=False, o_heads=False, comm=None):
    if mode == "nn":
        M = a.shape[0]
        K = a.shape[1]
        N = b.shape[1]
    elif mode == "nt":
        M = a.shape[1] if a_heads else a.shape[0]
        K = a.shape[0] * a.shape[2] if a_heads else a.shape[1]
        N = b.shape[0]
    else:
        K = a.shape[0]
        M = a.shape[1]
        N = b.shape[0] * b.shape[2] if b_heads else b.shape[1]
    tm = _tile(M, tm)
    tn = _tile(N, tn)
    tk = _tile(K, tk)
    nk = K // tk
    grid = (M // tm, N // tn, nk)
    nc = comm.n if comm is not None else 0

    if mode == "nn":
        a_spec = pl.BlockSpec((tm, tk), lambda i, j, k: (i, k))
        b_spec = pl.BlockSpec((tk, tn), lambda i, j, k: (k, j))
    elif mode == "nt":
        if a_heads:
            a_spec = pl.BlockSpec((tk // 256, tm, 256), lambda i, j, k: (k, i, 0))
        else:
            a_spec = pl.BlockSpec((tm, tk), lambda i, j, k: (i, k))
        b_spec = pl.BlockSpec((tn, tk), lambda i, j, k: (j, k))
    else:
        a_spec = pl.BlockSpec((tk, tm), lambda i, j, k: (k, i))
        if b_heads:
            b_spec = pl.BlockSpec((tn // 256, tk, 256), lambda i, j, k: (j, k, 0))
        else:
            b_spec = pl.BlockSpec((tk, tn), lambda i, j, k: (k, j))
    if o_heads:
        o_spec = pl.BlockSpec((tn // 256, tm, 256), lambda i, j, k: (j, i, 0))
        o_shape = jax.ShapeDtypeStruct((N // 256, M, 256), out_dtype)
    else:
        o_spec = pl.BlockSpec((tm, tn), lambda i, j, k: (i, j))
        o_shape = jax.ShapeDtypeStruct((M, N), out_dtype)
    in_specs = [a_spec, b_spec]
    args = [a, b]
    if res is not None:
        in_specs.append(pl.BlockSpec((tm, tn), lambda i, j, k: (i, j)))
        args.append(res)
    n_in = len(args)

    def heads(ref):
        return jnp.concatenate([ref[e] for e in range(ref.shape[0])], axis=1)

    def body(*refs):
        ins, (o_ref,), scr, cps = _split_refs(refs, n_in, 1, comm)
        a_ref, b_ref = ins[0], ins[1]
        r_ref = ins[2] if res is not None else None
        ids = [pl.program_id(d) for d in range(3)]
        _side_start(cps, functools.reduce(jnp.logical_and, [ids[d] == 0 for d in range(3)]))
        av = _bf(heads(a_ref) if a_heads else a_ref[...])
        bv = _bf(heads(b_ref) if b_heads else b_ref[...])
        if mode == "nn":
            p = _dot(av, bv, NN)
        elif mode == "nt":
            p = _dot(av, bv, NT)
        else:
            p = _dot_tn(av, bv)

        def fin(acc):
            if r_ref is not None:
                acc = acc + r_ref[...]
            acc = acc.astype(o_ref.dtype)
            if o_heads:
                for e in range(tn // 256):
                    o_ref[e] = acc[:, e * 256:(e + 1) * 256]
            else:
                o_ref[...] = acc

        if nk == 1:
            fin(p)
        else:
            acc_ref = scr[0]
            k = ids[2]

            @pl.when(k == 0)
            def _():
                acc_ref[...] = p

            @pl.when(k > 0)
            def _():
                acc_ref[...] += p

            @pl.when(k == nk - 1)
            def _():
                fin(acc_ref[...])

        _side_wait(cps, functools.reduce(jnp.logical_and, [ids[d] == grid[d] - 1 for d in range(3)]))

    scratch = ([] if nk == 1 else [pltpu.VMEM((tm, tn), F32)]) + (comm.scratch() if comm is not None else [])
    sem = ("arbitrary",) * 3 if comm is not None else ("parallel", "parallel", "arbitrary")
    out = pl.pallas_call(
        body, name=name, grid=grid, in_specs=in_specs + [ANY_SPEC] * nc, out_specs=[o_spec] + [ANY_SPEC] * nc,
        out_shape=[o_shape] + (comm.out_shapes() if comm is not None else []),
        scratch_shapes=scratch, compiler_params=_params(sem),
    )(*args, *(comm.arrays if comm is not None else []))
    return out[0] if comm is None else (out[0], out[1:])


def _rs(tr, w, cb=0):
    return pl.BlockSpec((tr, w), lambda i: (i, cb))


def _fs(shape):
    nd = len(shape)
    return pl.BlockSpec(shape, lambda i: (0,) * nd)


def _rowcall(name, body, S, tr, ins, in_specs, outs, out_specs, scratch=(), aliases=None):
    return pl.pallas_call(
        body, name=name, grid=(S // tr,), in_specs=in_specs, out_specs=out_specs, out_shape=outs,
        scratch_shapes=list(scratch), compiler_params=_params(("arbitrary",)),
        input_output_aliases=aliases or {},
    )(*ins)


def _dz_spec(tr, w, off):
    return pl.BlockSpec((tr, w), lambda i: (i, off // w))


def _sds(shape, dt):
    return jax.ShapeDtypeStruct(shape, dt)


def _fold8(v):
    tr, w = v.shape
    return jnp.sum(v.reshape(tr // 8, 8, w), axis=0)


def _acc_rows(i, n, acc_ref, out_ref, part):
    @pl.when(i == 0)
    def _():
        acc_ref[...] = part

    @pl.when(i > 0)
    def _():
        acc_ref[...] += part

    @pl.when(i == n - 1)
    def _():
        out_ref[...] = jnp.sum(acc_ref[...], axis=0, keepdims=True)


def _rms_fwd(x, g):
    r = lax.rsqrt(jnp.mean(x * x, axis=-1, keepdims=True) + EPS)
    return x * r * g


def _rms_bwd(x, dy, g):
    r = lax.rsqrt(jnp.mean(x * x, axis=-1, keepdims=True) + EPS)
    xh = x * r
    dxh = dy * g
    dx = r * (dxh - xh * jnp.mean(dxh * xh, axis=-1, keepdims=True))
    return dx, dy * xh


def _sig(x):
    return jax.nn.sigmoid(x)


def norm_fwd(name, x, g, tr):
    S, W = x.shape

    def body(x_ref, g_ref, o_ref):
        o_ref[...] = _rms_fwd(x_ref[...], g_ref[...]).astype(BF)

    return _rowcall(name, body, S, tr, [x, g], [_rs(tr, W), _fs((1, W))], _sds((S, W), BF), _rs(tr, W))


def norm_bwd(name, x, dy, g, dres, tr):
    S, W = x.shape
    n = S // tr

    def body(x_ref, dy_ref, g_ref, dr_ref, dx_ref, dg_ref, acc_ref):
        i = pl.program_id(0)
        dx, dgp = _rms_bwd(x_ref[...], dy_ref[...], g_ref[...])
        dx_ref[...] = dr_ref[...] + dx
        _acc_rows(i, n, acc_ref, dg_ref, _fold8(dgp))

    return _rowcall(name, body, S, tr, [x, dy, g, dres],
                    [_rs(tr, W), _rs(tr, W), _fs((1, W)), _rs(tr, W)],
                    (_sds((S, W), F32), _sds((1, W), F32)), (_rs(tr, W), _fs((1, W))),
                    scratch=[pltpu.VMEM((8, W), F32)])


def _lane_lt(shape, n):
    return lax.broadcasted_iota(jnp.int32, shape, 1) < n


def qkv_prep(z, qg, kvg, cs, tr):
    S = z.shape[0]

    def body(ql_ref, kl_ref, zk_ref, qg_ref, kg_ref, cs_ref, qn_ref, kn_ref, kr_ref):
        qn_ref[...] = _rms_fwd(ql_ref[...], qg_ref[...]).astype(BF)
        kn_ref[...] = _rms_fwd(kl_ref[...], kg_ref[...]).astype(BF)
        t = zk_ref[...] * cs_ref[...]
        t = t + pltpu.roll(t, 64, 1)
        kr_ref[...] = jnp.where(_lane_lt(t.shape, 64), t, 0.0).astype(BF)

    return _rowcall(
        "qkv_prep", body, S, tr, [z, z, z, qg, kvg, cs],
        [_rs(tr, QL, Z_Q // QL), _rs(tr, KVL, Z_KV // KVL), _rs(tr, 128, Z_KR // 128),
         _fs((1, QL)), _fs((1, KVL)), _rs(tr, 128)],
        (_sds((S, QL), BF), _sds((S, KVL), BF), _sds((S, 128), BF)),
        (_rs(tr, QL), _rs(tr, KVL), _rs(tr, 128)))


def qkv_prep_bwd(z, dqn, dkn, dkr, qg, kvg, cs, dz, tr):
    S = z.shape[0]
    n = S // tr

    def body(ql_ref, kl_ref, dq_ref, dk_ref, dkr_ref, qg_ref, kg_ref, cs_ref, _,
             dz_ref, dqg_ref, dkg_ref, accq, acck):
        i = pl.program_id(0)
        dql, gq = _rms_bwd(ql_ref[...], dq_ref[...], qg_ref[...])
        dkl, gk = _rms_bwd(kl_ref[...], dk_ref[...], kg_ref[...])
        t = jnp.where(_lane_lt((tr, 128), 64), dkr_ref[...], 0.0)
        t = (t + pltpu.roll(t, 64, 1)) * cs_ref[...]
        dz_ref[...] = jnp.concatenate(
            [dql.astype(BF), dkl.astype(BF), t.astype(BF), jnp.zeros((tr, 128), BF)], axis=1)
        _acc_rows(i, n, accq, dqg_ref, _fold8(gq))
        _acc_rows(i, n, acck, dkg_ref, _fold8(gk))

    return _rowcall(
        "qkv_prep_bwd", body, S, tr, [z, z, dqn, dkn, dkr, qg, kvg, cs, dz],
        [_rs(tr, QL, Z_Q // QL), _rs(tr, KVL, Z_KV // KVL), _rs(tr, QL), _rs(tr, KVL), _rs(tr, 128),
         _fs((1, QL)), _fs((1, KVL)), _rs(tr, 128), ANY_SPEC],
        (_sds((S, ZW), BF), _sds((1, QL), F32), _sds((1, KVL), F32)),
        (_dz_spec(tr, ZW - Z_Q, Z_Q), _fs((1, QL)), _fs((1, KVL))),
        scratch=[pltpu.VMEM((8, QL), F32), pltpu.VMEM((8, KVL), F32)], aliases={8: 0})


def gate_out(o, z, hs, tr):
    S = o.shape[0]

    def body(o_ref, gm_ref, h_ref, gl_ref, om_ref, hl_ref):
        gm = gm_ref[...]
        gl = gl_ref[...]
        om_ref[...] = (o_ref[...] * (gm * _sig(gm))).astype(BF)
        hl_ref[...] = (h_ref[...] * (gl * _sig(gl))).astype(BF)

    return _rowcall("gate_out", body, S, tr, [o, z, hs, z],
                    [_rs(tr, D), _rs(tr, D, ZB_GM), _rs(tr, D), _rs(tr, D, ZB_GL)],
                    (_sds((S, D), BF), _sds((S, D), BF)), (_rs(tr, D), _rs(tr, D)))


def gate_bwd(dom, o, dhl, hs, z, dz, tr):
    S = o.shape[0]

    def body(dom_ref, o_ref, dhl_ref, h_ref, gm_ref, gl_ref, _, do_ref, dh_ref, dz_ref):
        gm = gm_ref[...]
        sm = _sig(gm)
        dom_v = dom_ref[...]
        do_ref[...] = dom_v * (gm * sm)
        dz_ref[:, 0:D] = (dom_v * o_ref[...] * (sm * (1.0 + gm * (1.0 - sm)))).astype(BF)
        gl = gl_ref[...]
        sl = _sig(gl)
        dhl_v = dhl_ref[...]
        dh_ref[...] = dhl_v * (gl * sl)
        dz_ref[:, D:2 * D] = (dhl_v * h_ref[...] * (sl * (1.0 + gl * (1.0 - sl)))).astype(BF)

    return _rowcall("gate_bwd", body, S, tr, [dom, o, dhl, hs, z, z, dz],
                    [_rs(tr, D), _rs(tr, D), _rs(tr, D), _rs(tr, D), _rs(tr, D, ZB_GM), _rs(tr, D, ZB_GL), ANY_SPEC],
                    (_sds((S, D), F32), _sds((S, D), F32), _sds((S, ZW), BF)),
                    (_rs(tr, D), _rs(tr, D), _dz_spec(tr, 2 * D, ZB_GM * D)), aliases={6: 2})


def merge_fwd(ym, yl, z, tr):
    S = ym.shape[0]

    def body(ym_ref, yl_ref, mm_ref, ml_ref, o_ref):
        o_ref[...] = (_sig(mm_ref[...]) * ym_ref[...] + _sig(ml_ref[...]) * yl_ref[...]).astype(BF)

    return _rowcall("merge_fwd", body, S, tr, [ym, yl, z, z],
                    [_rs(tr, D), _rs(tr, D), _rs(tr, D, ZB_MM), _rs(tr, D, ZB_ML)],
                    _sds((S, D), BF), _rs(tr, D))


def merge_bwd(dmg, ym, yl, z, tr):
    S = ym.shape[0]

    def body(d_ref, ym_ref, yl_ref, mm_ref, ml_ref, dym_ref, dyl_ref, dz_ref):
        d = d_ref[...]
        sm = _sig(mm_ref[...])
        sl = _sig(ml_ref[...])
        dym_ref[...] = (d * sm).astype(BF)
        dyl_ref[...] = (d * sl).astype(BF)
        dz_ref[:, 0:D] = (d * ym_ref[...] * (sm * (1.0 - sm))).astype(BF)
        dz_ref[:, D:2 * D] = (d * yl_ref[...] * (sl * (1.0 - sl))).astype(BF)

    return _rowcall("merge_bwd", body, S, tr, [dmg, ym, yl, z, z],
                    [_rs(tr, D), _rs(tr, D), _rs(tr, D), _rs(tr, D, ZB_MM), _rs(tr, D, ZB_ML)],
                    (_sds((S, D), BF), _sds((S, D), BF), _sds((S, ZW), BF)),
                    (_rs(tr, D), _rs(tr, D), _dz_spec(tr, 2 * D, ZB_MM * D)))


def ple_fin(x1, pe, gp, tr):
    S = x1.shape[0]

    def body(x_ref, pe_ref, gp_ref, o_ref):
        o_ref[...] = x_ref[...] + pe_ref[...] * _sig(gp_ref[...])

    return _rowcall("ple_fin", body, S, tr, [x1, pe, gp], [_rs(tr, D)] * 3, _sds((S, D), F32), _rs(tr, D))


def ple_bwd(dx2, pe, gp, tr):
    S = dx2.shape[0]

    def body(d_ref, pe_ref, gp_ref, dpe_ref, dgp_ref):
        d = d_ref[...]
        sg = _sig(gp_ref[...])
        dpe_ref[...] = (d * sg).astype(BF)
        dgp_ref[...] = (d * pe_ref[...] * (sg * (1.0 - sg))).astype(BF)

    return _rowcall("ple_bwd", body, S, tr, [dx2, pe, gp], [_rs(tr, D)] * 3,
                    (_sds((S, D), BF), _sds((S, D), BF)), (_rs(tr, D), _rs(tr, D)))


def final_loss(x, g, tgt, tr):
    S, W = x.shape
    n = S // tr

    def body(x_ref, g_ref, t_ref, ls_ref, dx_ref, dg_ref, accl, accg):
        i = pl.program_id(0)
        xv = x_ref[...]
        gv = g_ref[...]
        e = _rms_fwd(xv, gv) - t_ref[...]
        e2 = _fold8(e * e)
        l8 = e2[:, 0:128]
        for k in range(1, W // 128):
            l8 = l8 + e2[:, k * 128:(k + 1) * 128]
        dx, dgp = _rms_bwd(xv, e * (1.0 / W), gv)
        dx_ref[...] = dx
        _acc_rows(i, n, accg, dg_ref, _fold8(dgp))

        @pl.when(i == 0)
        def _():
            accl[...] = l8

        @pl.when(i > 0)
        def _():
            accl[...] += l8

        @pl.when(i == n - 1)
        def _():
            tot = jnp.sum(jnp.sum(accl[...], axis=0, keepdims=True), axis=1, keepdims=True)
            ls_ref[...] = jnp.broadcast_to(tot, (1, 128))

    return _rowcall("final_loss", body, S, tr, [x, g, tgt], [_rs(tr, W), _fs((1, W)), _rs(tr, W)],
                    (_sds((1, 128), F32), _sds((S, W), F32), _sds((1, W), F32)),
                    (_fs((1, 128)), _rs(tr, W), _fs((1, W))),
                    scratch=[pltpu.VMEM((8, 128), F32), pltpu.VMEM((8, W), F32)])


def _chunk_mask(qi, kj, tq, tk):
    rows = qi * tq + lax.broadcasted_iota(jnp.int32, (tq, tk), 0)
    cols = kj * tk + lax.broadcasted_iota(jnp.int32, (tq, tk), 1)
    return (cols >> CHUNK_SHIFT) <= (rows >> CHUNK_SHIFT)


def _load_kv(kv_ref, kr_ref, kj, tk):
    off = pl.multiple_of(kj * tk, tk)
    kv = kv_ref[pl.ds(off, tk), :]
    k = jnp.concatenate([kv[:, 0:128], kr_ref[pl.ds(off, tk), :]], axis=1)
    return k, kv[:, 128:256], off


def _split_refs(refs, n_in, n_out, comm):
    nc = comm.n if comm is not None else 0
    ins = refs[:n_in]
    c_in = refs[n_in:n_in + nc]
    outs = refs[n_in + nc:n_in + nc + n_out]
    c_out = refs[n_in + nc + n_out:n_in + 2 * nc + n_out]
    rest = refs[n_in + 2 * nc + n_out:]
    if comm is None:
        return ins, outs, rest, None
    return ins, outs, rest[:len(rest) - 3], comm.copies(c_in, c_out, *rest[len(rest) - 3:])


def _side_start(cps, first):
    if cps is None:
        return

    @pl.when(first)
    def _():
        for cp in cps:
            cp.start()


def _side_wait(cps, last):
    if cps is None:
        return

    @pl.when(last)
    def _():
        for cp in cps:
            cp.wait()


def attn_fwd(q_ext, kv_ext, kr, cs, tq, tk, comm=None):
    H, S, _ = q_ext.shape
    nq = S // tq
    nd = tq // tk
    nc = comm.n if comm is not None else 0
    HP = 2

    SL = min(64, tq)
    rep = tk // 128

    def body(*refs):
        ((q_ref, kv_ref, kr_ref, cs_ref), (o_ref, lse_ref, qf_ref),
         (s_scr, p_scr, m_scr, l_scr, a_scr, acc_scr), cps) = _split_refs(refs, 4, 3, comm)
        hh = pl.program_id(0)
        qi = pl.program_id(1)
        _side_start(cps, jnp.logical_and(hh == 0, qi == 0))
        for e in range(HP):
            q = q_ref[e]
            hi = q[:, 128:256] * cs_ref[...]
            hi = hi + pltpu.roll(hi, 64, 1)
            hi = jnp.where(_lane_lt(hi.shape, 64), hi, 0.0)
            qf_ref[e] = jnp.concatenate([q[:, 0:128], hi], axis=1).astype(BF)
        m_scr[...] = jnp.full(m_scr.shape, -jnp.inf, F32)
        l_scr[...] = jnp.zeros_like(l_scr)
        acc_scr[...] = jnp.zeros_like(acc_scr)

        def step(kj, masked):
            off = pl.multiple_of(kj * tk, tk)
            kr_t = kr_ref[pl.ds(off, tk), :]
            for e in range(HP):
                k = jnp.concatenate([kv_ref[e, pl.ds(off, tk), 0:128], kr_t], axis=1)
                s_scr[e] = _dot(qf_ref[e], k, NT)
            for e in range(HP):
                for r in range(tq // SL):
                    rows = pl.ds(r * SL, SL)
                    s = s_scr[e, rows, :]
                    if masked:
                        rws = qi * tq + r * SL + lax.broadcasted_iota(jnp.int32, (SL, tk), 0)
                        cls = kj * tk + lax.broadcasted_iota(jnp.int32, (SL, tk), 1)
                        s = jnp.where((cls >> CHUNK_SHIFT) <= (rws >> CHUNK_SHIFT), s, -jnp.inf)
                    m_old = m_scr[e, rows, :]
                    m_new = jnp.maximum(m_old, jnp.max(s, axis=1, keepdims=True))
                    alpha = jnp.exp2((m_old - m_new) * EXP2_SCALE)
                    p = jnp.exp2(s * EXP2_SCALE - jnp.tile(m_new * EXP2_SCALE, (1, rep)))
                    l_scr[e, rows, :] = alpha * l_scr[e, rows, :] + jnp.sum(p, axis=1, keepdims=True)
                    m_scr[e, rows, :] = m_new
                    a_scr[e, rows, :] = alpha
                    p_scr[e, rows, :] = p.astype(BF)
            for e in range(HP):
                pv = _dot(p_scr[e], kv_ref[e, pl.ds(off, tk), 128:256], NN)
                acc_scr[e] = a_scr[e] * acc_scr[e] + pv

        def trip(kj, c):
            step(kj, False)
            return c

        lax.fori_loop(0, qi * nd, trip, 0)
        for d in range(nd):
            step(qi * nd + d, True)
        for e in range(HP):
            l = l_scr[e]
            o_ref[:, e * 128:(e + 1) * 128] = acc_scr[e] / l
            lse_ref[e] = m_scr[e] * SCALE + jnp.log(l)
        _side_wait(cps, jnp.logical_and(hh == H // HP - 1, qi == nq - 1))

    res = pl.pallas_call(
        body, name="attn_fwd" if comm is None else "attn_fwd_comm", grid=(H // HP, nq),
        in_specs=[pl.BlockSpec((HP, tq, 256), lambda h, i: (h, i, 0)),
                  pl.BlockSpec((HP, S, 256), lambda h, i: (h, 0, 0)),
                  pl.BlockSpec((S, 128), lambda h, i: (0, 0)),
                  pl.BlockSpec((tq, 128), lambda h, i: (i, 0))] + [ANY_SPEC] * nc,
        out_specs=[pl.BlockSpec((tq, HP * 128), lambda h, i: (i, h)),
                   pl.BlockSpec((HP, tq, 128), lambda h, i: (h, i, 0)),
                   pl.BlockSpec((HP, tq, 256), lambda h, i: (h, i, 0))] + [ANY_SPEC] * nc,
        out_shape=[_sds((S, H * 128), F32), _sds((H, S, 128), F32), _sds((H, S, 256), BF)]
        + (comm.out_shapes() if comm is not None else []),
        scratch_shapes=[pltpu.VMEM((HP, tq, tk), F32), pltpu.VMEM((HP, tq, tk), BF)]
        + [pltpu.VMEM((HP, tq, 128), F32)] * 4 + (comm.scratch() if comm is not None else []),
        compiler_params=_params(("arbitrary", "arbitrary")),
    )(q_ext, kv_ext, kr, cs, *(comm.arrays if comm is not None else []))
    return res[:3], res[3:]


def attn_bwd(qf, kv_ext, kr, o, do, lse, cs, tq, tk, comm=None):
    H, S, _ = qf.shape
    nq = S // tq
    nd = tq // tk
    nc = comm.n if comm is not None else 0

    SL = min(64, tq)
    rep = tk // 128

    def body(*refs):
        ((q_ref, kv_ref, kr_ref, o_ref, do_ref, lse_ref, cs_ref), (dq_ref, dkv_ref, dkr_ref),
         (dk_acc, dv_acc, s_scr, dp_scr, p_scr, ds_scr, lse_scr, dl_scr, dob_scr, dq_scr), cps) = _split_refs(
             refs, 7, 3, comm)
        h = pl.program_id(0)
        qi = pl.program_id(1)
        _side_start(cps, jnp.logical_and(h == 0, qi == 0))

        @pl.when(qi == 0)
        def _():
            dk_acc[...] = jnp.zeros_like(dk_acc)
            dv_acc[...] = jnp.zeros_like(dv_acc)

        @pl.when(jnp.logical_and(h == 0, qi == 0))
        def _():
            dkr_ref[...] = jnp.zeros_like(dkr_ref)

        dov = do_ref[...]
        dl_scr[...] = jnp.broadcast_to(jnp.sum(dov * o_ref[...], axis=1, keepdims=True), (tq, 128))
        lse_scr[...] = lse_ref[...] * LOG2E
        dob_scr[...] = dov.astype(BF)
        dq_scr[...] = jnp.zeros_like(dq_scr)

        def step(kj, masked):
            k, v, off = _load_kv(kv_ref, kr_ref, kj, tk)
            s_scr[...] = _dot(q_ref[...], k, NT)
            dp_scr[...] = _dot(dob_scr[...], v, NT)
            for r in range(tq // SL):
                rows = pl.ds(r * SL, SL)
                s = s_scr[rows, :]
                if masked:
                    rws = qi * tq + r * SL + lax.broadcasted_iota(jnp.int32, (SL, tk), 0)
                    cls = kj * tk + lax.broadcasted_iota(jnp.int32, (SL, tk), 1)
                    s = jnp.where((cls >> CHUNK_SHIFT) <= (rws >> CHUNK_SHIFT), s, -jnp.inf)
                p = jnp.exp2(s * EXP2_SCALE - jnp.tile(lse_scr[rows, :], (1, rep)))
                p_scr[rows, :] = p.astype(BF)
                ds_scr[rows, :] = (p * (dp_scr[rows, :] - jnp.tile(dl_scr[rows, :], (1, rep)))).astype(BF)
            dv_acc[pl.ds(off, tk), :] += _dot_tn(p_scr[...], dob_scr[...])
            dk_acc[pl.ds(off, tk), :] += _dot_tn(ds_scr[...], q_ref[...])
            k2, _, _ = _load_kv(kv_ref, kr_ref, kj, tk)
            dq_scr[...] += _dot(ds_scr[...], k2, NN)

        def trip(kj, c):
            step(kj, False)
            return c

        lax.fori_loop(0, qi * nd, trip, 0)
        for d in range(nd):
            step(qi * nd + d, True)
        dq = dq_scr[...] * SCALE
        hi = dq[:, 128:256]
        hi = (hi + pltpu.roll(hi, 64, 1)) * cs_ref[...]
        dq_ref[...] = jnp.concatenate([dq[:, 0:128], hi], axis=1).astype(BF)

        @pl.when(qi == nq - 1)
        def _():
            dkv_ref[...] = jnp.concatenate([dk_acc[:, 0:128] * SCALE, dv_acc[...]], axis=1).astype(BF)
            dkr_ref[...] += dk_acc[:, 128:256] * SCALE

        _side_wait(cps, jnp.logical_and(h == H - 1, qi == nq - 1))

    res = pl.pallas_call(
        body, name="attn_bwd" if comm is None else "attn_bwd_comm", grid=(H, nq),
        in_specs=[pl.BlockSpec((None, tq, 256), lambda h, i: (h, i, 0)),
                  pl.BlockSpec((None, S, 256), lambda h, i: (h, 0, 0)),
                  pl.BlockSpec((S, 128), lambda h, i: (0, 0)),
                  pl.BlockSpec((tq, 128), lambda h, i: (i, h)),
                  pl.BlockSpec((tq, 128), lambda h, i: (i, h)),
                  pl.BlockSpec((None, tq, 128), lambda h, i: (h, i, 0)),
                  pl.BlockSpec((tq, 128), lambda h, i: (i, 0))] + [ANY_SPEC] * nc,
        out_specs=[pl.BlockSpec((None, tq, 256), lambda h, i: (h, i, 0)),
                   pl.BlockSpec((None, S, 256), lambda h, i: (h, 0, 0)),
                   pl.BlockSpec((S, 128), lambda h, i: (0, 0))] + [ANY_SPEC] * nc,
        out_shape=[_sds((H, S, 256), BF), _sds((H, S, 256), BF), _sds((S, 128), F32)]
        + (comm.out_shapes() if comm is not None else []),
        scratch_shapes=[pltpu.VMEM((S, 256), F32), pltpu.VMEM((S, 128), F32),
                        pltpu.VMEM((tq, tk), F32), pltpu.VMEM((tq, tk), F32),
                        pltpu.VMEM((tq, tk), BF), pltpu.VMEM((tq, tk), BF),
                        pltpu.VMEM((tq, 128), F32), pltpu.VMEM((tq, 128), F32),
                        pltpu.VMEM((tq, 128), BF), pltpu.VMEM((tq, 256), F32)]
        + (comm.scratch() if comm is not None else []),
        compiler_params=_params(("arbitrary", "arbitrary")),
    )(qf, kv_ext, kr, o, do, lse, cs, *(comm.arrays if comm is not None else []))
    return res[:3], res[3:]


def _log1p(y):
    w = 1.0 + y
    return jnp.where(w == 1.0, y, jnp.log(w) * (y / (w - 1.0)))


def _expm1(x):
    u = jnp.exp(x)
    return jnp.where(u == 1.0, x, (u - 1.0) * (x / jnp.log(u)))


def _softplus(x):
    return jnp.maximum(x, 0.0) + _log1p(jnp.exp(-jnp.abs(x)))


def _blockdiag(xb, w_ref):
    return jnp.concatenate(
        [_dot(xb[:, k * BD:(k + 1) * BD], w_ref[k], NN) for k in range(NBLK)], axis=1)


def _gates(xc, wr_ref, br_ref, wi_ref, bi_ref, sp):
    xb = xc.astype(BF)
    r = _sig(_blockdiag(xb, wr_ref) + br_ref[...])
    ig = _sig(_blockdiag(xb, wi_ref) + bi_ref[...])
    log_a = (-LRU_C * r) * sp
    mult = jnp.sqrt(-_expm1(2.0 * log_a))
    return xb, r, ig, log_a, mult


def _prev8_spec(tr, w, cb):
    return pl.BlockSpec((8, w), lambda i: (jnp.maximum(i * (tr // 8) - 1, 0), cb))


def lru_fwd(z, conv_w, conv_b, w_rg, b_rg, w_ig, b_ig, lam, tr):
    S = z.shape[0]
    W = D

    def body(u_ref, up_ref, cw_ref, cb_ref, wr_ref, br_ref, wi_ref, bi_ref, lam_ref,
             xc_ref, a_ref, h_ref, buf, bt, hcar):
        i = pl.program_id(0)

        @pl.when(i == 0)
        def _():
            buf[0:8, :] = jnp.zeros((8, W), F32)
            hcar[...] = jnp.zeros_like(hcar)

        @pl.when(i > 0)
        def _():
            buf[0:8, :] = up_ref[...]

        buf[8:8 + tr, :] = u_ref[...]
        cw = cw_ref[...]
        xc = buf[pl.ds(5, tr), :] * cw[0:1, :]
        for kk in range(1, 4):
            xc = xc + buf[pl.ds(5 + kk, tr), :] * cw[kk:kk + 1, :]
        xc = xc + cb_ref[...]
        xc_ref[...] = xc
        sp = _softplus(-lam_ref[...])
        _, _, ig, log_a, mult = _gates(xc, wr_ref, br_ref, wi_ref, bi_ref, sp)
        a_ref[...] = jnp.exp(log_a)
        bt[...] = mult * (ig * xc)
        row = lax.broadcasted_iota(jnp.int32, (8, W), 0)

        def grp(g, hp):
            off = pl.multiple_of(g * 8, 8)
            A = a_ref[pl.ds(off, 8), :]
            B = bt[pl.ds(off, 8), :]
            for d in (1, 2, 4):
                ok = row >= d
                B = jnp.where(ok, A * pltpu.roll(B, d, 0) + B, B)
                A = jnp.where(ok, A * pltpu.roll(A, d, 0), A)
            hh = A * hp + B
            h_ref[pl.ds(off, 8), :] = hh
            return hh[7:8, :]

        hcar[...] = lax.fori_loop(0, tr // 8, grp, hcar[...])

    return _rowcall(
        "lru_fwd", body, S, tr, [z, z, conv_w, conv_b, w_rg, b_rg, w_ig, b_ig, lam],
        [_rs(tr, W, ZB_U), _prev8_spec(tr, W, ZB_U), _fs((4, W)), _fs((1, W)), _fs((NBLK, BD, BD)), _fs((1, W)),
         _fs((NBLK, BD, BD)), _fs((1, W)), _fs((1, W))],
        (_sds((S, W), F32),) * 3, (_rs(tr, W),) * 3,
        scratch=[pltpu.VMEM((tr + 8, W), F32), pltpu.VMEM((tr, W), F32), pltpu.VMEM((1, W), F32)])


def lru_bwd(dh, a, hs, xc, w_rg, b_rg, w_ig, b_ig, lam, tr):
    S, W = dh.shape
    n = S // tr
    nb8 = S // 8

    def rev(i):
        return n - 1 - i

    row_spec = pl.BlockSpec((tr, W), lambda i: (rev(i), 0))
    next8 = pl.BlockSpec((8, W), lambda i: (jnp.minimum((rev(i) + 1) * (tr // 8), nb8 - 1), 0))
    prev8 = pl.BlockSpec((8, W), lambda i: (jnp.maximum(rev(i) * (tr // 8) - 1, 0), 0))

    def body(dh_ref, a_ref, an_ref, h_ref, hp_ref, xc_ref, wr_ref, br_ref, wi_ref, bi_ref, lam_ref,
             dxc_ref, dwr_ref, dwi_ref, dbr_ref, dbi_ref, dlam_ref, dcb_ref,
             bufa, bufh, apr, gsc, gcar, acc_br, acc_bi, acc_sp, acc_cb):
        i = pl.program_id(0)
        first = i == 0
        last_tile = i == n - 1

        bufa[0:tr, :] = a_ref[...]
        bufa[tr:tr + 8, :] = an_ref[...]
        apr[...] = bufa[pl.ds(1, tr), :]
        bufh[8:8 + tr, :] = h_ref[...]

        @pl.when(last_tile)
        def _():
            bufh[0:8, :] = jnp.zeros((8, W), F32)

        @pl.when(jnp.logical_not(last_tile))
        def _():
            bufh[0:8, :] = hp_ref[...]

        @pl.when(first)
        def _():
            gcar[...] = jnp.zeros_like(gcar)

        row = lax.broadcasted_iota(jnp.int32, (8, W), 0)
        ng = tr // 8

        def grp(t, gn):
            g = ng - 1 - t
            off = pl.multiple_of(g * 8, 8)
            A = apr[pl.ds(off, 8), :]
            B = dh_ref[pl.ds(off, 8), :]
            for d in (1, 2, 4):
                ok = row < 8 - d
                B = jnp.where(ok, B + A * pltpu.roll(B, 8 - d, 0), B)
                A = jnp.where(ok, A * pltpu.roll(A, 8 - d, 0), A)
            gg = B + A * gn
            gsc[pl.ds(off, 8), :] = gg
            return gg[0:1, :]

        gcar[...] = lax.fori_loop(0, ng, grp, gcar[...])

        G = gsc[...]
        hprev = bufh[pl.ds(7, tr), :]
        xcv = xc_ref[...]
        lam_v = lam_ref[...]
        sp = _softplus(-lam_v)
        xb, r, ig, log_a, mult = _gates(xcv, wr_ref, br_ref, wi_ref, bi_ref, sp)
        av = a_ref[...]
        d_a = G * hprev
        d_mult = G * (ig * xcv)
        d_ig = G * (mult * xcv)
        d_log_a = d_a * av - d_mult * (av * av / mult)
        d_gr = (d_log_a * (-LRU_C * sp)) * (r * (1.0 - r))
        d_gi = d_ig * (ig * (1.0 - ig))
        gr_b = d_gr.astype(BF)
        gi_b = d_gi.astype(BF)
        dxc = G * (mult * ig)
        dxc = dxc + jnp.concatenate(
            [_dot(gr_b[:, k * BD:(k + 1) * BD], wr_ref[k], NT)
             + _dot(gi_b[:, k * BD:(k + 1) * BD], wi_ref[k], NT) for k in range(NBLK)], axis=1)
        dxc_ref[...] = dxc

        @pl.when(first)
        def _():
            dwr_ref[...] = jnp.zeros_like(dwr_ref)
            dwi_ref[...] = jnp.zeros_like(dwi_ref)

        for k in range(NBLK):
            sl = slice(k * BD, (k + 1) * BD)
            dwr_ref[k] += _dot_tn(xb[:, sl], gr_b[:, sl])
            dwi_ref[k] += _dot_tn(xb[:, sl], gi_b[:, sl])

        _acc_rows(i, n, acc_br, dbr_ref, _fold8(d_gr))
        _acc_rows(i, n, acc_bi, dbi_ref, _fold8(d_gi))
        _acc_rows(i, n, acc_cb, dcb_ref, _fold8(dxc))

        part = _fold8(d_log_a * (-LRU_C * r))

        @pl.when(first)
        def _():
            acc_sp[...] = part

        @pl.when(jnp.logical_not(first))
        def _():
            acc_sp[...] += part

        @pl.when(last_tile)
        def _():
            dsp = jnp.sum(acc_sp[...], axis=0, keepdims=True)
            dlam_ref[...] = dsp * (-_sig(-lam_v))

    vec = _sds((1, W), F32)
    wsh = _sds((NBLK, BD, BD), F32)
    return _rowcall(
        "lru_bwd", body, S, tr, [dh, a, a, hs, hs, xc, w_rg, b_rg, w_ig, b_ig, lam],
        [row_spec, row_spec, next8, row_spec, prev8, row_spec, _fs((NBLK, BD, BD)), _fs((1, W)),
         _fs((NBLK, BD, BD)), _fs((1, W)), _fs((1, W))],
        (_sds((S, W), F32), wsh, wsh, vec, vec, vec, vec),
        (row_spec, _fs((NBLK, BD, BD)), _fs((NBLK, BD, BD)), _fs((1, W)), _fs((1, W)), _fs((1, W)), _fs((1, W))),
        scratch=[pltpu.VMEM((tr + 8, W), F32), pltpu.VMEM((tr + 8, W), F32), pltpu.VMEM((tr, W), F32),
                 pltpu.VMEM((tr, W), F32), pltpu.VMEM((1, W), F32), pltpu.VMEM((8, W), F32),
                 pltpu.VMEM((8, W), F32), pltpu.VMEM((8, W), F32), pltpu.VMEM((8, W), F32)])


def conv_bwd(dxc, z, conv_w, dz, tr):
    S, W = dxc.shape
    n = S // tr
    nb8 = S // 8

    def body(d_ref, dn_ref, u_ref, up_ref, cw_ref, _, du_ref, dcw_ref, bufd, bufu, acc):
        i = pl.program_id(0)
        bufd[0:tr, :] = d_ref[...]

        @pl.when(i == n - 1)
        def _():
            bufd[tr:tr + 8, :] = jnp.zeros((8, W), F32)

        @pl.when(i < n - 1)
        def _():
            bufd[tr:tr + 8, :] = dn_ref[...]

        @pl.when(i == 0)
        def _():
            bufu[0:8, :] = jnp.zeros((8, W), F32)

        @pl.when(i > 0)
        def _():
            bufu[0:8, :] = up_ref[...]

        bufu[8:8 + tr, :] = u_ref[...]
        cw = cw_ref[...]
        dv = d_ref[...]
        du = dv * cw[3:4, :]
        for j in range(1, 4):
            du = du + bufd[pl.ds(j, tr), :] * cw[3 - j:4 - j, :]
        du_ref[...] = du.astype(BF)
        parts = [jnp.sum(_fold8(dv * bufu[pl.ds(5 + kk, tr), :]), axis=0, keepdims=True) for kk in range(4)]
        part = jnp.concatenate(parts, axis=0)

        @pl.when(i == 0)
        def _():
            acc[...] = part

        @pl.when(i > 0)
        def _():
            acc[...] += part

        @pl.when(i == n - 1)
        def _():
            dcw_ref[...] = acc[...]

    next8 = pl.BlockSpec((8, W), lambda i: (jnp.minimum((i + 1) * (tr // 8), nb8 - 1), 0))
    return _rowcall(
        "conv_bwd", body, S, tr, [dxc, dxc, z, z, conv_w, dz],
        [_rs(tr, W), next8, _rs(tr, W, ZB_U), _prev8_spec(tr, W, ZB_U), _fs((4, W)), ANY_SPEC],
        (_sds((S, ZW), BF), _sds((4, W), F32)), (_dz_spec(tr, W, ZB_U * W), _fs((4, W))),
        scratch=[pltpu.VMEM((tr + 8, W), F32), pltpu.VMEM((tr + 8, W), F32), pltpu.VMEM((4, W), F32)],
        aliases={5: 0})


def _me():
    x = lax.axis_index("x")
    y = lax.axis_index("y")
    c = lax.axis_index("c")
    return x, y, c


def _peer(r):
    x, y, c = _me()
    px = jnp.bitwise_xor(x, (r >> 2) & 1)
    py = jnp.bitwise_xor(y, (r >> 1) & 1)
    pc = jnp.bitwise_xor(c, r & 1)
    return (px, py, pc), 4 * px + 2 * py + pc


class Comm:
    def __init__(self, arrays, modes):
        self.arrays = list(arrays)
        self.modes = list(modes)
        self.n = len(self.arrays)

    def out_shapes(self):
        return [_sds(((NDEV,) + a.shape) if md == "gather" else a.shape, a.dtype)
                for a, md in zip(self.arrays, self.modes)]

    def scratch(self):
        return [pltpu.SemaphoreType.DMA((self.n * (NDEV - 1),)),
                pltpu.SemaphoreType.DMA((self.n * (NDEV - 1),)),
                pltpu.SemaphoreType.DMA((self.n,))]

    def copies(self, ins, outs, send_sems, recv_sems, loc_sems):
        x, y, c = _me()
        me = 4 * x + 2 * y + c
        cps = []
        for ai, md in enumerate(self.modes):
            src = ins[ai] if md == "gather" else ins[ai].at[me]
            cps.append(pltpu.make_async_copy(src, outs[ai].at[me], loc_sems.at[ai]))
        for r in range(1, NDEV):
            dev, idx = _peer(r)
            for ai, md in enumerate(self.modes):
                src = ins[ai] if md == "gather" else ins[ai].at[idx]
                k = ai * (NDEV - 1) + r - 1
                cps.append(pltpu.make_async_remote_copy(
                    src_ref=src, dst_ref=outs[ai].at[me], send_sem=send_sems.at[k], recv_sem=recv_sems.at[k],
                    device_id=dev, device_id_type=pl.DeviceIdType.MESH))
        return cps


ANY_SPEC = pl.BlockSpec(memory_space=pl.ANY)


def comm_call(name, comm):
    na = comm.n

    def body(*refs):
        cps = comm.copies(refs[:na], refs[na:2 * na], *refs[2 * na:])
        for cp in cps:
            cp.start()
        for cp in cps:
            cp.wait()

    return pl.pallas_call(
        body, name=name, in_specs=[ANY_SPEC] * na, out_specs=[ANY_SPEC] * na, out_shape=comm.out_shapes(),
        scratch_shapes=comm.scratch(), compiler_params=pltpu.CompilerParams(has_side_effects=True),
    )(*comm.arrays)


def adamw(name, parts, w, m, v, tr):
    L = len(parts)
    _, R, C = parts[0].shape
    tr = _tile(R, tr)

    def body(*refs):
        p_refs = refs[:L]
        w_ref, m_ref, v_ref, g_ref, d_ref, nm_ref, nv_ref = refs[L:]
        for l in range(L):
            g = p_refs[l][0].astype(F32)
            for k in range(1, NDEV):
                g = g + p_refs[l][k].astype(F32)
            g_ref[l] = g
            mn = ADAM_B1 * m_ref[l] + (1.0 - ADAM_B1) * g
            vn = ADAM_B2 * v_ref[l] + (1.0 - ADAM_B2) * (g * g)
            m_hat = mn / (1.0 - ADAM_B1 ** ADAM_STEP)
            v_hat = vn / (1.0 - ADAM_B2 ** ADAM_STEP)
            d_ref[l] = -ADAM_LR * (m_hat / (jnp.sqrt(v_hat) + ADAM_EPS) + ADAM_WD * w_ref[l])
            nm_ref[l] = mn
            nv_ref[l] = vn

    blk = pl.BlockSpec((L, tr, C), lambda i: (0, i, 0))
    pblk = pl.BlockSpec((NDEV, tr, C), lambda i: (0, i, 0))
    return pl.pallas_call(
        body, name=name, grid=(R // tr,), in_specs=[pblk] * L + [blk, blk, blk],
        out_specs=(blk,) * 4, out_shape=(_sds((L, R, C), F32),) * 4,
        compiler_params=_params(("parallel",)),
    )(*parts, w, m, v)


def _rot(w):
    h = w.shape[-1] // 2
    return jnp.concatenate([-w[..., h:], w[..., :h]], axis=-1)


def _unrot(dw):
    h = dw.shape[-1] // 2
    return jnp.concatenate([dw[..., h:], -dw[..., :h]], axis=-1)


def _cols(g):
    n, R, C = g.shape
    return g.transpose(1, 0, 2).reshape(R, n * C)


def _rows(g):
    n, R, C = g.shape
    return g.reshape(n * R, C)


def _split_cols(dw):
    R, NC = dw.shape
    return dw.reshape(R, NDEV, NC // NDEV).transpose(1, 0, 2)


REPL = ["attn_norm", "q_a_norm", "kv_a_norm", "conv_b", "w_rg", "b_rg", "w_ig", "b_ig", "lru_lambda",
        "ple_norm", "final_norm"]
SHARDED = ["w_in", "w_q_b", "w_kv_b", "conv_w", "w_o_mla", "w_o_lru", "w_out", "w_ple_gate", "w_ple"]
WEIGHTS = ["attn_norm", "w_in", "q_a_norm", "w_q_b", "kv_a_norm", "w_kv_b", "conv_w", "conv_b", "w_rg", "b_rg",
           "w_ig", "b_ig", "lru_lambda", "w_o_mla", "w_o_lru", "w_out", "ple_norm", "w_ple_gate", "w_ple",
           "final_norm"]


GATE_W = ("w_rg", "w_ig")
REPL_LAYER = [k for k in REPL[1:-1] if k not in GATE_W]


def _pack(vals):
    flat = jnp.concatenate([v.reshape(-1) for v in vals])
    n = flat.shape[0]
    rows = -(-n // (128 * 256)) * 256
    return jnp.pad(flat, (0, rows * 128 - n)).reshape(rows, 128)


def _unpack(packed, shapes):
    flat = packed.reshape(-1)
    out = []
    off = 0
    for s in shapes:
        n = int(np.prod(s))
        out.append(flat[off:off + n].reshape(s))
        off += n
    return out


def kernel(x, p, positions, attn_norm, w_in, q_a_norm, w_q_b, kv_a_norm, w_kv_b, conv_w, conv_b, w_rg, b_rg, w_ig, b_ig, lru_lambda, w_o_mla, w_o_lru, w_out, ple_norm, w_ple_gate, w_ple, final_norm, loss_target, m_attn_norm, m_w_in, m_q_a_norm, m_w_q_b, m_kv_a_norm, m_w_kv_b, m_conv_w, m_conv_b, m_w_rg, m_b_rg, m_w_ig, m_b_ig, m_lru_lambda, m_w_o_mla, m_w_o_lru, m_w_out, m_ple_norm, m_w_ple_gate, m_w_ple, m_final_norm, v_attn_norm, v_w_in, v_q_a_norm, v_w_q_b, v_kv_a_norm, v_w_kv_b, v_conv_w, v_conv_b, v_w_rg, v_b_rg, v_w_ig, v_b_ig, v_lru_lambda, v_w_o_mla, v_w_o_lru, v_w_out, v_ple_norm, v_w_ple_gate, v_w_ple, v_final_norm):
    W = dict(attn_norm=attn_norm, w_in=w_in, q_a_norm=q_a_norm, w_q_b=w_q_b, kv_a_norm=kv_a_norm, w_kv_b=w_kv_b,
             conv_w=conv_w, conv_b=conv_b, w_rg=w_rg, b_rg=b_rg, w_ig=w_ig, b_ig=b_ig, lru_lambda=lru_lambda,
             w_o_mla=w_o_mla, w_o_lru=w_o_lru, w_out=w_out, ple_norm=ple_norm, w_ple_gate=w_ple_gate, w_ple=w_ple,
             final_norm=final_norm)
    M = dict(attn_norm=m_attn_norm, w_in=m_w_in, q_a_norm=m_q_a_norm, w_q_b=m_w_q_b, kv_a_norm=m_kv_a_norm,
             w_kv_b=m_w_kv_b, conv_w=m_conv_w, conv_b=m_conv_b, w_rg=m_w_rg, b_rg=m_b_rg, w_ig=m_w_ig, b_ig=m_b_ig,
             lru_lambda=m_lru_lambda, w_o_mla=m_w_o_mla, w_o_lru=m_w_o_lru, w_out=m_w_out, ple_norm=m_ple_norm,
             w_ple_gate=m_w_ple_gate, w_ple=m_w_ple, final_norm=m_final_norm)
    V = dict(attn_norm=v_attn_norm, w_in=v_w_in, q_a_norm=v_q_a_norm, w_q_b=v_w_q_b, kv_a_norm=v_kv_a_norm,
             w_kv_b=v_w_kv_b, conv_w=v_conv_w, conv_b=v_conv_b, w_rg=v_w_rg, b_rg=v_b_rg, w_ig=v_w_ig, b_ig=v_b_ig,
             lru_lambda=v_lru_lambda, w_o_mla=v_w_o_mla, w_o_lru=v_w_o_lru, w_out=v_w_out, ple_norm=v_ple_norm,
             w_ple_gate=v_w_ple_gate, w_ple=v_w_ple, final_norm=v_final_norm)

    L = w_in.shape[0]
    S = x.shape[1]
    xs = x[0]
    tgt = loss_target[0]
    tr = _tile(S, max(8, min(256, S // 2)))
    tq = _tile(S, max(256, min(1024, S // 2)))
    tk = max(128, min(512, tq // 2))

    inv_freq = ROPE_THETA ** (-jnp.arange(0, ROPE, 2, dtype=F32) / ROPE)
    ang = positions[0].astype(F32)[:, None] * inv_freq
    cs = jnp.concatenate([jnp.cos(ang), jnp.cos(ang), jnp.sin(ang), jnp.sin(ang)], axis=1)

    GATHERED = ["w_in", "w_q_b", "w_kv_b", "w_o_mla", "w_o_lru", "w_out", "w_ple_gate", "w_ple"]

    def gather_set(l, first=0):
        arrs = [W[k][l].astype(BF) for k in GATHERED] + [conv_w[l]]
        return Comm(arrs[first:], ["gather"] * (len(arrs) - first))

    def prep_win(g_in):
        win = _cols(g_in)
        kr_w = win[:, 1024:1088]
        return jnp.concatenate([win[:, 1088:3136], win[:, 5184:11328], win[:, 3136:5184], win[:, 0:1024], kr_w,
                                _rot(kr_w),
                                jnp.zeros((D, ZW - Z_KR - 128), BF)], axis=-1)

    def prep(g):
        g_qb, g_kvb, g_om, g_ol, g_out, g_pg, g_ple, g_cw = g[-8:]
        wq = _cols(g_qb).reshape(QL, NH, 192)
        d = dict(wq=jnp.concatenate([wq, _rot(wq[..., 128:])], axis=-1).reshape(QL, NH * 256),
                 wkv=_cols(g_kvb), wom=_rows(g_om), wol=_rows(g_ol), wout=_rows(g_out), wpg=_rows(g_pg),
                 wple=_cols(g_ple), cw=_cols(g_cw))
        if len(g) == 9:
            d["win"] = prep_win(g[0])
        return d

    wts = [None] * L
    win0 = prep_win(comm_call("gather_win0", Comm([w_in[0].astype(BF)], ["gather"]))[0])
    wrg_b = w_rg.astype(BF)
    wig_b = w_ig.astype(BF)

    def vec(a, l):
        return a[l][None, :]

    saved = []
    xcur = xs
    for l in range(L):
        h = norm_fwd("norm_in", xcur, vec(attn_norm, l), tr)
        if l == 0:
            z, got = mm("mm_in_comm", h, win0, "nn", tn=1280, tk=2048, comm=gather_set(0, first=1))
            wts[0] = dict(prep(got), win=win0)
        else:
            z = mm("mm_in", h, wts[l]["win"], "nn", tn=1280, tk=2048)
        wl = wts[l]
        qn, kvn, kr = qkv_prep(z, vec(q_a_norm, l), vec(kv_a_norm, l), cs, tr)
        q_ext = mm("mm_q", qn, wl["wq"], "nn", o_heads=True, tk=512)
        kv_ext = mm("mm_kv", kvn, wl["wkv"], "nn", o_heads=True, tk=512, out_dtype=BF)
        (o, lse, qf), got = attn_fwd(q_ext, kv_ext, kr, cs, tq, tk, gather_set(l + 1) if l + 1 < L else None)
        if l + 1 < L:
            wts[l + 1] = prep(got)
        xc, a, hs = lru_fwd(z, wl["cw"], vec(conv_b, l), wrg_b[l], vec(b_rg, l), wig_b[l], vec(b_ig, l),
                            vec(lru_lambda, l), tr)
        om, hl = gate_out(o, z, hs, tr)
        ym = mm("mm_om", om, wl["wom"], "nn", tk=2048)
        yl = mm("mm_ol", hl, wl["wol"], "nn", tk=2048)
        mg = merge_fwd(ym, yl, z, tr)
        x1 = mm("mm_out", mg, wl["wout"], "nn", tk=2048, res=xcur)
        hp = norm_fwd("norm_ple", x1, vec(ple_norm, l), tr)
        gp = mm("mm_pg", hp, wl["wpg"], "nn", tk=2048)
        pe = mm("mm_ple", p[l, 0], wl["wple"], "nn")
        x2 = ple_fin(x1, pe, gp, tr)
        saved.append(dict(x=xcur, h=h, z=z, qn=qn, kvn=kvn, kr=kr, kv_ext=kv_ext, o=o, lse=lse, qf=qf, xc=xc, a=a,
                          hs=hs, om=om, hl=hl, ym=ym, yl=yl, mg=mg, x1=x1, hp=hp, gp=gp, pe=pe))
        xcur = x2

    lsum, dx, d_final = final_loss(xcur, final_norm[None, :], tgt, tr)
    loss = lax.psum(0.5 * lsum[0, 0] / D, MESH_AXES)

    gr = {k: [None] * L for k in WEIGHTS if k != "final_norm"}
    landed_early = [None] * L
    landed_late = [None] * L
    zero_fn = jnp.zeros_like(final_norm)

    def early_grads(l):
        return ([gr[k][l].reshape(NDEV, D // NDEV, D) for k in ("w_o_mla", "w_o_lru", "w_out", "w_ple_gate")]
                + [_split_cols(gr["w_ple"][l])], ["a2a"] * 5)

    def late_grads(l):
        rep = _pack([gr[k][l] for k in REPL_LAYER] + [d_final[0] if l == L - 1 else zero_fn])
        return ([_split_cols(gr[k][l]) for k in ("w_in", "w_q_b", "w_kv_b", "conv_w")] + [rep]
                + [gr[k][l].reshape(NBLK * BD, BD) for k in GATE_W], ["a2a"] * 4 + ["gather"] * 3)

    for l in reversed(range(L)):
        sv = saved[l]
        wl = wts[l]
        z = sv["z"]
        dpe, dgp = ple_bwd(dx, sv["pe"], sv["gp"], tr)
        gr["w_ple"][l] = mm("mm_dple", p[l, 0], dpe, "tn", tm=256, tk=2048, out_dtype=BF)
        gr["w_ple_gate"][l] = mm("mm_dpg", sv["hp"], dgp, "tn", tk=2048, out_dtype=BF)
        dhp = mm("mm_dhp", dgp, wl["wpg"], "nt", tk=2048)
        dx1, gr["ple_norm"][l] = norm_bwd("norm_ple_bwd", sv["x1"], dhp, vec(ple_norm, l), dx, tr)
        dmg = mm("mm_dmg", dx1, wl["wout"], "nt", tk=2048)
        gr["w_out"][l] = mm("mm_dwout", sv["mg"], dx1, "tn", tk=2048, out_dtype=BF)
        dym, dyl, dz = merge_bwd(dmg, sv["ym"], sv["yl"], z, tr)
        dom = mm("mm_dom", dym, wl["wom"], "nt", tk=2048)
        gr["w_o_mla"][l] = mm("mm_dwom", sv["om"], dym, "tn", tk=2048, out_dtype=BF)
        dhl = mm("mm_dhl", dyl, wl["wol"], "nt", tk=2048)
        gr["w_o_lru"][l] = mm("mm_dwol", sv["hl"], dyl, "tn", tk=2048, out_dtype=BF)
        do, dhs, dz = gate_bwd(dom, sv["o"], dhl, sv["hs"], z, dz, tr)
        arrs, modes = early_grads(l)
        if l + 1 < L:
            arrs2, modes2 = late_grads(l + 1)
            arrs, modes = arrs + arrs2, modes + modes2
        (dq_ext, dkv_ext, dkr), got = attn_bwd(sv["qf"], sv["kv_ext"], sv["kr"], sv["o"], do, sv["lse"], cs, tq, tk,
                                               Comm(arrs, modes))
        landed_early[l] = got[:5]
        if l + 1 < L:
            landed_late[l + 1] = got[5:]
        dqn = mm("mm_dqn", dq_ext, wl["wq"], "nt", a_heads=True, tn=512, tk=2048)
        dwq_ext = mm("mm_dwq", sv["qn"], dq_ext, "tn", b_heads=True, tm=512, tk=2048, out_dtype=BF)
        dkn = mm("mm_dkn", dkv_ext, wl["wkv"], "nt", a_heads=True, tn=512, tk=2048)
        gr["w_kv_b"][l] = mm("mm_dwkv", sv["kvn"], dkv_ext, "tn", b_heads=True, tm=512, tk=2048, out_dtype=BF)
        dz, gr["q_a_norm"][l], gr["kv_a_norm"][l] = qkv_prep_bwd(
            z, dqn, dkn, dkr, vec(q_a_norm, l), vec(kv_a_norm, l), cs, dz, tr)
        dwq4 = dwq_ext.reshape(QL, NH, 256)
        gr["w_q_b"][l] = jnp.concatenate(
            [dwq4[..., 0:128], dwq4[..., 128:192] + _unrot(dwq4[..., 192:256])], axis=-1).reshape(QL, NH * 192)
        (dxc, gr["w_rg"][l], gr["w_ig"][l], gr["b_rg"][l], gr["b_ig"][l], gr["lru_lambda"][l],
         gr["conv_b"][l]) = lru_bwd(dhs, sv["a"], sv["hs"], sv["xc"], wrg_b[l], vec(b_rg, l), wig_b[l],
                                    vec(b_ig, l), vec(lru_lambda, l), tr)
        dz, gr["conv_w"][l] = conv_bwd(dxc, z, wl["cw"], dz, tr)
        dwin_ext = mm("mm_dwin", sv["h"], dz, "tn", tn=1280, tk=2048, out_dtype=BF)
        gr["w_in"][l] = jnp.concatenate(
            [dwin_ext[:, Z_Q:Z_KR], dwin_ext[:, Z_KR:Z_KR + 64] + _unrot(dwin_ext[:, Z_KR + 64:Z_KR + 128]),
             dwin_ext[:, 0:D], dwin_ext[:, 4 * D:5 * D], dwin_ext[:, D:4 * D]], axis=1)
        if l == 0:
            dh, landed_late[0] = mm("mm_dh_comm", dz, wl["win"], "nt", tk=2304, comm=Comm(*late_grads(0)))
        else:
            dh = mm("mm_dh", dz, wl["win"], "nt", tk=2304)
        dx, gr["attn_norm"][l] = norm_bwd("norm_in_bwd", sv["x"], dh, vec(attn_norm, l), dx1, tr)
    grad_x = dx[None]
    (landed_an,) = comm_call("gather_attn_norm", Comm([jnp.concatenate(gr["attn_norm"]).reshape(L * D // 128, 128)],
                                                      ["gather"]))

    res = {}
    for ki, k in enumerate(SHARDED):
        parts = [landed_late[l][ki] if ki < 4 else landed_early[l][ki - 4] for l in range(L)]
        res[k] = adamw("adamw_" + k, parts, W[k], M[k], V[k], 64)

    def rep_pack(T):
        fn = jnp.concatenate([jnp.zeros((L - 1, D), F32), T["final_norm"][None]], axis=0)
        flat = jnp.concatenate([T[k].reshape(L, -1) for k in REPL_LAYER] + [fn], axis=1)
        rows = landed_late[0][4].shape[1]
        return jnp.pad(flat, ((0, 0), (0, rows * 128 - flat.shape[1]))).reshape(L, rows, 128)

    rep = adamw("adamw_rep", [landed_late[l][4] for l in range(L)], rep_pack(W), rep_pack(M), rep_pack(V), 256)
    shapes = [W[k].shape[1:] for k in REPL_LAYER] + [final_norm.shape]
    per_layer = [[_unpack(t[l], shapes) for l in range(L)] for t in rep]
    for i, k in enumerate(REPL_LAYER):
        res[k] = tuple(jnp.stack([per_layer[t][l][i] for l in range(L)]) for t in range(4))
    res["final_norm"] = tuple(per_layer[t][L - 1][-1] for t in range(4))
    for gi, k in enumerate(GATE_W):
        gw = adamw("adamw_" + k, [landed_late[l][5 + gi] for l in range(L)],
                   *(T[k].reshape(L, NBLK * BD, BD) for T in (W, M, V)), 256)
        res[k] = tuple(t.reshape(W[k].shape) for t in gw)
    an = adamw("adamw_attn_norm", [landed_an], *(T["attn_norm"].reshape(1, L * D // 128, 128) for T in (W, M, V)), 64)
    res["attn_norm"] = tuple(t.reshape(L, D) for t in an)

    outs = [loss, grad_x]
    for t in range(4):
        outs += [res[k][t] for k in WEIGHTS]
    return tuple(outs)
```

```python
import functools
import math

import numpy as np
import jax
import jax.numpy as jnp
from jax import lax
from jax.experimental import pallas as pl
from jax.experimental.pallas import tpu as pltpu

F32 = jnp.float32
BF = jnp.bfloat16
MESH_AXES = ("x", "y", "c")
NDEV = 8

D = 2048
NH = 16
QL = 512
KVL = 512
ROPE = 64
PLE = 256
NBLK = 16
BD = 128
CHUNK_SHIFT = 6
EPS = 1e-6
LRU_C = 8.0
ROPE_THETA = 10000.0
IN_TOTAL = 11328
ZW = 11520
ZB_GM, ZB_GL, ZB_MM, ZB_ML, ZB_U = 0, 1, 2, 3, 4
Z_Q = 10240
Z_KV = 10752
Z_KR = 11264
SCALE = 1.0 / math.sqrt(128 + 64)
EXP2_SCALE = SCALE * math.log2(math.e)
LOG2E = math.log2(math.e)

ADAM_LR = 0.001
ADAM_B1 = 0.9
ADAM_B2 = 0.999
ADAM_EPS = 1e-08
ADAM_WD = 0.01
ADAM_STEP = 10

VMEM_LIMIT = 60 * 1024 * 1024

NN = (((1,), (0,)), ((), ()))
NT = (((1,), (1,)), ((), ()))


def _tile(n, pref):
    t = min(n, pref)
    while n % t:
        t //= 2
    return t


def _params(sem):
    return pltpu.CompilerParams(dimension_semantics=sem, vmem_limit_bytes=VMEM_LIMIT)


def _dot(a, b, dims=NN):
    return lax.dot_general(a, b, dims, preferred_element_type=F32)


def _dot_tn(a, b):
    return lax.dot_general(a.T, b, NN, preferred_element_type=F32)


def _bf(v):
    return v if v.dtype == BF else v.astype(BF)


def mm(name, a, b, mode, *, out_dtype=F32, tm=1024, tn=1024, tk=1024, res=None,
       a_heads=False, b_heads=False, o_heads=False, comm=None):
    if mode == "nn":
        M = a.shape[0]
        K = a.shape[1]
        N = b.shape[1]
    elif mode == "nt":
        M = a.shape[1] if a_heads else a.shape[0]
        K = a.shape[0] * a.shape[2] if a_heads else a.shape[1]
        N = b.shape[0]
    else:
        K = a.shape[0]
        M = a.shape[1]
        N = b.shape[0] * b.shape[2] if b_heads else b.shape[1]
    tm = _tile(M, tm)
    tn = _tile(N, tn)
    tk = _tile(K, tk)
    nk = K // tk
    grid = (M // tm, N // tn, nk)
    nc = comm.n if comm is not None else 0

    if mode == "nn":
        a_spec = pl.BlockSpec((tm, tk), lambda i, j, k: (i, k))
        b_spec = pl.BlockSpec((tk, tn), lambda i, j, k: (k, j))
    elif mode == "nt":
        if a_heads:
            a_spec = pl.BlockSpec((tk // 256, tm, 256), lambda i, j, k: (k, i, 0))
        else:
            a_spec = pl.BlockSpec((tm, tk), lambda i, j, k: (i, k))
        b_spec = pl.BlockSpec((tn, tk), lambda i, j, k: (j, k))
    else:
        a_spec = pl.BlockSpec((tk, tm), lambda i, j, k: (k, i))
        if b_heads:
            b_spec = pl.BlockSpec((tn // 256, tk, 256), lambda i, j, k: (j, k, 0))
        else:
            b_spec = pl.BlockSpec((tk, tn), lambda i, j, k: (k, j))
    if o_heads:
        o_spec = pl.BlockSpec((tn // 256, tm, 256), lambda i, j, k: (j, i, 0))
        o_shape = jax.ShapeDtypeStruct((N // 256, M, 256), out_dtype)
    else:
        o_spec = pl.BlockSpec((tm, tn), lambda i, j, k: (i, j))
        o_shape = jax.ShapeDtypeStruct((M, N), out_dtype)
    in_specs = [a_spec, b_spec]
    args = [a, b]
    if res is not None:
        in_specs.append(pl.BlockSpec((tm, tn), lambda i, j, k: (i, j)))
        args.append(res)
    n_in = len(args)

    def heads(ref):
        return jnp.concatenate([ref[e] for e in range(ref.shape[0])], axis=1)

    def body(*refs):
        ins, (o_ref,), scr, cps = _split_refs(refs, n_in, 1, comm)
        a_ref, b_ref = ins[0], ins[1]
        r_ref = ins[2] if res is not None else None
        ids = [pl.program_id(d) for d in range(3)]
        _side_start(cps, functools.reduce(jnp.logical_and, [ids[d] == 0 for d in range(3)]))
        av = _bf(heads(a_ref) if a_heads else a_ref[...])
        bv = _bf(heads(b_ref) if b_heads else b_ref[...])
        if mode == "nn":
            p = _dot(av, bv, NN)
        elif mode == "nt":
            p = _dot(av, bv, NT)
        else:
            p = _dot_tn(av, bv)

        def fin(acc):
            if r_ref is not None:
                acc = acc + r_ref[...]
            acc = acc.astype(o_ref.dtype)
            if o_heads:
                for e in range(tn // 256):
                    o_ref[e] = acc[:, e * 256:(e + 1) * 256]
            else:
                o_ref[...] = acc

        if nk == 1:
            fin(p)
        else:
            acc_ref = scr[0]
            k = ids[2]

            @pl.when(k == 0)
            def _():
                acc_ref[...] = p

            @pl.when(k > 0)
            def _():
                acc_ref[...] += p

            @pl.when(k == nk - 1)
            def _():
                fin(acc_ref[...])

        _side_wait(cps, functools.reduce(jnp.logical_and, [ids[d] == grid[d] - 1 for d in range(3)]))

    scratch = ([] if nk == 1 else [pltpu.VMEM((tm, tn), F32)]) + (comm.scratch() if comm is not None else [])
    sem = ("arbitrary",) * 3 if comm is not None else ("parallel", "parallel", "arbitrary")
    out = pl.pallas_call(
        body, name=name, grid=grid, in_specs=in_specs + [ANY_SPEC] * nc, out_specs=[o_spec] + [ANY_SPEC] * nc,
        out_shape=[o_shape] + (comm.out_shapes() if comm is not None else []),
        scratch_shapes=scratch, compiler_params=_params(sem),
    )(*args, *(comm.arrays if comm is not None else []))
    return out[0] if comm is None else (out[0], out[1:])


def _rs(tr, w, cb=0):
    return pl.BlockSpec((tr, w), lambda i: (i, cb))


def _fs(shape):
    nd = len(shape)
    return pl.BlockSpec(shape, lambda i: (0,) * nd)


def _rowcall(name, body, S, tr, ins, in_specs, outs, out_specs, scratch=(), aliases=None):
    return pl.pallas_call(
        body, name=name, grid=(S // tr,), in_specs=in_specs, out_specs=out_specs, out_shape=outs,
        scratch_shapes=list(scratch), compiler_params=_params(("arbitrary",)),
        input_output_aliases=aliases or {},
    )(*ins)


def _dz_spec(tr, w, off):
    return pl.BlockSpec((tr, w), lambda i: (i, off // w))


def _sds(shape, dt):
    return jax.ShapeDtypeStruct(shape, dt)


def _fold8(v):
    tr, w = v.shape
    return jnp.sum(v.reshape(tr // 8, 8, w), axis=0)


def _acc_rows(i, n, acc_ref, out_ref, part):
    @pl.when(i == 0)
    def _():
        acc_ref[...] = part

    @pl.when(i > 0)
    def _():
        acc_ref[...] += part

    @pl.when(i == n - 1)
    def _():
        out_ref[...] = jnp.sum(acc_ref[...], axis=0, keepdims=True)


def _rms_fwd(x, g):
    r = lax.rsqrt(jnp.mean(x * x, axis=-1, keepdims=True) + EPS)
    return x * r * g


def _rms_bwd(x, dy, g):
    r = lax.rsqrt(jnp.mean(x * x, axis=-1, keepdims=True) + EPS)
    xh = x * r
    dxh = dy * g
    dx = r * (dxh - xh * jnp.mean(dxh * xh, axis=-1, keepdims=True))
    return dx, dy * xh


def _sig(x):
    return jax.nn.sigmoid(x)


def norm_fwd(name, x, g, tr):
    S, W = x.shape

    def body(x_ref, g_ref, o_ref):
        o_ref[...] = _rms_fwd(x_ref[...], g_ref[...]).astype(BF)

    return _rowcall(name, body, S, tr, [x, g], [_rs(tr, W), _fs((1, W))], _sds((S, W), BF), _rs(tr, W))


def norm_bwd(name, x, dy, g, dres, tr):
    S, W = x.shape
    n = S // tr

    def body(x_ref, dy_ref, g_ref, dr_ref, dx_ref, dg_ref, acc_ref):
        i = pl.program_id(0)
        dx, dgp = _rms_bwd(x_ref[...], dy_ref[...], g_ref[...])
        dx_ref[...] = dr_ref[...] + dx
        _acc_rows(i, n, acc_ref, dg_ref, _fold8(dgp))

    return _rowcall(name, body, S, tr, [x, dy, g, dres],
                    [_rs(tr, W), _rs(tr, W), _fs((1, W)), _rs(tr, W)],
                    (_sds((S, W), F32), _sds((1, W), F32)), (_rs(tr, W), _fs((1, W))),
                    scratch=[pltpu.VMEM((8, W), F32)])


def _lane_lt(shape, n):
    return lax.broadcasted_iota(jnp.int32, shape, 1) < n


def qkv_prep(z, qg, kvg, cs, tr):
    S = z.shape[0]

    def body(ql_ref, kl_ref, zk_ref, qg_ref, kg_ref, cs_ref, qn_ref, kn_ref, kr_ref):
        qn_ref[...] = _rms_fwd(ql_ref[...], qg_ref[...]).astype(BF)
        kn_ref[...] = _rms_fwd(kl_ref[...], kg_ref[...]).astype(BF)
        t = zk_ref[...] * cs_ref[...]
        t = t + pltpu.roll(t, 64, 1)
        kr_ref[...] = jnp.where(_lane_lt(t.shape, 64), t, 0.0).astype(BF)

    return _rowcall(
        "qkv_prep", body, S, tr, [z, z, z, qg, kvg, cs],
        [_rs(tr, QL, Z_Q // QL), _rs(tr, KVL, Z_KV // KVL), _rs(tr, 128, Z_KR // 128),
         _fs((1, QL)), _fs((1, KVL)), _rs(tr, 128)],
        (_sds((S, QL), BF), _sds((S, KVL), BF), _sds((S, 128), BF)),
        (_rs(tr, QL), _rs(tr, KVL), _rs(tr, 128)))


def qkv_prep_bwd(z, dqn, dkn, dkr, qg, kvg, cs, dz, tr):
    S = z.shape[0]
    n = S // tr

    def body(ql_ref, kl_ref, dq_ref, dk_ref, dkr_ref, qg_ref, kg_ref, cs_ref, _,
             dz_ref, dqg_ref, dkg_ref, accq, acck):
        i = pl.program_id(0)
        dql, gq = _rms_bwd(ql_ref[...], dq_ref[...], qg_ref[...])
        dkl, gk = _rms_bwd(kl_ref[...], dk_ref[...], kg_ref[...])
        t = jnp.where(_lane_lt((tr, 128), 64), dkr_ref[...], 0.0)
        t = (t + pltpu.roll(t, 64, 1)) * cs_ref[...]
        dz_ref[...] = jnp.concatenate(
            [dql.astype(BF), dkl.astype(BF), t.astype(BF), jnp.zeros((tr, 128), BF)], axis=1)
        _acc_rows(i, n, accq, dqg_ref, _fold8(gq))
        _acc_rows(i, n, acck, dkg_ref, _fold8(gk))

    return _rowcall(
        "qkv_prep_bwd", body, S, tr, [z, z, dqn, dkn, dkr, qg, kvg, cs, dz],
        [_rs(tr, QL, Z_Q // QL), _rs(tr, KVL, Z_KV // KVL), _rs(tr, QL), _rs(tr, KVL), _rs(tr, 128),
         _fs((1, QL)), _fs((1, KVL)), _rs(tr, 128), ANY_SPEC],
        (_sds((S, ZW), BF), _sds((1, QL), F32), _sds((1, KVL), F32)),
        (_dz_spec(tr, ZW - Z_Q, Z_Q), _fs((1, QL)), _fs((1, KVL))),
        scratch=[pltpu.VMEM((8, QL), F32), pltpu.VMEM((8, KVL), F32)], aliases={8: 0})


def gate_out(o, z, hs, tr):
    S = o.shape[0]

    def body(o_ref, gm_ref, h_ref, gl_ref, om_ref, hl_ref):
        gm = gm_ref[...]
        gl = gl_ref[...]
        om_ref[...] = (o_ref[...] * (gm * _sig(gm))).astype(BF)
        hl_ref[...] = (h_ref[...] * (gl * _sig(gl))).astype(BF)

    return _rowcall("gate_out", body, S, tr, [o, z, hs, z],
                    [_rs(tr, D), _rs(tr, D, ZB_GM), _rs(tr, D), _rs(tr, D, ZB_GL)],
                    (_sds((S, D), BF), _sds((S, D), BF)), (_rs(tr, D), _rs(tr, D)))


def gate_bwd(dom, o, dhl, hs, z, dz, tr):
    S = o.shape[0]

    def body(dom_ref, o_ref, dhl_ref, h_ref, gm_ref, gl_ref, _, do_ref, dh_ref, dz_ref):
        gm = gm_ref[...]
        sm = _sig(gm)
        dom_v = dom_ref[...]
        do_ref[...] = dom_v * (gm * sm)
        dz_ref[:, 0:D] = (dom_v * o_ref[...] * (sm * (1.0 + gm * (1.0 - sm)))).astype(BF)
        gl = gl_ref[...]
        sl = _sig(gl)
        dhl_v = dhl_ref[...]
        dh_ref[...] = dhl_v * (gl * sl)
        dz_ref[:, D:2 * D] = (dhl_v * h_ref[...] * (sl * (1.0 + gl * (1.0 - sl)))).astype(BF)

    return _rowcall("gate_bwd", body, S, tr, [dom, o, dhl, hs, z, z, dz],
                    [_rs(tr, D), _rs(tr, D), _rs(tr, D), _rs(tr, D), _rs(tr, D, ZB_GM), _rs(tr, D, ZB_GL), ANY_SPEC],
                    (_sds((S, D), F32), _sds((S, D), F32), _sds((S, ZW), BF)),
                    (_rs(tr, D), _rs(tr, D), _dz_spec(tr, 2 * D, ZB_GM * D)), aliases={6: 2})


def merge_fwd(ym, yl, z, tr):
    S = ym.shape[0]

    def body(ym_ref, yl_ref, mm_ref, ml_ref, o_ref):
        o_ref[...] = (_sig(mm_ref[...]) * ym_ref[...] + _sig(ml_ref[...]) * yl_ref[...]).astype(BF)

    return _rowcall("merge_fwd", body, S, tr, [ym, yl, z, z],
                    [_rs(tr, D), _rs(tr, D), _rs(tr, D, ZB_MM), _rs(tr, D, ZB_ML)],
                    _sds((S, D), BF), _rs(tr, D))


def merge_bwd(dmg, ym, yl, z, tr):
    S = ym.shape[0]

    def body(d_ref, ym_ref, yl_ref, mm_ref, ml_ref, dym_ref, dyl_ref, dz_ref):
        d = d_ref[...]
        sm = _sig(mm_ref[...])
        sl = _sig(ml_ref[...])
        dym_ref[...] = (d * sm).astype(BF)
        dyl_ref[...] = (d * sl).astype(BF)
        dz_ref[:, 0:D] = (d * ym_ref[...] * (sm * (1.0 - sm))).astype(BF)
        dz_ref[:, D:2 * D] = (d * yl_ref[...] * (sl * (1.0 - sl))).astype(BF)

    return _rowcall("merge_bwd", body, S, tr, [dmg, ym, yl, z, z],
                    [_rs(tr, D), _rs(tr, D), _rs(tr, D), _rs(tr, D, ZB_MM), _rs(tr, D, ZB_ML)],
                    (_sds((S, D), BF), _sds((S, D), BF), _sds((S, ZW), BF)),
                    (_rs(tr, D), _rs(tr, D), _dz_spec(tr, 2 * D, ZB_MM * D)))


def ple_fin(x1, pe, gp, tr):
    S = x1.shape[0]

    def body(x_ref, pe_ref, gp_ref, o_ref):
        o_ref[...] = x_ref[...] + pe_ref[...] * _sig(gp_ref[...])

    return _rowcall("ple_fin", body, S, tr, [x1, pe, gp], [_rs(tr, D)] * 3, _sds((S, D), F32), _rs(tr, D))


def ple_bwd(dx2, pe, gp, tr):
    S = dx2.shape[0]

    def body(d_ref, pe_ref, gp_ref, dpe_ref, dgp_ref):
        d = d_ref[...]
        sg = _sig(gp_ref[...])
        dpe_ref[...] = (d * sg).astype(BF)
        dgp_ref[...] = (d * pe_ref[...] * (sg * (1.0 - sg))).astype(BF)

    return _rowcall("ple_bwd", body, S, tr, [dx2, pe, gp], [_rs(tr, D)] * 3,
                    (_sds((S, D), BF), _sds((S, D), BF)), (_rs(tr, D), _rs(tr, D)))


def final_loss(x, g, tgt, tr):
    S, W = x.shape
    n = S // tr

    def body(x_ref, g_ref, t_ref, ls_ref, dx_ref, dg_ref, accl, accg):
        i = pl.program_id(0)
        xv = x_ref[...]
        gv = g_ref[...]
        e = _rms_fwd(xv, gv) - t_ref[...]
        e2 = _fold8(e * e)
        l8 = e2[:, 0:128]
        for k in range(1, W // 128):
            l8 = l8 + e2[:, k * 128:(k + 1) * 128]
        dx, dgp = _rms_bwd(xv, e * (1.0 / W), gv)
        dx_ref[...] = dx
        _acc_rows(i, n, accg, dg_ref, _fold8(dgp))

        @pl.when(i == 0)
        def _():
            accl[...] = l8

        @pl.when(i > 0)
        def _():
            accl[...] += l8

        @pl.when(i == n - 1)
        def _():
            tot = jnp.sum(jnp.sum(accl[...], axis=0, keepdims=True), axis=1, keepdims=True)
            ls_ref[...] = jnp.broadcast_to(tot, (1, 128))

    return _rowcall("final_loss", body, S, tr, [x, g, tgt], [_rs(tr, W), _fs((1, W)), _rs(tr, W)],
                    (_sds((1, 128), F32), _sds((S, W), F32), _sds((1, W), F32)),
                    (_fs((1, 128)), _rs(tr, W), _fs((1, W))),
                    scratch=[pltpu.VMEM((8, 128), F32), pltpu.VMEM((8, W), F32)])


def _chunk_mask(qi, kj, tq, tk):
    rows = qi * tq + lax.broadcasted_iota(jnp.int32, (tq, tk), 0)
    cols = kj * tk + lax.broadcasted_iota(jnp.int32, (tq, tk), 1)
    return (cols >> CHUNK_SHIFT) <= (rows >> CHUNK_SHIFT)


def _load_kv(kv_ref, kr_ref, kj, tk):
    off = pl.multiple_of(kj * tk, tk)
    kv = kv_ref[pl.ds(off, tk), :]
    k = jnp.concatenate([kv[:, 0:128], kr_ref[pl.ds(off, tk), :]], axis=1)
    return k, kv[:, 128:256], off


def _split_refs(refs, n_in, n_out, comm):
    nc = comm.n if comm is not None else 0
    ins = refs[:n_in]
    c_in = refs[n_in:n_in + nc]
    outs = refs[n_in + nc:n_in + nc + n_out]
    c_out = refs[n_in + nc + n_out:n_in + 2 * nc + n_out]
    rest = refs[n_in + 2 * nc + n_out:]
    if comm is None:
        return ins, outs, rest, None
    return ins, outs, rest[:len(rest) - 3], comm.copies(c_in, c_out, *rest[len(rest) - 3:])


def _side_start(cps, first):
    if cps is None:
        return

    @pl.when(first)
    def _():
        for cp in cps:
            cp.start()


def _side_wait(cps, last):
    if cps is None:
        return

    @pl.when(last)
    def _():
        for cp in cps:
            cp.wait()


def attn_fwd(q_ext, kv_ext, kr, cs, tq, tk, comm=None):
    H, S, _ = q_ext.shape
    nq = S // tq
    nd = tq // tk
    nc = comm.n if comm is not None else 0
    HP = 2

    SL = min(32, tq)
    rep = tk // 128

    def body(*refs):
        ((q_ref, kv_ref, kr_ref, cs_ref), (o_ref, lse_ref, qf_ref),
         (s_scr, p_scr, m_scr, a_scr, acc_scr), cps) = _split_refs(refs, 4, 3, comm)
        hh = pl.program_id(0)
        qi = pl.program_id(1)
        _side_start(cps, jnp.logical_and(hh == 0, qi == 0))
        for e in range(HP):
            q = q_ref[e]
            hi = q[:, 128:256] * cs_ref[...]
            hi = hi + pltpu.roll(hi, 64, 1)
            hi = jnp.where(_lane_lt(hi.shape, 64), hi, 0.0)
            qf_ref[e] = jnp.concatenate([q[:, 0:128], hi], axis=1).astype(BF)
        m_scr[...] = jnp.full(m_scr.shape, -jnp.inf, F32)
        acc_scr[...] = jnp.zeros_like(acc_scr)
        ones = jnp.ones((tk, 128), BF)

        def step(kj, masked):
            off = pl.multiple_of(kj * tk, tk)
            kr_t = kr_ref[pl.ds(off, tk), :]
            for e in range(HP):
                k = jnp.concatenate([kv_ref[e, pl.ds(off, tk), 0:128], kr_t], axis=1)
                s_scr[e] = _dot(qf_ref[e], k, NT)
            for e in range(HP):
                for r in range(tq // SL):
                    rows = pl.ds(r * SL, SL)
                    s = s_scr[e, rows, :]
                    if masked:
                        rws = qi * tq + r * SL + lax.broadcasted_iota(jnp.int32, (SL, tk), 0)
                        cls = kj * tk + lax.broadcasted_iota(jnp.int32, (SL, tk), 1)
                        s = jnp.where((cls >> CHUNK_SHIFT) <= (rws >> CHUNK_SHIFT), s, -jnp.inf)
                    m_old = m_scr[e, rows, :]
                    m_new = jnp.maximum(m_old, jnp.max(s, axis=1, keepdims=True))
                    alpha = jnp.exp2((m_old - m_new) * EXP2_SCALE)
                    p = jnp.exp2(s * EXP2_SCALE - jnp.tile(m_new * EXP2_SCALE, (1, rep)))
                    m_scr[e, rows, :] = m_new
                    a_scr[e, rows, :] = alpha
                    p_scr[e, rows, :] = p.astype(BF)
            for e in range(HP):
                pv = _dot(p_scr[e], jnp.concatenate([kv_ref[e, pl.ds(off, tk), 128:256], ones], axis=1), NN)
                acc_scr[e] = jnp.tile(a_scr[e], (1, 2)) * acc_scr[e] + pv

        def trip(kj, c):
            step(kj, False)
            return c

        lax.fori_loop(0, qi * nd, trip, 0)
        for d in range(nd):
            step(qi * nd + d, True)
        for e in range(HP):
            l = acc_scr[e, :, 128:256]
            o_ref[:, e * 128:(e + 1) * 128] = acc_scr[e, :, 0:128] / l
            lse_ref[e] = m_scr[e] * SCALE + jnp.log(l)
        _side_wait(cps, jnp.logical_and(hh == H // HP - 1, qi == nq - 1))

    res = pl.pallas_call(
        body, name="attn_fwd" if comm is None else "attn_fwd_comm", grid=(H // HP, nq),
        in_specs=[pl.BlockSpec((HP, tq, 256), lambda h, i: (h, i, 0)),
                  pl.BlockSpec((HP, S, 256), lambda h, i: (h, 0, 0)),
                  pl.BlockSpec((S, 128), lambda h, i: (0, 0)),
                  pl.BlockSpec((tq, 128), lambda h, i: (i, 0))] + [ANY_SPEC] * nc,
        out_specs=[pl.BlockSpec((tq, HP * 128), lambda h, i: (i, h)),
                   pl.BlockSpec((HP, tq, 128), lambda h, i: (h, i, 0)),
                   pl.BlockSpec((HP, tq, 256), lambda h, i: (h, i, 0))] + [ANY_SPEC] * nc,
        out_shape=[_sds((S, H * 128), F32), _sds((H, S, 128), F32), _sds((H, S, 256), BF)]
        + (comm.out_shapes() if comm is not None else []),
        scratch_shapes=[pltpu.VMEM((HP, tq, tk), F32), pltpu.VMEM((HP, tq, tk), BF)]
        + [pltpu.VMEM((HP, tq, 128), F32)] * 2 + [pltpu.VMEM((HP, tq, 256), F32)]
        + (comm.scratch() if comm is not None else []),
        compiler_params=_params(("arbitrary", "arbitrary")),
    )(q_ext, kv_ext, kr, cs, *(comm.arrays if comm is not None else []))
    return res[:3], res[3:]


def attn_bwd(qf, kv_ext, kr, o, do, lse, cs, tq, tk, comm=None):
    H, S, _ = qf.shape
    nq = S // tq
    nd = tq // tk
    nc = comm.n if comm is not None else 0

    SL = min(64, tq)
    rep = tk // 128

    def body(*refs):
        ((q_ref, kv_ref, kr_ref, o_ref, do_ref, lse_ref, cs_ref), (dq_ref, dkv_ref, dkr_ref),
         (dk_acc, dv_acc, s_scr, dp_scr, p_scr, ds_scr, lse_scr, dl_scr, dob_scr, dq_scr), cps) = _split_refs(
             refs, 7, 3, comm)
        h = pl.program_id(0)
        qi = pl.program_id(1)
        _side_start(cps, jnp.logical_and(h == 0, qi == 0))

        @pl.when(qi == 0)
        def _():
            dk_acc[...] = jnp.zeros_like(dk_acc)
            dv_acc[...] = jnp.zeros_like(dv_acc)

        @pl.when(jnp.logical_and(h == 0, qi == 0))
        def _():
            dkr_ref[...] = jnp.zeros_like(dkr_ref)

        dov = do_ref[...]
        dl_scr[...] = jnp.broadcast_to(jnp.sum(dov * o_ref[...], axis=1, keepdims=True), (tq, 128))
        lse_scr[...] = lse_ref[...] * LOG2E
        dob_scr[...] = dov.astype(BF)
        dq_scr[...] = jnp.zeros_like(dq_scr)

        def step(kj, masked):
            k, v, off = _load_kv(kv_ref, kr_ref, kj, tk)
            s_scr[...] = _dot(q_ref[...], k, NT)
            dp_scr[...] = _dot(dob_scr[...], v, NT)
            for r in range(tq // SL):
                rows = pl.ds(r * SL, SL)
                s = s_scr[rows, :]
                if masked:
                    rws = qi * tq + r * SL + lax.broadcasted_iota(jnp.int32, (SL, tk), 0)
                    cls = kj * tk + lax.broadcasted_iota(jnp.int32, (SL, tk), 1)
                    s = jnp.where((cls >> CHUNK_SHIFT) <= (rws >> CHUNK_SHIFT), s, -jnp.inf)
                p = jnp.exp2(s * EXP2_SCALE - jnp.tile(lse_scr[rows, :], (1, rep)))
                p_scr[rows, :] = p.astype(BF)
                ds_scr[rows, :] = (p * (dp_scr[rows, :] - jnp.tile(dl_scr[rows, :], (1, rep)))).astype(BF)
            dv_acc[pl.ds(off, tk), :] += _dot_tn(p_scr[...], dob_scr[...])
            dk_acc[pl.ds(off, tk), :] += _dot_tn(ds_scr[...], q_ref[...])
            k2, _, _ = _load_kv(kv_ref, kr_ref, kj, tk)
            dq_scr[...] += _dot(ds_scr[...], k2, NN)

        def trip(kj, c):
            step(kj, False)
            return c

        lax.fori_loop(0, qi * nd, trip, 0)
        for d in range(nd):
            step(qi * nd + d, True)
        dq = dq_scr[...] * SCALE
        hi = dq[:, 128:256]
        hi = (hi + pltpu.roll(hi, 64, 1)) * cs_ref[...]
        dq_ref[...] = jnp.concatenate([dq[:, 0:128], hi], axis=1).astype(BF)

        @pl.when(qi == nq - 1)
        def _():
            dkv_ref[...] = jnp.concatenate([dk_acc[:, 0:128] * SCALE, dv_acc[...]], axis=1).astype(BF)
            dkr_ref[...] += dk_acc[:, 128:256] * SCALE

        _side_wait(cps, jnp.logical_and(h == H - 1, qi == nq - 1))

    res = pl.pallas_call(
        body, name="attn_bwd" if comm is None else "attn_bwd_comm", grid=(H, nq),
        in_specs=[pl.BlockSpec((None, tq, 256), lambda h, i: (h, i, 0)),
                  pl.BlockSpec((None, S, 256), lambda h, i: (h, 0, 0)),
                  pl.BlockSpec((S, 128), lambda h, i: (0, 0)),
                  pl.BlockSpec((tq, 128), lambda h, i: (i, h)),
                  pl.BlockSpec((tq, 128), lambda h, i: (i, h)),
                  pl.BlockSpec((None, tq, 128), lambda h, i: (h, i, 0)),
                  pl.BlockSpec((tq, 128), lambda h, i: (i, 0))] + [ANY_SPEC] * nc,
        out_specs=[pl.BlockSpec((None, tq, 256), lambda h, i: (h, i, 0)),
                   pl.BlockSpec((None, S, 256), lambda h, i: (h, 0, 0)),
                   pl.BlockSpec((S, 128), lambda h, i: (0, 0))] + [ANY_SPEC] * nc,
        out_shape=[_sds((H, S, 256), BF), _sds((H, S, 256), BF), _sds((S, 128), F32)]
        + (comm.out_shapes() if comm is not None else []),
        scratch_shapes=[pltpu.VMEM((S, 256), F32), pltpu.VMEM((S, 128), F32),
                        pltpu.VMEM((tq, tk), F32), pltpu.VMEM((tq, tk), F32),
                        pltpu.VMEM((tq, tk), BF), pltpu.VMEM((tq, tk), BF),
                        pltpu.VMEM((tq, 128), F32), pltpu.VMEM((tq, 128), F32),
                        pltpu.VMEM((tq, 128), BF), pltpu.VMEM((tq, 256), F32)]
        + (comm.scratch() if comm is not None else []),
        compiler_params=_params(("arbitrary", "arbitrary")),
    )(qf, kv_ext, kr, o, do, lse, cs, *(comm.arrays if comm is not None else []))
    return res[:3], res[3:]


def _log1p(y):
    w = 1.0 + y
    return jnp.where(w == 1.0, y, jnp.log(w) * (y / (w - 1.0)))


def _expm1(x):
    u = jnp.exp(x)
    return jnp.where(u == 1.0, x, (u - 1.0) * (x / jnp.log(u)))


def _softplus(x):
    return jnp.maximum(x, 0.0) + _log1p(jnp.exp(-jnp.abs(x)))


def _blockdiag(xb, w_ref):
    return jnp.concatenate(
        [_dot(xb[:, k * BD:(k + 1) * BD], w_ref[k], NN) for k in range(NBLK)], axis=1)


def _gates(xc, wr_ref, br_ref, wi_ref, bi_ref, sp):
    xb = xc.astype(BF)
    r = _sig(_blockdiag(xb, wr_ref) + br_ref[...])
    ig = _sig(_blockdiag(xb, wi_ref) + bi_ref[...])
    log_a = (-LRU_C * r) * sp
    mult = jnp.sqrt(-_expm1(2.0 * log_a))
    return xb, r, ig, log_a, mult


def _prev8_spec(tr, w, cb):
    return pl.BlockSpec((8, w), lambda i: (jnp.maximum(i * (tr // 8) - 1, 0), cb))


def lru_fwd(z, conv_w, conv_b, w_rg, b_rg, w_ig, b_ig, lam, tr):
    S = z.shape[0]
    W = D

    def body(u_ref, up_ref, cw_ref, cb_ref, wr_ref, br_ref, wi_ref, bi_ref, lam_ref,
             xc_ref, a_ref, h_ref, buf, bt, hcar):
        i = pl.program_id(0)

        @pl.when(i == 0)
        def _():
            buf[0:8, :] = jnp.zeros((8, W), F32)
            hcar[...] = jnp.zeros_like(hcar)

        @pl.when(i > 0)
        def _():
            buf[0:8, :] = up_ref[...]

        buf[8:8 + tr, :] = u_ref[...]
        cw = cw_ref[...]
        xc = buf[pl.ds(5, tr), :] * cw[0:1, :]
        for kk in range(1, 4):
            xc = xc + buf[pl.ds(5 + kk, tr), :] * cw[kk:kk + 1, :]
        xc = xc + cb_ref[...]
        xc_ref[...] = xc
        sp = _softplus(-lam_ref[...])
        _, _, ig, log_a, mult = _gates(xc, wr_ref, br_ref, wi_ref, bi_ref, sp)
        a_ref[...] = jnp.exp(log_a)
        bt[...] = mult * (ig * xc)
        row = lax.broadcasted_iota(jnp.int32, (8, W), 0)

        def grp(g, hp):
            off = pl.multiple_of(g * 8, 8)
            A = a_ref[pl.ds(off, 8), :]
            B = bt[pl.ds(off, 8), :]
            for d in (1, 2, 4):
                ok = row >= d
                B = jnp.where(ok, A * pltpu.roll(B, d, 0) + B, B)
                A = jnp.where(ok, A * pltpu.roll(A, d, 0), A)
            hh = A * hp + B
            h_ref[pl.ds(off, 8), :] = hh
            return hh[7:8, :]

        hcar[...] = lax.fori_loop(0, tr // 8, grp, hcar[...])

    return _rowcall(
        "lru_fwd", body, S, tr, [z, z, conv_w, conv_b, w_rg, b_rg, w_ig, b_ig, lam],
        [_rs(tr, W, ZB_U), _prev8_spec(tr, W, ZB_U), _fs((4, W)), _fs((1, W)), _fs((NBLK, BD, BD)), _fs((1, W)),
         _fs((NBLK, BD, BD)), _fs((1, W)), _fs((1, W))],
        (_sds((S, W), F32),) * 3, (_rs(tr, W),) * 3,
        scratch=[pltpu.VMEM((tr + 8, W), F32), pltpu.VMEM((tr, W), F32), pltpu.VMEM((1, W), F32)])


def lru_bwd(dh, a, hs, xc, w_rg, b_rg, w_ig, b_ig, lam, tr):
    S, W = dh.shape
    n = S // tr
    nb8 = S // 8

    def rev(i):
        return n - 1 - i

    row_spec = pl.BlockSpec((tr, W), lambda i: (rev(i), 0))
    next8 = pl.BlockSpec((8, W), lambda i: (jnp.minimum((rev(i) + 1) * (tr // 8), nb8 - 1), 0))
    prev8 = pl.BlockSpec((8, W), lambda i: (jnp.maximum(rev(i) * (tr // 8) - 1, 0), 0))

    def body(dh_ref, a_ref, an_ref, h_ref, hp_ref, xc_ref, wr_ref, br_ref, wi_ref, bi_ref, lam_ref,
             dxc_ref, dwr_ref, dwi_ref, dbr_ref, dbi_ref, dlam_ref, dcb_ref,
             bufa, bufh, apr, gsc, gcar, acc_br, acc_bi, acc_sp, acc_cb):
        i = pl.program_id(0)
        first = i == 0
        last_tile = i == n - 1

        bufa[0:tr, :] = a_ref[...]
        bufa[tr:tr + 8, :] = an_ref[...]
        apr[...] = bufa[pl.ds(1, tr), :]
        bufh[8:8 + tr, :] = h_ref[...]

        @pl.when(last_tile)
        def _():
            bufh[0:8, :] = jnp.zeros((8, W), F32)

        @pl.when(jnp.logical_not(last_tile))
        def _():
            bufh[0:8, :] = hp_ref[...]

        @pl.when(first)
        def _():
            gcar[...] = jnp.zeros_like(gcar)

        row = lax.broadcasted_iota(jnp.int32, (8, W), 0)
        ng = tr // 8

        def grp(t, gn):
            g = ng - 1 - t
            off = pl.multiple_of(g * 8, 8)
            A = apr[pl.ds(off, 8), :]
            B = dh_ref[pl.ds(off, 8), :]
            for d in (1, 2, 4):
                ok = row < 8 - d
                B = jnp.where(ok, B + A * pltpu.roll(B, 8 - d, 0), B)
                A = jnp.where(ok, A * pltpu.roll(A, 8 - d, 0), A)
            gg = B + A * gn
            gsc[pl.ds(off, 8), :] = gg
            return gg[0:1, :]

        gcar[...] = lax.fori_loop(0, ng, grp, gcar[...])

        G = gsc[...]
        hprev = bufh[pl.ds(7, tr), :]
        xcv = xc_ref[...]
        lam_v = lam_ref[...]
        sp = _softplus(-lam_v)
        xb, r, ig, log_a, mult = _gates(xcv, wr_ref, br_ref, wi_ref, bi_ref, sp)
        av = a_ref[...]
        d_a = G * hprev
        d_mult = G * (ig * xcv)
        d_ig = G * (mult * xcv)
        d_log_a = d_a * av - d_mult * (av * av / mult)
        d_gr = (d_log_a * (-LRU_C * sp)) * (r * (1.0 - r))
        d_gi = d_ig * (ig * (1.0 - ig))
        gr_b = d_gr.astype(BF)
        gi_b = d_gi.astype(BF)
        dxc = G * (mult * ig)
        dxc = dxc + jnp.concatenate(
            [_dot(gr_b[:, k * BD:(k + 1) * BD], wr_ref[k], NT)
             + _dot(gi_b[:, k * BD:(k + 1) * BD], wi_ref[k], NT) for k in range(NBLK)], axis=1)
        dxc_ref[...] = dxc

        @pl.when(first)
        def _():
            dwr_ref[...] = jnp.zeros_like(dwr_ref)
            dwi_ref[...] = jnp.zeros_like(dwi_ref)

        for k in range(NBLK):
            sl = slice(k * BD, (k + 1) * BD)
            dwr_ref[k] += _dot_tn(xb[:, sl], gr_b[:, sl])
            dwi_ref[k] += _dot_tn(xb[:, sl], gi_b[:, sl])

        _acc_rows(i, n, acc_br, dbr_ref, _fold8(d_gr))
        _acc_rows(i, n, acc_bi, dbi_ref, _fold8(d_gi))
        _acc_rows(i, n, acc_cb, dcb_ref, _fold8(dxc))

        part = _fold8(d_log_a * (-LRU_C * r))

        @pl.when(first)
        def _():
            acc_sp[...] = part

        @pl.when(jnp.logical_not(first))
        def _():
            acc_sp[...] += part

        @pl.when(last_tile)
        def _():
            dsp = jnp.sum(acc_sp[...], axis=0, keepdims=True)
            dlam_ref[...] = dsp * (-_sig(-lam_v))

    vec = _sds((1, W), F32)
    wsh = _sds((NBLK, BD, BD), F32)
    return _rowcall(
        "lru_bwd", body, S, tr, [dh, a, a, hs, hs, xc, w_rg, b_rg, w_ig, b_ig, lam],
        [row_spec, row_spec, next8, row_spec, prev8, row_spec, _fs((NBLK, BD, BD)), _fs((1, W)),
         _fs((NBLK, BD, BD)), _fs((1, W)), _fs((1, W))],
        (_sds((S, W), F32), wsh, wsh, vec, vec, vec, vec),
        (row_spec, _fs((NBLK, BD, BD)), _fs((NBLK, BD, BD)), _fs((1, W)), _fs((1, W)), _fs((1, W)), _fs((1, W))),
        scratch=[pltpu.VMEM((tr + 8, W), F32), pltpu.VMEM((tr + 8, W), F32), pltpu.VMEM((tr, W), F32),
                 pltpu.VMEM((tr, W), F32), pltpu.VMEM((1, W), F32), pltpu.VMEM((8, W), F32),
                 pltpu.VMEM((8, W), F32), pltpu.VMEM((8, W), F32), pltpu.VMEM((8, W), F32)])


def conv_bwd(dxc, z, conv_w, dz, tr):
    S, W = dxc.shape
    n = S // tr
    nb8 = S // 8

    def body(d_ref, dn_ref, u_ref, up_ref, cw_ref, _, du_ref, dcw_ref, bufd, bufu, acc):
        i = pl.program_id(0)
        bufd[0:tr, :] = d_ref[...]

        @pl.when(i == n - 1)
        def _():
            bufd[tr:tr + 8, :] = jnp.zeros((8, W), F32)

        @pl.when(i < n - 1)
        def _():
            bufd[tr:tr + 8, :] = dn_ref[...]

        @pl.when(i == 0)
        def _():
            bufu[0:8, :] = jnp.zeros((8, W), F32)

        @pl.when(i > 0)
        def _():
            bufu[0:8, :] = up_ref[...]

        bufu[8:8 + tr, :] = u_ref[...]
        cw = cw_ref[...]
        dv = d_ref[...]
        du = dv * cw[3:4, :]
        for j in range(1, 4):
            du = du + bufd[pl.ds(j, tr), :] * cw[3 - j:4 - j, :]
        du_ref[...] = du.astype(BF)
        parts = [jnp.sum(_fold8(dv * bufu[pl.ds(5 + kk, tr), :]), axis=0, keepdims=True) for kk in range(4)]
        part = jnp.concatenate(parts, axis=0)

        @pl.when(i == 0)
        def _():
            acc[...] = part

        @pl.when(i > 0)
        def _():
            acc[...] += part

        @pl.when(i == n - 1)
        def _():
            dcw_ref[...] = acc[...]

    next8 = pl.BlockSpec((8, W), lambda i: (jnp.minimum((i + 1) * (tr // 8), nb8 - 1), 0))
    return _rowcall(
        "conv_bwd", body, S, tr, [dxc, dxc, z, z, conv_w, dz],
        [_rs(tr, W), next8, _rs(tr, W, ZB_U), _prev8_spec(tr, W, ZB_U), _fs((4, W)), ANY_SPEC],
        (_sds((S, ZW), BF), _sds((4, W), F32)), (_dz_spec(tr, W, ZB_U * W), _fs((4, W))),
        scratch=[pltpu.VMEM((tr + 8, W), F32), pltpu.VMEM((tr + 8, W), F32), pltpu.VMEM((4, W), F32)],
        aliases={5: 0})


def _me():
    x = lax.axis_index("x")
    y = lax.axis_index("y")
    c = lax.axis_index("c")
    return x, y, c


def _peer(r):
    x, y, c = _me()
    px = jnp.bitwise_xor(x, (r >> 2) & 1)
    py = jnp.bitwise_xor(y, (r >> 1) & 1)
    pc = jnp.bitwise_xor(c, r & 1)
    return (px, py, pc), 4 * px + 2 * py + pc


class Comm:
    def __init__(self, arrays, modes):
        self.arrays = list(arrays)
        self.modes = list(modes)
        self.n = len(self.arrays)

    def out_shapes(self):
        return [_sds(((NDEV,) + a.shape) if md == "gather" else a.shape, a.dtype)
                for a, md in zip(self.arrays, self.modes)]

    def scratch(self):
        return [pltpu.SemaphoreType.DMA((self.n * (NDEV - 1),)),
                pltpu.SemaphoreType.DMA((self.n * (NDEV - 1),)),
                pltpu.SemaphoreType.DMA((self.n,))]

    def copies(self, ins, outs, send_sems, recv_sems, loc_sems):
        x, y, c = _me()
        me = 4 * x + 2 * y + c
        cps = []
        for ai, md in enumerate(self.modes):
            src = ins[ai] if md == "gather" else ins[ai].at[me]
            cps.append(pltpu.make_async_copy(src, outs[ai].at[me], loc_sems.at[ai]))
        for r in range(1, NDEV):
            dev, idx = _peer(r)
            for ai, md in enumerate(self.modes):
                src = ins[ai] if md == "gather" else ins[ai].at[idx]
                k = ai * (NDEV - 1) + r - 1
                cps.append(pltpu.make_async_remote_copy(
                    src_ref=src, dst_ref=outs[ai].at[me], send_sem=send_sems.at[k], recv_sem=recv_sems.at[k],
                    device_id=dev, device_id_type=pl.DeviceIdType.MESH))
        return cps


ANY_SPEC = pl.BlockSpec(memory_space=pl.ANY)


def comm_call(name, comm):
    na = comm.n

    def body(*refs):
        cps = comm.copies(refs[:na], refs[na:2 * na], *refs[2 * na:])
        for cp in cps:
            cp.start()
        for cp in cps:
            cp.wait()

    return pl.pallas_call(
        body, name=name, in_specs=[ANY_SPEC] * na, out_specs=[ANY_SPEC] * na, out_shape=comm.out_shapes(),
        scratch_shapes=comm.scratch(), compiler_params=pltpu.CompilerParams(has_side_effects=True),
    )(*comm.arrays)


def adamw(name, parts, w, m, v, tr):
    L = len(parts)
    _, R, C = parts[0].shape
    tr = _tile(R, tr)

    def body(*refs):
        p_refs = refs[:L]
        w_ref, m_ref, v_ref, g_ref, d_ref, nm_ref, nv_ref = refs[L:]
        for l in range(L):
            g = p_refs[l][0].astype(F32)
            for k in range(1, NDEV):
                g = g + p_refs[l][k].astype(F32)
            g_ref[l] = g
            mn = ADAM_B1 * m_ref[l] + (1.0 - ADAM_B1) * g
            vn = ADAM_B2 * v_ref[l] + (1.0 - ADAM_B2) * (g * g)
            m_hat = mn / (1.0 - ADAM_B1 ** ADAM_STEP)
            v_hat = vn / (1.0 - ADAM_B2 ** ADAM_STEP)
            d_ref[l] = -ADAM_LR * (m_hat / (jnp.sqrt(v_hat) + ADAM_EPS) + ADAM_WD * w_ref[l])
            nm_ref[l] = mn
            nv_ref[l] = vn

    blk = pl.BlockSpec((L, tr, C), lambda i: (0, i, 0))
    pblk = pl.BlockSpec((NDEV, tr, C), lambda i: (0, i, 0))
    return pl.pallas_call(
        body, name=name, grid=(R // tr,), in_specs=[pblk] * L + [blk, blk, blk],
        out_specs=(blk,) * 4, out_shape=(_sds((L, R, C), F32),) * 4,
        compiler_params=_params(("parallel",)),
    )(*parts, w, m, v)


def _rot(w):
    h = w.shape[-1] // 2
    return jnp.concatenate([-w[..., h:], w[..., :h]], axis=-1)


def _unrot(dw):
    h = dw.shape[-1] // 2
    return jnp.concatenate([dw[..., h:], -dw[..., :h]], axis=-1)


def _cols(g):
    n, R, C = g.shape
    return g.transpose(1, 0, 2).reshape(R, n * C)


def _rows(g):
    n, R, C = g.shape
    return g.reshape(n * R, C)


def _split_cols(dw):
    R, NC = dw.shape
    return dw.reshape(R, NDEV, NC // NDEV).transpose(1, 0, 2)


REPL = ["attn_norm", "q_a_norm", "kv_a_norm", "conv_b", "w_rg", "b_rg", "w_ig", "b_ig", "lru_lambda",
        "ple_norm", "final_norm"]
SHARDED = ["w_in", "w_q_b", "w_kv_b", "conv_w", "w_o_mla", "w_o_lru", "w_out", "w_ple_gate", "w_ple"]
WEIGHTS = ["attn_norm", "w_in", "q_a_norm", "w_q_b", "kv_a_norm", "w_kv_b", "conv_w", "conv_b", "w_rg", "b_rg",
           "w_ig", "b_ig", "lru_lambda", "w_o_mla", "w_o_lru", "w_out", "ple_norm", "w_ple_gate", "w_ple",
           "final_norm"]


GATE_W = ("w_rg", "w_ig")
REPL_LAYER = [k for k in REPL[1:-1] if k not in GATE_W]


def _pack(vals):
    flat = jnp.concatenate([v.reshape(-1) for v in vals])
    n = flat.shape[0]
    rows = -(-n // (128 * 256)) * 256
    return jnp.pad(flat, (0, rows * 128 - n)).reshape(rows, 128)


def _unpack(packed, shapes):
    flat = packed.reshape(-1)
    out = []
    off = 0
    for s in shapes:
        n = int(np.prod(s))
        out.append(flat[off:off + n].reshape(s))
        off += n
    return out


def kernel(x, p, positions, attn_norm, w_in, q_a_norm, w_q_b, kv_a_norm, w_kv_b, conv_w, conv_b, w_rg, b_rg, w_ig, b_ig, lru_lambda, w_o_mla, w_o_lru, w_out, ple_norm, w_ple_gate, w_ple, final_norm, loss_target, m_attn_norm, m_w_in, m_q_a_norm, m_w_q_b, m_kv_a_norm, m_w_kv_b, m_conv_w, m_conv_b, m_w_rg, m_b_rg, m_w_ig, m_b_ig, m_lru_lambda, m_w_o_mla, m_w_o_lru, m_w_out, m_ple_norm, m_w_ple_gate, m_w_ple, m_final_norm, v_attn_norm, v_w_in, v_q_a_norm, v_w_q_b, v_kv_a_norm, v_w_kv_b, v_conv_w, v_conv_b, v_w_rg, v_b_rg, v_w_ig, v_b_ig, v_lru_lambda, v_w_o_mla, v_w_o_lru, v_w_out, v_ple_norm, v_w_ple_gate, v_w_ple, v_final_norm):
    W = dict(attn_norm=attn_norm, w_in=w_in, q_a_norm=q_a_norm, w_q_b=w_q_b, kv_a_norm=kv_a_norm, w_kv_b=w_kv_b,
             conv_w=conv_w, conv_b=conv_b, w_rg=w_rg, b_rg=b_rg, w_ig=w_ig, b_ig=b_ig, lru_lambda=lru_lambda,
             w_o_mla=w_o_mla, w_o_lru=w_o_lru, w_out=w_out, ple_norm=ple_norm, w_ple_gate=w_ple_gate, w_ple=w_ple,
             final_norm=final_norm)
    M = dict(attn_norm=m_attn_norm, w_in=m_w_in, q_a_norm=m_q_a_norm, w_q_b=m_w_q_b, kv_a_norm=m_kv_a_norm,
             w_kv_b=m_w_kv_b, conv_w=m_conv_w, conv_b=m_conv_b, w_rg=m_w_rg, b_rg=m_b_rg, w_ig=m_w_ig, b_ig=m_b_ig,
             lru_lambda=m_lru_lambda, w_o_mla=m_w_o_mla, w_o_lru=m_w_o_lru, w_out=m_w_out, ple_norm=m_ple_norm,
             w_ple_gate=m_w_ple_gate, w_ple=m_w_ple, final_norm=m_final_norm)
    V = dict(attn_norm=v_attn_norm, w_in=v_w_in, q_a_norm=v_q_a_norm, w_q_b=v_w_q_b, kv_a_norm=v_kv_a_norm,
             w_kv_b=v_w_kv_b, conv_w=v_conv_w, conv_b=v_conv_b, w_rg=v_w_rg, b_rg=v_b_rg, w_ig=v_w_ig, b_ig=v_b_ig,
             lru_lambda=v_lru_lambda, w_o_mla=v_w_o_mla, w_o_lru=v_w_o_lru, w_out=v_w_out, ple_norm=v_ple_norm,
             w_ple_gate=v_w_ple_gate, w_ple=v_w_ple, final_norm=v_final_norm)

    L = w_in.shape[0]
    S = x.shape[1]
    xs = x[0]
    tgt = loss_target[0]
    tr = _tile(S, max(8, min(256, S // 2)))
    tq = _tile(S, max(256, min(1024, S // 2)))
    tk = max(128, min(512, tq // 2))

    inv_freq = ROPE_THETA ** (-jnp.arange(0, ROPE, 2, dtype=F32) / ROPE)
    ang = positions[0].astype(F32)[:, None] * inv_freq
    cs = jnp.concatenate([jnp.cos(ang), jnp.cos(ang), jnp.sin(ang), jnp.sin(ang)], axis=1)

    GATHERED = ["w_in", "w_q_b", "w_kv_b", "w_o_mla", "w_o_lru", "w_out", "w_ple_gate", "w_ple"]

    def gather_set(l, first=0):
        arrs = [W[k][l].astype(BF) for k in GATHERED] + [conv_w[l]]
        return Comm(arrs[first:], ["gather"] * (len(arrs) - first))

    def prep_win(g_in):
        win = _cols(g_in)
        kr_w = win[:, 1024:1088]
        return jnp.concatenate([win[:, 1088:3136], win[:, 5184:11328], win[:, 3136:5184], win[:, 0:1024], kr_w,
                                _rot(kr_w),
                                jnp.zeros((D, ZW - Z_KR - 128), BF)], axis=-1)

    def prep(g):
        g_qb, g_kvb, g_om, g_ol, g_out, g_pg, g_ple, g_cw = g[-8:]
        wq = _cols(g_qb).reshape(QL, NH, 192)
        d = dict(wq=jnp.concatenate([wq, _rot(wq[..., 128:])], axis=-1).reshape(QL, NH * 256),
                 wkv=_cols(g_kvb), wom=_rows(g_om), wol=_rows(g_ol), wout=_rows(g_out), wpg=_rows(g_pg),
                 wple=_cols(g_ple), cw=_cols(g_cw))
        if len(g) == 9:
            d["win"] = prep_win(g[0])
        return d

    wts = [None] * L
    win0 = prep_win(comm_call("gather_win0", Comm([w_in[0].astype(BF)], ["gather"]))[0])
    wrg_b = w_rg.astype(BF)
    wig_b = w_ig.astype(BF)

    def vec(a, l):
        return a[l][None, :]

    saved = []
    xcur = xs
    for l in range(L):
        h = norm_fwd("norm_in", xcur, vec(attn_norm, l), tr)
        if l == 0:
            z, got = mm("mm_in_comm", h, win0, "nn", tn=1280, tk=2048, comm=gather_set(0, first=1))
            wts[0] = dict(prep(got), win=win0)
        else:
            z = mm("mm_in", h, wts[l]["win"], "nn", tn=1280, tk=2048)
        wl = wts[l]
        qn, kvn, kr = qkv_prep(z, vec(q_a_norm, l), vec(kv_a_norm, l), cs, tr)
        q_ext = mm("mm_q", qn, wl["wq"], "nn", o_heads=True, tk=512)
        kv_ext = mm("mm_kv", kvn, wl["wkv"], "nn", o_heads=True, tk=512, out_dtype=BF)
        (o, lse, qf), got = attn_fwd(q_ext, kv_ext, kr, cs, tq, tk, gather_set(l + 1) if l + 1 < L else None)
        if l + 1 < L:
            wts[l + 1] = prep(got)
        xc, a, hs = lru_fwd(z, wl["cw"], vec(conv_b, l), wrg_b[l], vec(b_rg, l), wig_b[l], vec(b_ig, l),
                            vec(lru_lambda, l), tr)
        om, hl = gate_out(o, z, hs, tr)
        ym = mm("mm_om", om, wl["wom"], "nn", tk=2048)
        yl = mm("mm_ol", hl, wl["wol"], "nn", tk=2048)
        mg = merge_fwd(ym, yl, z, tr)
        x1 = mm("mm_out", mg, wl["wout"], "nn", tk=2048, res=xcur)
        hp = norm_fwd("norm_ple", x1, vec(ple_norm, l), tr)
        gp = mm("mm_pg", hp, wl["wpg"], "nn", tk=2048)
        pe = mm("mm_ple", p[l, 0], wl["wple"], "nn")
        x2 = ple_fin(x1, pe, gp, tr)
        saved.append(dict(x=xcur, h=h, z=z, qn=qn, kvn=kvn, kr=kr, kv_ext=kv_ext, o=o, lse=lse, qf=qf, xc=xc, a=a,
                          hs=hs, om=om, hl=hl, ym=ym, yl=yl, mg=mg, x1=x1, hp=hp, gp=gp, pe=pe))
        xcur = x2

    lsum, dx, d_final = final_loss(xcur, final_norm[None, :], tgt, tr)
    loss = lax.psum(0.5 * lsum[0, 0] / D, MESH_AXES)

    gr = {k: [None] * L for k in WEIGHTS if k != "final_norm"}
    landed_early = [None] * L
    landed_late = [None] * L
    zero_fn = jnp.zeros_like(final_norm)

    def early_grads(l):
        return ([gr[k][l].reshape(NDEV, D // NDEV, D) for k in ("w_o_mla", "w_o_lru", "w_out", "w_ple_gate")]
                + [_split_cols(gr["w_ple"][l])], ["a2a"] * 5)

    def late_grads(l):
        rep = _pack([gr[k][l] for k in REPL_LAYER] + [d_final[0] if l == L - 1 else zero_fn])
        return ([_split_cols(gr[k][l]) for k in ("w_in", "w_q_b", "w_kv_b", "conv_w")] + [rep]
                + [gr[k][l].reshape(NBLK * BD, BD) for k in GATE_W], ["a2a"] * 4 + ["gather"] * 3)

    for l in reversed(range(L)):
        sv = saved[l]
        wl = wts[l]
        z = sv["z"]
        dpe, dgp = ple_bwd(dx, sv["pe"], sv["gp"], tr)
        gr["w_ple"][l] = mm("mm_dple", p[l, 0], dpe, "tn", tm=256, tk=2048, out_dtype=BF)
        gr["w_ple_gate"][l] = mm("mm_dpg", sv["hp"], dgp, "tn", tk=2048, out_dtype=BF)
        dhp = mm("mm_dhp", dgp, wl["wpg"], "nt", tk=2048)
        dx1, gr["ple_norm"][l] = norm_bwd("norm_ple_bwd", sv["x1"], dhp, vec(ple_norm, l), dx, tr)
        dmg = mm("mm_dmg", dx1, wl["wout"], "nt", tk=2048)
        gr["w_out"][l] = mm("mm_dwout", sv["mg"], dx1, "tn", tk=2048, out_dtype=BF)
        dym, dyl, dz = merge_bwd(dmg, sv["ym"], sv["yl"], z, tr)
        dom = mm("mm_dom", dym, wl["wom"], "nt", tk=2048)
        gr["w_o_mla"][l] = mm("mm_dwom", sv["om"], dym, "tn", tk=2048, out_dtype=BF)
        dhl = mm("mm_dhl", dyl, wl["wol"], "nt", tk=2048)
        gr["w_o_lru"][l] = mm("mm_dwol", sv["hl"], dyl, "tn", tk=2048, out_dtype=BF)
        do, dhs, dz = gate_bwd(dom, sv["o"], dhl, sv["hs"], z, dz, tr)
        arrs, modes = early_grads(l)
        if l + 1 < L:
            arrs2, modes2 = late_grads(l + 1)
            arrs, modes = arrs + arrs2, modes + modes2
        (dq_ext, dkv_ext, dkr), got = attn_bwd(sv["qf"], sv["kv_ext"], sv["kr"], sv["o"], do, sv["lse"], cs, tq, tk,
                                               Comm(arrs, modes))
        landed_early[l] = got[:5]
        if l + 1 < L:
            landed_late[l + 1] = got[5:]
        dqn = mm("mm_dqn", dq_ext, wl["wq"], "nt", a_heads=True, tn=512, tk=2048)
        dwq_ext = mm("mm_dwq", sv["qn"], dq_ext, "tn", b_heads=True, tm=512, tk=2048, out_dtype=BF)
        dkn = mm("mm_dkn", dkv_ext, wl["wkv"], "nt", a_heads=True, tn=512, tk=2048)
        gr["w_kv_b"][l] = mm("mm_dwkv", sv["kvn"], dkv_ext, "tn", b_heads=True, tm=512, tk=2048, out_dtype=BF)
        dz, gr["q_a_norm"][l], gr["kv_a_norm"][l] = qkv_prep_bwd(
            z, dqn, dkn, dkr, vec(q_a_norm, l), vec(kv_a_norm, l), cs, dz, tr)
        dwq4 = dwq_ext.reshape(QL, NH, 256)
        gr["w_q_b"][l] = jnp.concatenate(
            [dwq4[..., 0:128], dwq4[..., 128:192] + _unrot(dwq4[..., 192:256])], axis=-1).reshape(QL, NH * 192)
        (dxc, gr["w_rg"][l], gr["w_ig"][l], gr["b_rg"][l], gr["b_ig"][l], gr["lru_lambda"][l],
         gr["conv_b"][l]) = lru_bwd(dhs, sv["a"], sv["hs"], sv["xc"], wrg_b[l], vec(b_rg, l), wig_b[l],
                                    vec(b_ig, l), vec(lru_lambda, l), tr)
        dz, gr["conv_w"][l] = conv_bwd(dxc, z, wl["cw"], dz, tr)
        dwin_ext = mm("mm_dwin", sv["h"], dz, "tn", tn=1280, tk=2048, out_dtype=BF)
        gr["w_in"][l] = jnp.concatenate(
            [dwin_ext[:, Z_Q:Z_KR], dwin_ext[:, Z_KR:Z_KR + 64] + _unrot(dwin_ext[:, Z_KR + 64:Z_KR + 128]),
             dwin_ext[:, 0:D], dwin_ext[:, 4 * D:5 * D], dwin_ext[:, D:4 * D]], axis=1)
        if l == 0:
            dh, landed_late[0] = mm("mm_dh_comm", dz, wl["win"], "nt", tk=2304, comm=Comm(*late_grads(0)))
        else:
            dh = mm("mm_dh", dz, wl["win"], "nt", tk=2304)
        dx, gr["attn_norm"][l] = norm_bwd("norm_in_bwd", sv["x"], dh, vec(attn_norm, l), dx1, tr)
    grad_x = dx[None]
    (landed_an,) = comm_call("gather_attn_norm", Comm([jnp.concatenate(gr["attn_norm"]).reshape(L * D // 128, 128)],
                                                      ["gather"]))

    res = {}
    for ki, k in enumerate(SHARDED):
        parts = [landed_late[l][ki] if ki < 4 else landed_early[l][ki - 4] for l in range(L)]
        res[k] = adamw("adamw_" + k, parts, W[k], M[k], V[k], 64)

    def rep_pack(T):
        fn = jnp.concatenate([jnp.zeros((L - 1, D), F32), T["final_norm"][None]], axis=0)
        flat = jnp.concatenate([T[k].reshape(L, -1) for k in REPL_LAYER] + [fn], axis=1)
        rows = landed_late[0][4].shape[1]
        return jnp.pad(flat, ((0, 0), (0, rows * 128 - flat.shape[1]))).reshape(L, rows, 128)

    rep = adamw("adamw_rep", [landed_late[l][4] for l in range(L)], rep_pack(W), rep_pack(M), rep_pack(V), 256)
    shapes = [W[k].shape[1:] for k in REPL_LAYER] + [final_norm.shape]
    per_layer = [[_unpack(t[l], shapes) for l in range(L)] for t in rep]
    for i, k in enumerate(REPL_LAYER):
        res[k] = tuple(jnp.stack([per_layer[t][l][i] for l in range(L)]) for t in range(4))
    res["final_norm"] = tuple(per_layer[t][L - 1][-1] for t in range(4))
    for gi, k in enumerate(GATE_W):
        gw = adamw("adamw_" + k, [landed_late[l][5 + gi] for l in range(L)],
                   *(T[k].reshape(L, NBLK * BD, BD) for T in (W, M, V)), 256)
        res[k] = tuple(t.reshape(W[k].shape) for t in gw)
    an = adamw("adamw_attn_norm", [landed_an], *(T["attn_norm"].reshape(1, L * D // 128, 128) for T in (W, M, V)), 64)
    res["attn_norm"] = tuple(t.reshape(L, D) for t in an)

    outs = [loss, grad_x]
    for t in range(4):
        outs += [res[k][t] for k in WEIGHTS]
    return tuple(outs)
```
